```python
import math
import jax, jax.numpy as jnp
from jax import lax
import numpy as np

D_MODEL = 2048
BATCH = 4
SEQ = 2048
DEPTH = 4
DEC_BATCH = 128
DEC_SEQ = 4
PAST_LEN = 16384
PAGE_SIZE = 128

D_GDN = 3 * D_MODEL // 8
D_SSD = 3 * D_MODEL // 8
D_POOL = D_MODEL - D_GDN - D_SSD
D_MIX = D_GDN + D_SSD + D_POOL

CONV_K = 4
GDN_HEAD_DIM = 128
GDN_HEADS = D_GDN // GDN_HEAD_DIM
GDN_CHUNK = 64

SSD_HEAD_DIM = 64
SSD_HEADS = D_SSD // SSD_HEAD_DIM
SSD_STATE = 128
SSD_GROUPS = 2
SSD_HEADS_PER_GROUP = SSD_HEADS // SSD_GROUPS
SSD_CHUNK = 128
SSD_CONV_CH = D_SSD + 2 * SSD_GROUPS * SSD_STATE

POOL_WINDOWS = (2, 4, 8, 16)
POOL_GROUPS = len(POOL_WINDOWS)
POOL_GROUP_DIM = D_POOL // POOL_GROUPS
POOL_BUF = max(POOL_WINDOWS) - 1

IN_SIZES = (3 * D_GDN, D_GDN, GDN_HEADS, GDN_HEADS, D_SSD, SSD_CONV_CH, SSD_HEADS, D_POOL, D_POOL)
D_IN_PROJ = sum(IN_SIZES)
EPS = 1e-6

kernel_name = 'hybrid_gdn_ssd_pool_decode_step'


def _rmsnorm(x, w):
    xf = x.astype(jnp.float32)
    y = xf * lax.rsqrt(jnp.mean(xf * xf, axis=-1, keepdims=True) + EPS)
    return (y * w.astype(jnp.float32)).astype(x.dtype)


def _l2norm(x):
    xf = x.astype(jnp.float32)
    return xf * lax.rsqrt(jnp.sum(xf * xf, axis=-1, keepdims=True) + EPS)


def _pad_time(a, n_pad):
    if n_pad == 0:
        return a
    pad = [(0, 0)] * a.ndim
    pad[1] = (0, n_pad)
    return jnp.pad(a, pad)


def _causal_conv(u, buf, w, b):
    t_len = u.shape[1]
    cat = jnp.concatenate([buf.astype(u.dtype), u], axis=1)
    out = sum(w[j] * cat[:, j:j + t_len] for j in range(CONV_K))
    if b is not None:
        out = out + b
    return out, cat[:, t_len:]


def _gated_delta_chunked(q, k, v, beta, g, s0):
    bsz, t_len, n_h, dv = v.shape
    c = min(GDN_CHUNK, t_len)
    n_pad = (-t_len) % c
    q, k, v, beta, g = [_pad_time(a.astype(jnp.float32), n_pad) for a in (q, k, v, beta, g)]
    n = (t_len + n_pad) // c

    def chunks(a):
        a = a.reshape((bsz, n, c, n_h) + a.shape[3:])
        return jnp.moveaxis(a, (1, 3), (0, 2))

    qc, kc, vc, bc, gc = chunks(q), chunks(k), chunks(v), chunks(beta), chunks(g)
    gcum = jnp.cumsum(gc, axis=-1)
    causal = jnp.tril(jnp.ones((c, c), bool))
    strict = jnp.tril(jnp.ones((c, c), bool), -1)
    diff = gcum[..., :, None] - gcum[..., None, :]
    decay = jnp.where(causal, jnp.exp(jnp.where(causal, diff, 0.0)), 0.0)
    kk = jnp.einsum('nbhid,nbhjd->nbhij', kc, kc)
    a_mat = jnp.where(strict, kk * decay * bc[..., :, None], 0.0)
    lhs = jnp.eye(c, dtype=jnp.float32) + a_mat
    rhs = jnp.concatenate([vc * bc[..., None], kc * (bc * jnp.exp(gcum))[..., None]], axis=-1)
    sol = lax.linalg.triangular_solve(lhs, rhs, left_side=True, lower=True, unit_diagonal=True)
    u_c, w_c = sol[..., :dv], sol[..., dv:]
    qk = jnp.where(causal, jnp.einsum('nbhid,nbhjd->nbhij', qc, kc) * decay, 0.0)
    q_dec = qc * jnp.exp(gcum)[..., None]
    k_dec = kc * jnp.exp(gcum[..., -1:] - gcum)[..., None]
    g_last = jnp.exp(gcum[..., -1])

    def step(s, inp):
        u_i, w_i, qk_i, q_i, k_i, gl_i = inp
        v_new = u_i - jnp.einsum('bhcd,bhde->bhce', w_i, s)
        o = jnp.einsum('bhcd,bhde->bhce', q_i, s) + jnp.einsum('bhij,bhje->bhie', qk_i, v_new)
        s = s * gl_i[..., None, None] + jnp.einsum('bhcd,bhce->bhde', k_i, v_new)
        return s, o

    s, o = lax.scan(step, s0.astype(jnp.float32), (u_c, w_c, qk, q_dec, k_dec, g_last))
    o = jnp.moveaxis(o, (0, 2), (1, 3)).reshape(bsz, n * c, n_h, dv)[:, :t_len]
    return o, s


def _ssd_chunked(x, dt, a, b_in, c_in, h0):
    bsz, t_len = x.shape[:2]
    c = min(SSD_CHUNK, t_len)
    n_pad = (-t_len) % c
    x, dt, b_in, c_in = [_pad_time(t.astype(jnp.float32), n_pad) for t in (x, dt, b_in, c_in)]
    n = (t_len + n_pad) // c
    gg, rr = SSD_GROUPS, SSD_HEADS_PER_GROUP

    def chunks(t):
        return jnp.swapaxes(t.reshape((bsz, n, c) + t.shape[2:]), 0, 1)

    xc = chunks(x).reshape(n, bsz, c, gg, rr, SSD_HEAD_DIM)
    dtc = chunks(dt).reshape(n, bsz, c, gg, rr)
    bc, cc = chunks(b_in), chunks(c_in)
    acum = jnp.cumsum(dtc * a.astype(jnp.float32).reshape(gg, rr), axis=2)
    causal = jnp.tril(jnp.ones((c, c), bool))[:, :, None, None]
    diff = acum[:, :, :, None] - acum[:, :, None, :]
    seg = jnp.where(causal, jnp.exp(jnp.where(causal, diff, 0.0)), 0.0)
    xdt = xc * dtc[..., None]
    cb = jnp.einsum('nbigs,nbjgs->nbijg', cc, bc)
    y_diag = jnp.einsum('nbijg,nbijgr,nbjgrp->nbigrp', cb, seg, xdt)
    decay_out = jnp.exp(acum)
    decay_in = jnp.exp(acum[:, :, -1:] - acum)
    decay_chunk = jnp.exp(acum[:, :, -1])

    def step(h, inp):
        c_i, b_i, xdt_i, dout_i, din_i, dch_i = inp
        y_off = jnp.einsum('bigs,bigr,bgrps->bigrp', c_i, dout_i, h)
        h = h * dch_i[..., None, None] + jnp.einsum('bjgs,bjgr,bjgrp->bgrps', b_i, din_i, xdt_i)
        return h, y_off

    h0 = h0.astype(jnp.float32).reshape(bsz, gg, rr, SSD_HEAD_DIM, SSD_STATE)
    h, y_off = lax.scan(step, h0, (cc, bc, xdt, decay_out, decay_in, decay_chunk))
    y = jnp.swapaxes(y_diag + y_off, 0, 1).reshape(bsz, n * c, SSD_HEADS, SSD_HEAD_DIM)[:, :t_len]
    return y, h.reshape(bsz, SSD_HEADS, SSD_HEAD_DIM, SSD_STATE)


def _pool_mix(u, buf, pos0, pool_w, pool_scale):
    t_len = u.shape[1]
    uf = u.astype(jnp.float32)
    cat = jnp.concatenate([buf.astype(jnp.float32), uf], axis=1)
    cs = jnp.pad(jnp.cumsum(cat, axis=1), ((0, 0), (1, 0), (0, 0)))
    pos = pos0 + jnp.arange(t_len)
    outs = []
    for gi, w in enumerate(POOL_WINDOWS):
        sl = slice(gi * POOL_GROUP_DIM, (gi + 1) * POOL_GROUP_DIM)
        hi = cs[:, POOL_BUF + 1:POOL_BUF + 1 + t_len, sl]
        lo = cs[:, POOL_BUF + 1 - w:POOL_BUF + 1 - w + t_len, sl]
        cnt = jnp.minimum(w, pos + 1).astype(jnp.float32)
        outs.append((hi - lo) / cnt[None, :, None])
    pooled = jnp.concatenate(outs, axis=-1) - uf
    pooled = pooled.reshape(u.shape[0], t_len, POOL_GROUPS, POOL_GROUP_DIM)
    mixed = jnp.einsum('btgc,gcd->btgd', pooled, pool_w.astype(jnp.float32))
    y = mixed.reshape(u.shape[0], t_len, D_POOL) * pool_scale.astype(jnp.float32)
    return y, cat[:, -POOL_BUF:].astype(buf.dtype)


def _mixer_layer(x, states, pos0, norm_w, w_in, gdn_conv_w, gdn_a_log, gdn_dt_bias, gdn_norm_w,
                 ssd_conv_w, ssd_conv_b, ssd_a_log, ssd_dt_bias, ssd_d, ssd_norm_w,
                 pool_w, pool_scale, w_out):
    gdn_s, gdn_buf, ssd_h, ssd_buf, pool_buf = states
    bsz, t_len, _ = x.shape
    h = _rmsnorm(x, norm_w)
    proj = jnp.einsum('btd,de->bte', h, w_in)
    offs = np.cumsum(IN_SIZES)[:-1].tolist()
    qkv, z_gdn, b_gdn, a_gdn, z_ssd, xbc, dt_ssd, u_pool, z_pool = jnp.split(proj, offs, axis=-1)

    qkv, gdn_buf_new = _causal_conv(qkv, gdn_buf, gdn_conv_w, None)
    qkv = jax.nn.silu(qkv)
    q, k, v = [qkv[..., i * D_GDN:(i + 1) * D_GDN].reshape(bsz, t_len, GDN_HEADS, GDN_HEAD_DIM)
               for i in range(3)]
    q = _l2norm(q) * (GDN_HEAD_DIM ** -0.5)
    k = _l2norm(k)
    beta = jax.nn.sigmoid(b_gdn.astype(jnp.float32))
    g = -jnp.exp(gdn_a_log.astype(jnp.float32)) * jax.nn.softplus(
        a_gdn.astype(jnp.float32) + gdn_dt_bias.astype(jnp.float32))
    o, gdn_s_new = _gated_delta_chunked(q, k, v, beta, g, gdn_s)
    z_g = z_gdn.reshape(bsz, t_len, GDN_HEADS, GDN_HEAD_DIM).astype(jnp.float32)
    y_gdn = (_rmsnorm(o, gdn_norm_w) * jax.nn.silu(z_g)).reshape(bsz, t_len, D_GDN)

    xbc, ssd_buf_new = _causal_conv(xbc, ssd_buf, ssd_conv_w, ssd_conv_b)
    xbc = jax.nn.silu(xbc)
    gs = SSD_GROUPS * SSD_STATE
    x_s = xbc[..., :D_SSD].reshape(bsz, t_len, SSD_HEADS, SSD_HEAD_DIM)
    b_s = xbc[..., D_SSD:D_SSD + gs].reshape(bsz, t_len, SSD_GROUPS, SSD_STATE)
    c_s = xbc[..., D_SSD + gs:].reshape(bsz, t_len, SSD_GROUPS, SSD_STATE)
    dt = jax.nn.softplus(dt_ssd.astype(jnp.float32) + ssd_dt_bias.astype(jnp.float32))
    a = -jnp.exp(ssd_a_log.astype(jnp.float32))
    y_s, ssd_h_new = _ssd_chunked(x_s, dt, a, b_s, c_s, ssd_h)
    y_s = y_s + ssd_d.astype(jnp.float32)[:, None] * x_s.astype(jnp.float32)
    y_s = y_s.reshape(bsz, t_len, D_SSD) * jax.nn.silu(z_ssd.astype(jnp.float32))
    y_ssd = _rmsnorm(y_s.reshape(bsz, t_len, SSD_GROUPS, D_SSD // SSD_GROUPS),
                     ssd_norm_w.reshape(SSD_GROUPS, D_SSD // SSD_GROUPS)).reshape(bsz, t_len, D_SSD)

    y_p, pool_buf_new = _pool_mix(u_pool, pool_buf, pos0, pool_w, pool_scale)
    y_pool = y_p * jax.nn.silu(z_pool.astype(jnp.float32))

    y_cat = jnp.concatenate([y_gdn, y_ssd, y_pool], axis=-1).astype(x.dtype)
    x = x + jnp.einsum('bte,ed->btd', y_cat, w_out)
    new_states = (gdn_s_new.astype(gdn_s.dtype), gdn_buf_new.astype(gdn_buf.dtype),
                  ssd_h_new.astype(ssd_h.dtype), ssd_buf_new.astype(ssd_buf.dtype), pool_buf_new)
    return x, new_states


def _dt_bias(key, shape):
    dt = jnp.exp(jax.random.uniform(key, shape, jnp.float32, math.log(1e-3), math.log(1e-1)))
    return dt + jnp.log(-jnp.expm1(-dt))


def setup_inputs(seed: int = 0) -> dict:
    key = jax.random.key(seed)
    ks = list(jax.random.split(key, 24))
    f32 = jnp.float32

    def nrm(i, shape, scale):
        return jax.random.normal(ks[i], shape, f32) * scale

    return {
        'x_prompt': nrm(0, (BATCH, SEQ, D_MODEL), 1.0),
        'x_sample': nrm(1, (DEC_BATCH, DEC_SEQ, D_MODEL), 1.0),
        'state_gdn': nrm(2, (DEPTH, DEC_BATCH, GDN_HEADS, GDN_HEAD_DIM, GDN_HEAD_DIM), 0.1),
        'state_gdn_conv': nrm(3, (DEPTH, DEC_BATCH, CONV_K - 1, 3 * D_GDN), 1.0),
        'state_ssd': nrm(4, (DEPTH, DEC_BATCH, SSD_HEADS, SSD_HEAD_DIM, SSD_STATE), 0.1),
        'state_ssd_conv': nrm(5, (DEPTH, DEC_BATCH, CONV_K - 1, SSD_CONV_CH), 1.0),
        'state_pool': nrm(6, (DEPTH, DEC_BATCH, POOL_BUF, D_POOL), 1.0),
        'norm_w': 1.0 + nrm(7, (DEPTH, D_MODEL), 0.02),
        'w_in': nrm(8, (DEPTH, D_MODEL, D_IN_PROJ), D_MODEL ** -0.5),
        'gdn_conv_w': nrm(9, (DEPTH, CONV_K, 3 * D_GDN), CONV_K ** -0.5),
        'gdn_a_log': jnp.log(jax.random.uniform(ks[10], (DEPTH, GDN_HEADS), f32, 1.0, 16.0)),
        'gdn_dt_bias': _dt_bias(ks[11], (DEPTH, GDN_HEADS)),
        'gdn_norm_w': 1.0 + nrm(12, (DEPTH, GDN_HEAD_DIM), 0.02),
        'ssd_conv_w': nrm(13, (DEPTH, CONV_K, SSD_CONV_CH), CONV_K ** -0.5),
        'ssd_conv_b': nrm(14, (DEPTH, SSD_CONV_CH), 0.01),
        'ssd_a_log': jnp.log(jax.random.uniform(ks[15], (DEPTH, SSD_HEADS), f32, 1.0, 16.0)),
        'ssd_dt_bias': _dt_bias(ks[16], (DEPTH, SSD_HEADS)),
        'ssd_d': 1.0 + nrm(17, (DEPTH, SSD_HEADS), 0.02),
        'ssd_norm_w': 1.0 + nrm(18, (DEPTH, D_SSD), 0.02),
        'pool_w': nrm(19, (DEPTH, POOL_GROUPS, POOL_GROUP_DIM, POOL_GROUP_DIM), POOL_GROUP_DIM ** -0.5),
        'pool_scale': 1.0 + nrm(20, (DEPTH, D_POOL), 0.02),
        'w_out': nrm(21, (DEPTH, D_MIX, D_MODEL), D_MIX ** -0.5),
        'final_norm_w': 1.0 + nrm(22, (D_MODEL,), 0.02),
    }


def reference(x_prompt, x_sample, state_gdn, state_gdn_conv, state_ssd, state_ssd_conv, state_pool,
              norm_w, w_in, gdn_conv_w, gdn_a_log, gdn_dt_bias, gdn_norm_w,
              ssd_conv_w, ssd_conv_b, ssd_a_log, ssd_dt_bias, ssd_d, ssd_norm_w,
              pool_w, pool_scale, w_out, final_norm_w):
    dtp = x_prompt.dtype
    zero_states = (
        jnp.zeros((BATCH, GDN_HEADS, GDN_HEAD_DIM, GDN_HEAD_DIM), dtp),
        jnp.zeros((BATCH, CONV_K - 1, 3 * D_GDN), dtp),
        jnp.zeros((BATCH, SSD_HEADS, SSD_HEAD_DIM, SSD_STATE), dtp),
        jnp.zeros((BATCH, CONV_K - 1, SSD_CONV_CH), dtp),
        jnp.zeros((BATCH, POOL_BUF, D_POOL), dtp),
    )
    xp, xs = x_prompt, x_sample
    new_p, new_s = [], []
    for l in range(DEPTH):
        lw = (norm_w[l], w_in[l], gdn_conv_w[l], gdn_a_log[l], gdn_dt_bias[l], gdn_norm_w[l],
              ssd_conv_w[l], ssd_conv_b[l], ssd_a_log[l], ssd_dt_bias[l], ssd_d[l], ssd_norm_w[l],
              pool_w[l], pool_scale[l], w_out[l])
        xp, sp = _mixer_layer(xp, zero_states, 0, *lw)
        past = (state_gdn[l], state_gdn_conv[l], state_ssd[l], state_ssd_conv[l], state_pool[l])
        xs, ss = _mixer_layer(xs, past, PAST_LEN, *lw)
        new_p.append(sp)
        new_s.append(ss)
    y_prompt = _rmsnorm(xp, final_norm_w)
    y_sample = _rmsnorm(xs, final_norm_w)
    new_gdn_p = jnp.stack([s[0] for s in new_p])
    new_gdn_conv_p = jnp.stack([s[1] for s in new_p])
    new_ssd_p = jnp.stack([s[2] for s in new_p])
    new_ssd_conv_p = jnp.stack([s[3] for s in new_p])
    new_pool_p = jnp.stack([s[4] for s in new_p])
    new_gdn_s = jnp.stack([s[0] for s in new_s])
    new_gdn_conv_s = jnp.stack([s[1] for s in new_s])
    new_ssd_s = jnp.stack([s[2] for s in new_s])
    new_ssd_conv_s = jnp.stack([s[3] for s in new_s])
    new_pool_s = jnp.stack([s[4] for s in new_s])
    return (y_prompt, y_sample, new_gdn_p, new_gdn_conv_p, new_ssd_p, new_ssd_conv_p, new_pool_p,
            new_gdn_s, new_gdn_conv_s, new_ssd_s, new_ssd_conv_s, new_pool_s)
```

```python
import functools

import jax
import jax.numpy as jnp
import numpy as np
from jax import lax
from jax.experimental import pallas as pl
from jax.experimental.pallas import tpu as pltpu

F32 = jnp.float32
BF16 = jnp.bfloat16

D_MODEL = 2048
DEPTH = 4
PAST_LEN = 16384
D_GDN = 768
D_SSD = 768
D_POOL = 512
CONV_K = 4
GDN_HEAD_DIM = 128
GDN_HEADS = 6
GDN_CHUNK = 64
SSD_HEAD_DIM = 64
SSD_HEADS = 12
SSD_STATE = 128
SSD_GROUPS = 2
SSD_HEADS_PER_GROUP = 6
SSD_GROUP_DIM = D_SSD // SSD_GROUPS
SSD_CHUNK = 128
SSD_CONV_CH = 1280
POOL_WINDOWS = (2, 4, 8, 16)
POOL_GROUP_DIM = 128
POOL_BUF = 15
IN_SIZES = (3 * D_GDN, D_GDN, GDN_HEADS, GDN_HEADS, D_SSD, SSD_CONV_CH, SSD_HEADS, D_POOL, D_POOL)
EPS = 1e-6

C_QKV = 0
C_ZG = 2304
C_ZS = 3072
C_XBC = 3840
C_UP = 5120
C_ZP = 5632
C_SM = 6144
D_PACK = 6272
L_BETA = 0
L_GDN_G = 6
L_SSD_DT = 12

LANE = 128
SUBLANE = 8
VMEM_LIMIT = 48 * 1024 * 1024

PROJ_TN = 896
PROMPT_TT = 128
SAMPLE_NB = 8
SAMPLE_T = 4
SEQ_TILE = 8


def _dot(a, b):
    return jnp.dot(a, b, preferred_element_type=F32)


def _dot_nt(a, b):
    return lax.dot_general(a, b, (((1,), (1,)), ((), ())), preferred_element_type=F32)


def _dot_tn(a, b):
    return lax.dot_general(a, b, (((0,), (0,)), ((), ())), preferred_element_type=F32)


def _sigmoid(x):
    return 1.0 / (1.0 + jnp.exp(-x))


def _silu(x):
    return x * _sigmoid(x)


def _softplus(x):
    return jnp.maximum(x, 0.0) + jnp.log1p(jnp.exp(-jnp.abs(x)))


def _bf(x):
    return x.astype(BF16)


def _inproj_kernel(x_ref, nw_ref, w_ref, o_ref, h_ref):
    @pl.when(pl.program_id(1) == 0)
    def _():
        x = x_ref[...]
        ms = jnp.mean(x * x, axis=-1, keepdims=True)
        h_ref[...] = _bf(x * lax.rsqrt(ms + EPS) * nw_ref[...])

    o_ref[...] = _dot(h_ref[...], w_ref[...])


def _inproj(x, nw, w_packed, layer, tm):
    n = x.shape[0]
    return pl.pallas_call(
        _inproj_kernel,
        grid=(n // tm, D_PACK // PROJ_TN),
        in_specs=[
            pl.BlockSpec((tm, D_MODEL), lambda i, j: (i, 0)),
            pl.BlockSpec((None, 1, D_MODEL), lambda i, j: (layer, 0, 0)),
            pl.BlockSpec((None, D_MODEL, PROJ_TN), lambda i, j: (layer, 0, j)),
        ],
        out_specs=pl.BlockSpec((tm, PROJ_TN), lambda i, j: (i, j)),
        out_shape=jax.ShapeDtypeStruct((n, D_PACK), F32),
        scratch_shapes=[pltpu.VMEM((tm, D_MODEL), BF16)],
        compiler_params=pltpu.CompilerParams(
            dimension_semantics=("parallel", "arbitrary"), vmem_limit_bytes=VMEM_LIMIT),
        name="inproj",
    )(x, nw, w_packed)


def _outproj_kernel(y_ref, w_ref, x_ref, o_ref):
    o_ref[...] = x_ref[...] + _dot(_bf(y_ref[...]), w_ref[...])


def _outproj_final_kernel(y_ref, w_ref, x_ref, fw_ref, o_ref):
    x = x_ref[...] + _dot(_bf(y_ref[...]), w_ref[...])
    ms = jnp.mean(x * x, axis=-1, keepdims=True)
    o_ref[...] = x * lax.rsqrt(ms + EPS) * fw_ref[...]


def _outproj(y, w_out, x, layer, tm, final_w=None):
    n = x.shape[0]
    in_specs = [
        pl.BlockSpec((tm, D_MODEL), lambda i: (i, 0)),
        pl.BlockSpec((None, D_MODEL, D_MODEL), lambda i: (layer, 0, 0)),
        pl.BlockSpec((tm, D_MODEL), lambda i: (i, 0)),
    ]
    args = [y, w_out, x]
    body = _outproj_kernel
    if final_w is not None:
        in_specs.append(pl.BlockSpec((1, D_MODEL), lambda i: (0, 0)))
        args.append(final_w)
        body = _outproj_final_kernel
    return pl.pallas_call(
        body,
        grid=(n // tm,),
        in_specs=in_specs,
        out_specs=pl.BlockSpec((tm, D_MODEL), lambda i: (i, 0)),
        out_shape=jax.ShapeDtypeStruct((n, D_MODEL), F32),
        compiler_params=pltpu.CompilerParams(
            dimension_semantics=("parallel",), vmem_limit_bytes=VMEM_LIMIT),
        name="outproj",
    )(*args)


def _conv_silu(ext_ref, r0, rows, c0, width, w_ref, wc0, b_ref=None):
    acc = None
    for j in range(CONV_K):
        lo = r0 - (CONV_K - 1) + j
        term = ext_ref[lo:lo + rows, c0:c0 + width] * w_ref[j:j + 1, wc0:wc0 + width]
        acc = term if acc is None else acc + term
    if b_ref is not None:
        acc = acc + b_ref[0:1, wc0:wc0 + width]
    return _silu(acc)


def _l2norm(x):
    return x * lax.rsqrt(jnp.sum(x * x, axis=-1, keepdims=True) + EPS)


def _seg_cumsum(x, seg_len, row_in_seg):
    s = 1
    while s < seg_len:
        x = x + jnp.where(row_in_seg >= s, pltpu.roll(x, s, 0), 0.0)
        s *= 2
    return x


def _gates(small, gp_ref):
    sp = _softplus(small + gp_ref[1:2, :])
    gd = -jnp.exp(gp_ref[0:1, :]) * sp
    return _sigmoid(small), sp, gd


def _masked_decay(col, row, causal):
    return jnp.where(causal, jnp.exp(jnp.where(causal, col - row, 0.0)), 0.0)


def _gdn_head(qh, kh, vh, beta_c, gc_c, gc_r, causal, strict, valid_c, n_levels, seqs, get_state, set_state):
    eg = jnp.exp(gc_c)
    decay = _masked_decay(gc_c, gc_r, causal)
    kb, qb = _bf(kh), _bf(qh)
    kk = _dot_nt(kb, kb)
    nmat = jnp.where(strict, -(kk * decay * beta_c), 0.0)
    p = nmat
    for _ in range(n_levels):
        pb = _bf(p)
        p = _dot(pb, pb)
        nmat = nmat + p + _dot(_bf(nmat), _bf(p))
    rhs = jnp.concatenate([vh * beta_c, kh * (beta_c * eg)], axis=-1)
    sol = rhs + _dot(_bf(nmat), _bf(rhs))
    u, w = sol[:, :GDN_HEAD_DIM], sol[:, GDN_HEAD_DIM:]
    qk = jnp.where(causal, _dot_nt(qb, kb) * decay, 0.0)
    qdec = qh * eg
    o_parts, v_parts = [], []
    for (s0, sl, last, idx) in seqs:
        state = get_state(idx)
        sb = _bf(state)
        glast = gc_c[last:last + 1, :]
        vn = u[s0:s0 + sl] - _dot(_bf(w[s0:s0 + sl]), sb)
        kd = kh[s0:s0 + sl] * jnp.exp(glast - gc_c[s0:s0 + sl])
        if valid_c is not None:
            vn = jnp.where(valid_c[s0:s0 + sl], vn, 0.0)
            kd = jnp.where(valid_c[s0:s0 + sl], kd, 0.0)
        o_parts.append(_dot(_bf(qdec[s0:s0 + sl]), sb))
        v_parts.append(vn)
        set_state(idx, state * jnp.exp(glast) + _dot_tn(_bf(kd), _bf(vn)))
    o_state = o_parts[0] if len(o_parts) == 1 else jnp.concatenate(o_parts, axis=0)
    v_new = v_parts[0] if len(v_parts) == 1 else jnp.concatenate(v_parts, axis=0)
    return o_state + _dot(_bf(qk), _bf(v_new))


def _ssd_head(xs, dt_c, a_c, a_r, cb, cg, bg, causal, valid_c, seqs, get_state, set_state):
    seg = _masked_decay(a_c, a_r, causal)
    xdt = xs * dt_c
    y = _dot(_bf(cb * seg), _bf(xdt))
    cgb, bgb = _bf(cg), _bf(bg)
    ea = jnp.exp(a_c)
    parts = []
    for (s0, sl, last, idx) in seqs:
        state = get_state(idx)
        alast = a_c[last:last + 1, :]
        xin = xdt[s0:s0 + sl] * jnp.exp(alast - a_c[s0:s0 + sl])
        if valid_c is not None:
            xin = jnp.where(valid_c[s0:s0 + sl], xin, 0.0)
        parts.append(ea[s0:s0 + sl] * _dot_nt(cgb[s0:s0 + sl], _bf(state)))
        set_state(idx, state * jnp.exp(alast) + _dot_tn(_bf(xin), bgb[s0:s0 + sl]))
    y_off = parts[0] if len(parts) == 1 else jnp.concatenate(parts, axis=0)
    return y + y_off


def _gdn_finish(o, z, gnorm_ref):
    on = o * lax.rsqrt(jnp.mean(o * o, axis=-1, keepdims=True) + EPS) * gnorm_ref[0:1, :]
    return on * _silu(z)


def _ssd_finish(ys_ref, xs_ref, z_of, sd_ref, snorm_ref, store):
    for g in range(SSD_GROUPS):
        c0 = g * SSD_GROUP_DIM
        sl = slice(c0, c0 + SSD_GROUP_DIM)
        yz = (ys_ref[:, sl] + sd_ref[0:1, sl] * xs_ref[:, sl]) * _silu(z_of(c0, SSD_GROUP_DIM))
        yn = yz * lax.rsqrt(jnp.mean(yz * yz, axis=-1, keepdims=True) + EPS) * snorm_ref[0:1, sl]
        store(D_GDN + c0, yn)


def _mixer_prompt_kernel(proj_ref, gp_ref, gcw_ref, gnorm_ref, scw_ref, scb_ref, sd_ref, snorm_ref,
                         pw_ref, pscale_ref,
                         y_ref, sg_ref, ss_ref,
                         eq_ref, ex_ref, ep_ref, ys_ref, xs_ref, *, pos0):
    tt = PROMPT_TT
    i = pl.program_id(1)

    @pl.when(i == 0)
    def _():
        sg_ref[...] = jnp.zeros(sg_ref.shape, F32)
        ss_ref[...] = jnp.zeros(ss_ref.shape, F32)
        eq_ref[0:SUBLANE, :] = jnp.zeros((SUBLANE, eq_ref.shape[1]), F32)
        ex_ref[0:SUBLANE, :] = jnp.zeros((SUBLANE, ex_ref.shape[1]), F32)
        ep_ref[0:2 * SUBLANE, :] = jnp.zeros((2 * SUBLANE, D_POOL), F32)

    eq_ref[SUBLANE:SUBLANE + tt, :] = proj_ref[:, C_QKV:C_QKV + 3 * D_GDN]
    ex_ref[SUBLANE:SUBLANE + tt, :] = proj_ref[:, C_XBC:C_XBC + SSD_CONV_CH]
    ep_ref[2 * SUBLANE:2 * SUBLANE + tt, :] = proj_ref[:, C_UP:C_UP + D_POOL]

    beta, sp, gd = _gates(proj_ref[:, C_SM:C_SM + LANE], gp_ref)
    row = lax.broadcasted_iota(jnp.int32, (tt, LANE), 0)
    gc = _seg_cumsum(gd, GDN_CHUNK, row & (GDN_CHUNK - 1))
    ac = _seg_cumsum(gd, SSD_CHUNK, row)

    def store_y(c0, val, r0=0):
        y_ref[r0:r0 + val.shape[0], c0:c0 + val.shape[1]] = val.astype(y_ref.dtype)

    ci = lax.broadcasted_iota(jnp.int32, (GDN_CHUNK, GDN_CHUNK), 0)
    cj = lax.broadcasted_iota(jnp.int32, (GDN_CHUNK, GDN_CHUNK), 1)
    causal_g, strict_g = ci >= cj, ci > cj

    def get_sg(h):
        return sg_ref[0, h]

    def set_sg(h, val):
        sg_ref[0, h] = val

    for c in range(tt // GDN_CHUNK):
        r0 = c * GDN_CHUNK
        gc_blk = gc[r0:r0 + GDN_CHUNK]
        gc_t = gc_blk.T
        for h in range(GDN_HEADS):
            conv = functools.partial(_conv_silu, eq_ref, SUBLANE + r0, GDN_CHUNK)
            c0 = h * GDN_HEAD_DIM
            qh = _l2norm(conv(c0, GDN_HEAD_DIM, gcw_ref, c0)) * (GDN_HEAD_DIM ** -0.5)
            kh = _l2norm(conv(D_GDN + c0, GDN_HEAD_DIM, gcw_ref, D_GDN + c0))
            vh = conv(2 * D_GDN + c0, GDN_HEAD_DIM, gcw_ref, 2 * D_GDN + c0)
            o = _gdn_head(
                qh, kh, vh,
                beta[r0:r0 + GDN_CHUNK, L_BETA + h:L_BETA + h + 1],
                gc_blk[:, L_GDN_G + h:L_GDN_G + h + 1],
                gc_t[L_GDN_G + h:L_GDN_G + h + 1, :],
                causal_g, strict_g, None, 5,
                [(0, GDN_CHUNK, GDN_CHUNK - 1, h)], get_sg, set_sg)
            z = proj_ref[r0:r0 + GDN_CHUNK, C_ZG + c0:C_ZG + c0 + GDN_HEAD_DIM]
            store_y(c0, _gdn_finish(o, z, gnorm_ref), r0)

    si = lax.broadcasted_iota(jnp.int32, (SSD_CHUNK, SSD_CHUNK), 0)
    sj = lax.broadcasted_iota(jnp.int32, (SSD_CHUNK, SSD_CHUNK), 1)
    causal_s = si >= sj
    ac_t = ac.T

    def get_ss(hd):
        return ss_ref[0, hd]

    def set_ss(hd, val):
        ss_ref[0, hd] = val

    conv_x = functools.partial(_conv_silu, ex_ref, SUBLANE, tt)
    for g in range(SSD_GROUPS):
        bg = conv_x(D_SSD + g * SSD_STATE, SSD_STATE, scw_ref, D_SSD + g * SSD_STATE, scb_ref)
        cg = conv_x(D_SSD + (SSD_GROUPS + g) * SSD_STATE, SSD_STATE, scw_ref,
                    D_SSD + (SSD_GROUPS + g) * SSD_STATE, scb_ref)
        cb = _dot_nt(_bf(cg), _bf(bg))
        for pair in range(SSD_HEADS_PER_GROUP // 2):
            c0 = (g * SSD_HEADS_PER_GROUP + 2 * pair) * SSD_HEAD_DIM
            xs2 = conv_x(c0, 2 * SSD_HEAD_DIM, scw_ref, c0, scb_ref)
            xs_ref[:, c0:c0 + 2 * SSD_HEAD_DIM] = xs2
            for half in range(2):
                hd = g * SSD_HEADS_PER_GROUP + 2 * pair + half
                lane = L_SSD_DT + hd
                y = _ssd_head(
                    xs2[:, half * SSD_HEAD_DIM:(half + 1) * SSD_HEAD_DIM],
                    sp[:, lane:lane + 1], ac[:, lane:lane + 1], ac_t[lane:lane + 1, :],
                    cb, cg, bg, causal_s, None,
                    [(0, SSD_CHUNK, SSD_CHUNK - 1, hd)], get_ss, set_ss)
                ys_ref[:, hd * SSD_HEAD_DIM:(hd + 1) * SSD_HEAD_DIM] = y

    _ssd_finish(ys_ref, xs_ref, lambda c0, w: proj_ref[:, C_ZS + c0:C_ZS + c0 + w], sd_ref, snorm_ref, store_y)

    pos = pos0 + i * tt + lax.broadcasted_iota(jnp.int32, (tt, 1), 0)
    for gi, win in enumerate(POOL_WINDOWS):
        sl = slice(gi * POOL_GROUP_DIM, (gi + 1) * POOL_GROUP_DIM)
        acc = ep_ref[2 * SUBLANE:2 * SUBLANE + tt, sl]
        u = acc
        for k in range(1, win):
            acc = acc + ep_ref[2 * SUBLANE - k:2 * SUBLANE - k + tt, sl]
        cnt = jnp.minimum(win, pos + 1).astype(F32)
        pooled = acc / cnt - u
        mixed = _dot(_bf(pooled), pw_ref[gi])
        z = proj_ref[:, C_ZP + gi * POOL_GROUP_DIM:C_ZP + (gi + 1) * POOL_GROUP_DIM]
        store_y(D_GDN + D_SSD + gi * POOL_GROUP_DIM, mixed * pscale_ref[0:1, sl] * _silu(z))

    eq_ref[0:SUBLANE, :] = eq_ref[tt:tt + SUBLANE, :]
    ex_ref[0:SUBLANE, :] = ex_ref[tt:tt + SUBLANE, :]
    ep_ref[0:2 * SUBLANE, :] = ep_ref[tt:tt + 2 * SUBLANE, :]


def _layer_spec(shape):
    nd = len(shape)
    return lambda layer: pl.BlockSpec((None,) + tuple(shape[1:]), lambda *_: (layer,) + (0,) * (nd - 1))


def _param_specs(params, layer):
    return [_layer_spec(p.shape)(layer) for p in params]


def _mixer_prompt(proj, params, layer, batch, seq):
    tt = PROMPT_TT
    nt = seq // tt
    return pl.pallas_call(
        functools.partial(_mixer_prompt_kernel, pos0=0),
        grid=(batch, nt),
        in_specs=[pl.BlockSpec((tt, D_PACK), lambda b, i: (b * nt + i, 0))] + _param_specs(params, layer),
        out_specs=[
            pl.BlockSpec((tt, D_MODEL), lambda b, i: (b * nt + i, 0)),
            pl.BlockSpec((1, GDN_HEADS, GDN_HEAD_DIM, GDN_HEAD_DIM), lambda b, i: (b, 0, 0, 0)),
            pl.BlockSpec((1, SSD_HEADS, SSD_HEAD_DIM, SSD_STATE), lambda b, i: (b, 0, 0, 0)),
        ],
        out_shape=[
            jax.ShapeDtypeStruct((batch * seq, D_MODEL), BF16),
            jax.ShapeDtypeStruct((batch, GDN_HEADS, GDN_HEAD_DIM, GDN_HEAD_DIM), F32),
            jax.ShapeDtypeStruct((batch, SSD_HEADS, SSD_HEAD_DIM, SSD_STATE), F32),
        ],
        scratch_shapes=[
            pltpu.VMEM((SUBLANE + tt, 3 * D_GDN), F32),
            pltpu.VMEM((SUBLANE + tt, SSD_CONV_CH), F32),
            pltpu.VMEM((2 * SUBLANE + tt, D_POOL), F32),
            pltpu.VMEM((tt, D_SSD), F32),
            pltpu.VMEM((tt, D_SSD), F32),
        ],
        compiler_params=pltpu.CompilerParams(
            dimension_semantics=("parallel", "arbitrary"), vmem_limit_bytes=VMEM_LIMIT),
        name="mixer_prompt",
    )(proj, *params)


def _mixer_sample_kernel(proj_ref, sg_in, gcv_in, ss_in, scv_in, pool_in,
                         gp_ref, gcw_ref, gnorm_ref, scw_ref, scb_ref, sd_ref, snorm_ref, pw_ref, pscale_ref,
                         y_ref, sg_out, ss_out,
                         e_ref, ep_ref, ys_ref, xs_ref, ye_ref, *, pos0):
    nb, st, t_len = SAMPLE_NB, SEQ_TILE, SAMPLE_T
    rows = nb * st
    tok0 = st - t_len
    base = SUBLANE

    e_ref[...] = jnp.zeros(e_ref.shape, F32)
    ep_ref[...] = jnp.zeros(ep_ref.shape, F32)
    pool_rows = ep_ref.shape[0] // nb
    pool_tok0 = pool_rows - t_len
    for b in range(nb):
        r = base + b * st
        e_ref[r + tok0:r + st, :] = proj_ref[b * t_len:(b + 1) * t_len, :]
        e_ref[r + tok0 - (CONV_K - 1):r + tok0, C_QKV:C_QKV + 3 * D_GDN] = gcv_in[b]
        e_ref[r + tok0 - (CONV_K - 1):r + tok0, C_XBC:C_XBC + SSD_CONV_CH] = scv_in[b]
        pr = b * pool_rows
        ep_ref[pr + pool_tok0 - POOL_BUF:pr + pool_tok0, :] = pool_in[b]
        ep_ref[pr + pool_tok0:pr + pool_rows, :] = proj_ref[b * t_len:(b + 1) * t_len, C_UP:C_UP + D_POOL]

    def erows(c0, width):
        return e_ref[base:base + rows, c0:c0 + width]

    beta, sp, gd = _gates(erows(C_SM, LANE), gp_ref)
    row = lax.broadcasted_iota(jnp.int32, (rows, LANE), 0)
    cum = _seg_cumsum(gd, t_len, (row & (st - 1)) - tok0)
    cum_t = cum.T

    ri = lax.broadcasted_iota(jnp.int32, (rows, rows), 0)
    rj = lax.broadcasted_iota(jnp.int32, (rows, rows), 1)
    same = ((ri >> 3) == (rj >> 3)) & ((ri & (st - 1)) >= tok0) & ((rj & (st - 1)) >= tok0)
    causal, strict = same & (ri >= rj), same & (ri > rj)
    valid_c = (lax.broadcasted_iota(jnp.int32, (rows, 1), 0) & (st - 1)) >= tok0

    def store_y(c0, val):
        ye_ref[:, c0:c0 + val.shape[1]] = val

    for h in range(GDN_HEADS):
        conv = functools.partial(_conv_silu, e_ref, base, rows)
        c0 = h * GDN_HEAD_DIM
        qh = _l2norm(conv(C_QKV + c0, GDN_HEAD_DIM, gcw_ref, c0)) * (GDN_HEAD_DIM ** -0.5)
        kh = _l2norm(conv(C_QKV + D_GDN + c0, GDN_HEAD_DIM, gcw_ref, D_GDN + c0))
        vh = conv(C_QKV + 2 * D_GDN + c0, GDN_HEAD_DIM, gcw_ref, 2 * D_GDN + c0)

        def get_sg(b, h=h):
            return sg_in[b, h]

        def set_sg(b, val, h=h):
            sg_out[b, h] = val

        o = _gdn_head(
            qh, kh, vh,
            beta[:, L_BETA + h:L_BETA + h + 1],
            cum[:, L_GDN_G + h:L_GDN_G + h + 1],
            cum_t[L_GDN_G + h:L_GDN_G + h + 1, :],
            causal, strict, valid_c, 1,
            [(b * st, st, b * st + st - 1, b) for b in range(nb)], get_sg, set_sg)
        store_y(c0, _gdn_finish(o, erows(C_ZG + c0, GDN_HEAD_DIM), gnorm_ref))

    conv_x = functools.partial(_conv_silu, e_ref, base, rows)
    for g in range(SSD_GROUPS):
        bg = conv_x(C_XBC + D_SSD + g * SSD_STATE, SSD_STATE, scw_ref, D_SSD + g * SSD_STATE, scb_ref)
        cg = conv_x(C_XBC + D_SSD + (SSD_GROUPS + g) * SSD_STATE, SSD_STATE, scw_ref,
                    D_SSD + (SSD_GROUPS + g) * SSD_STATE, scb_ref)
        cb = _dot_nt(_bf(cg), _bf(bg))
        for pair in range(SSD_HEADS_PER_GROUP // 2):
            c0 = (g * SSD_HEADS_PER_GROUP + 2 * pair) * SSD_HEAD_DIM
            xs2 = conv_x(C_XBC + c0, 2 * SSD_HEAD_DIM, scw_ref, c0, scb_ref)
            xs_ref[:, c0:c0 + 2 * SSD_HEAD_DIM] = xs2
            for half in range(2):
                hd = g * SSD_HEADS_PER_GROUP + 2 * pair + half
                lane = L_SSD_DT + hd

                def get_ss(b, hd=hd):
                    return ss_in[b, hd]

                def set_ss(b, val, hd=hd):
                    ss_out[b, hd] = val

                y = _ssd_head(
                    xs2[:, half * SSD_HEAD_DIM:(half + 1) * SSD_HEAD_DIM],
                    sp[:, lane:lane + 1], cum[:, lane:lane + 1], cum_t[lane:lane + 1, :],
                    cb, cg, bg, causal, valid_c,
                    [(b * st, st, b * st + st - 1, b) for b in range(nb)], get_ss, set_ss)
                ys_ref[:, hd * SSD_HEAD_DIM:(hd + 1) * SSD_HEAD_DIM] = y

    _ssd_finish(ys_ref, xs_ref, lambda c0, w: erows(C_ZS + c0, w), sd_ref, snorm_ref, store_y)

    tpos = lax.broadcasted_iota(jnp.int32, (st, 1), 0) - tok0
    for gi, win in enumerate(POOL_WINDOWS):
        sl = slice(gi * POOL_GROUP_DIM, (gi + 1) * POOL_GROUP_DIM)
        cnt = jnp.minimum(win, pos0 + tpos + 1).astype(F32)
        tiles = []
        for b in range(nb):
            r = (b + 1) * pool_rows - st
            acc = ep_ref[r:r + st, sl]
            u = acc
            for k in range(1, win):
                acc = acc + ep_ref[r - k:r - k + st, sl]
            tiles.append(acc / cnt - u)
        pooled = jnp.concatenate(tiles, axis=0)
        mixed = _dot(_bf(pooled), pw_ref[gi])
        z = erows(C_ZP + gi * POOL_GROUP_DIM, POOL_GROUP_DIM)
        store_y(D_GDN + D_SSD + gi * POOL_GROUP_DIM, mixed * pscale_ref[0:1, sl] * _silu(z))

    for b in range(nb):
        y_ref[b * t_len:(b + 1) * t_len, :] = ye_ref[b * st + tok0:(b + 1) * st, :]


def _mixer_sample(proj, states, params, layer, n_seq):
    nb, st, t_len = SAMPLE_NB, SEQ_TILE, SAMPLE_T
    rows = nb * st
    state_gdn, state_gdn_conv, state_ssd, state_ssd_conv, state_pool = states
    pool_rows = 3 * SUBLANE
    assert pool_rows - t_len - POOL_BUF >= 1 and (pool_rows - t_len) % SUBLANE == st - t_len

    def seq_spec(tail):
        nd = len(tail)
        return pl.BlockSpec((None, nb) + tuple(tail), lambda i: (layer, i) + (0,) * nd)

    return pl.pallas_call(
        functools.partial(_mixer_sample_kernel, pos0=PAST_LEN),
        grid=(n_seq // nb,),
        in_specs=[
            pl.BlockSpec((nb * t_len, D_PACK), lambda i: (i, 0)),
            seq_spec(state_gdn.shape[2:]),
            seq_spec(state_gdn_conv.shape[2:]),
            seq_spec(state_ssd.shape[2:]),
            seq_spec(state_ssd_conv.shape[2:]),
            seq_spec(state_pool.shape[2:]),
        ] + _param_specs(params, layer),
        out_specs=[
            pl.BlockSpec((nb * t_len, D_MODEL), lambda i: (i, 0)),
            pl.BlockSpec((nb,) + state_gdn.shape[2:], lambda i: (i, 0, 0, 0)),
            pl.BlockSpec((nb,) + state_ssd.shape[2:], lambda i: (i, 0, 0, 0)),
        ],
        out_shape=[
            jax.ShapeDtypeStruct((n_seq * t_len, D_MODEL), F32),
            jax.ShapeDtypeStruct((n_seq,) + state_gdn.shape[2:], F32),
            jax.ShapeDtypeStruct((n_seq,) + state_ssd.shape[2:], F32),
        ],
        scratch_shapes=[
            pltpu.VMEM((SUBLANE + rows, D_PACK), F32),
            pltpu.VMEM((nb * pool_rows, D_POOL), F32),
            pltpu.VMEM((rows, D_SSD), F32),
            pltpu.VMEM((rows, D_SSD), F32),
            pltpu.VMEM((rows, D_MODEL), F32),
        ],
        compiler_params=pltpu.CompilerParams(
            dimension_semantics=("parallel",), vmem_limit_bytes=VMEM_LIMIT),
        name="mixer_sample",
    )(proj, state_gdn, state_gdn_conv, state_ssd, state_ssd_conv, state_pool, *params)


def _pack_w_in(w_in):
    offs = np.cumsum((0,) + IN_SIZES)

    def seg(i):
        return w_in[..., offs[i]:offs[i + 1]]

    pad = jnp.zeros(w_in.shape[:-1] + (LANE - 2 * GDN_HEADS - SSD_HEADS,), w_in.dtype)
    packed = jnp.concatenate([seg(0), seg(1), seg(4), seg(5), seg(7), seg(8), seg(2), seg(3), seg(6), pad], axis=-1)
    return packed.astype(BF16)


def _gate_params(gdn_a_log, gdn_dt_bias, ssd_a_log, ssd_dt_bias):
    def lanes(gdn, ssd):
        z = jnp.zeros((DEPTH, LANE), F32)
        z = z.at[:, L_GDN_G:L_GDN_G + GDN_HEADS].set(gdn)
        return z.at[:, L_SSD_DT:L_SSD_DT + SSD_HEADS].set(ssd)

    rows = jnp.stack([lanes(gdn_a_log, ssd_a_log), lanes(gdn_dt_bias, ssd_dt_bias)], axis=1)
    return jnp.concatenate([rows, jnp.zeros((DEPTH, SUBLANE - 2, LANE), F32)], axis=1)


def kernel(x_prompt, x_sample, state_gdn, state_gdn_conv, state_ssd, state_ssd_conv, state_pool, norm_w, w_in, gdn_conv_w, gdn_a_log, gdn_dt_bias, gdn_norm_w, ssd_conv_w, ssd_conv_b, ssd_a_log, ssd_dt_bias, ssd_d, ssd_norm_w, pool_w, pool_scale, w_out, final_norm_w):
    batch, seq, _ = x_prompt.shape
    n_seq, t_len, _ = x_sample.shape
    assert t_len == SAMPLE_T and seq % PROMPT_TT == 0 and n_seq % SAMPLE_NB == 0

    w_in_p = _pack_w_in(w_in)
    w_out_b = w_out.astype(BF16)
    params = (
        _gate_params(gdn_a_log, gdn_dt_bias, ssd_a_log, ssd_dt_bias),
        gdn_conv_w,
        gdn_norm_w[:, None, :],
        ssd_conv_w,
        ssd_conv_b[:, None, :],
        jnp.repeat(ssd_d, SSD_HEAD_DIM, axis=-1)[:, None, :],
        ssd_norm_w[:, None, :],
        pool_w.astype(BF16),
        pool_scale[:, None, :],
    )
    nw = norm_w[:, None, :]
    fw = final_norm_w[None, :]
    states = (state_gdn, state_gdn_conv, state_ssd, state_ssd_conv, state_pool)

    xp = x_prompt.reshape(batch * seq, D_MODEL)
    xs = x_sample.reshape(n_seq * t_len, D_MODEL)
    new_p = [[] for _ in range(5)]
    new_s = [[] for _ in range(5)]
    for layer in range(DEPTH):
        final = fw if layer == DEPTH - 1 else None

        proj_p = _inproj(xp, nw, w_in_p, layer, 512)
        y_p, gdn_p, ssd_p = _mixer_prompt(proj_p, params, layer, batch, seq)
        xp = _outproj(y_p, w_out_b, xp, layer, 512, final)
        pp = proj_p.reshape(batch, seq, D_PACK)
        new_p[0].append(gdn_p)
        new_p[1].append(pp[:, seq - (CONV_K - 1):, C_QKV:C_QKV + 3 * D_GDN])
        new_p[2].append(ssd_p)
        new_p[3].append(pp[:, seq - (CONV_K - 1):, C_XBC:C_XBC + SSD_CONV_CH])
        new_p[4].append(pp[:, seq - POOL_BUF:, C_UP:C_UP + D_POOL])

        proj_s = _inproj(xs, nw, w_in_p, layer, n_seq * t_len)
        y_s, gdn_s, ssd_s = _mixer_sample(proj_s, states, params, layer, n_seq)
        xs = _outproj(y_s, w_out_b, xs, layer, n_seq * t_len, final)
        ps = proj_s.reshape(n_seq, t_len, D_PACK)
        new_s[0].append(gdn_s)
        new_s[1].append(ps[:, t_len - (CONV_K - 1):, C_QKV:C_QKV + 3 * D_GDN])
        new_s[2].append(ssd_s)
        new_s[3].append(ps[:, t_len - (CONV_K - 1):, C_XBC:C_XBC + SSD_CONV_CH])
        new_s[4].append(jnp.concatenate([state_pool[layer][:, t_len:], ps[:, :, C_UP:C_UP + D_POOL]], axis=1))

    y_prompt = xp.reshape(batch, seq, D_MODEL)
    y_sample = xs.reshape(n_seq, t_len, D_MODEL)
    return (y_prompt, y_sample) + tuple(jnp.stack(a) for a in new_p) + tuple(jnp.stack(a) for a in new_s)
```

```python
import functools

import jax
import jax.numpy as jnp
import numpy as np
from jax import lax
from jax.experimental import pallas as pl
from jax.experimental.pallas import tpu as pltpu

F32 = jnp.float32
BF16 = jnp.bfloat16

D_MODEL = 2048
DEPTH = 4
PAST_LEN = 16384
D_GDN = 768
D_SSD = 768
D_POOL = 512
CONV_K = 4
GDN_HEAD_DIM = 128
GDN_HEADS = 6
GDN_CHUNK = 64
SSD_HEAD_DIM = 64
SSD_HEADS = 12
SSD_STATE = 128
SSD_GROUPS = 2
SSD_HEADS_PER_GROUP = 6
SSD_GROUP_DIM = D_SSD // SSD_GROUPS
SSD_CHUNK = 128
SSD_CONV_CH = 1280
POOL_WINDOWS = (2, 4, 8, 16)
POOL_GROUP_DIM = 128
POOL_BUF = 15
IN_SIZES = (3 * D_GDN, D_GDN, GDN_HEADS, GDN_HEADS, D_SSD, SSD_CONV_CH, SSD_HEADS, D_POOL, D_POOL)
EPS = 1e-6

C_QKV = 0
C_ZG = 2304
C_ZS = 3072
C_XBC = 3840
C_UP = 5120
C_ZP = 5632
C_SM = 6144
D_PACK = 6272
L_BETA = 0
L_GDN_G = 6
L_SSD_DT = 12

LANE = 128
SUBLANE = 8
VMEM_LIMIT = 48 * 1024 * 1024

PROJ_TN = 896
PROMPT_TT = 128
SAMPLE_NB = 8
SAMPLE_T = 4
SEQ_TILE = 8


def _dot(a, b):
    return jnp.dot(a, b, preferred_element_type=F32)


def _dot_nt(a, b):
    return lax.dot_general(a, b, (((1,), (1,)), ((), ())), preferred_element_type=F32)


def _dot_tn(a, b):
    return lax.dot_general(a, b, (((0,), (0,)), ((), ())), preferred_element_type=F32)


def _sigmoid(x):
    return 0.5 + 0.5 * jnp.tanh(0.5 * x)


def _silu(x):
    hx = 0.5 * x
    return hx + hx * jnp.tanh(hx)


def _softplus(x):
    return jnp.maximum(x, 0.0) + jnp.log1p(jnp.exp(-jnp.abs(x)))


def _bf(x):
    return x.astype(BF16)


def _inproj_kernel(x_ref, nw_ref, w_ref, o_ref, h_ref):
    @pl.when(pl.program_id(1) == 0)
    def _():
        x = x_ref[...]
        ms = jnp.mean(x * x, axis=-1, keepdims=True)
        h_ref[...] = _bf(x * lax.rsqrt(ms + EPS) * nw_ref[...])

    o_ref[...] = _dot(h_ref[...], w_ref[...])


def _inproj(x, nw, w_packed, layer, tm):
    n = x.shape[0]
    return pl.pallas_call(
        _inproj_kernel,
        grid=(n // tm, D_PACK // PROJ_TN),
        in_specs=[
            pl.BlockSpec((tm, D_MODEL), lambda i, j: (i, 0)),
            pl.BlockSpec((None, 1, D_MODEL), lambda i, j: (layer, 0, 0)),
            pl.BlockSpec((None, D_MODEL, PROJ_TN), lambda i, j: (layer, 0, j)),
        ],
        out_specs=pl.BlockSpec((tm, PROJ_TN), lambda i, j: (i, j)),
        out_shape=jax.ShapeDtypeStruct((n, D_PACK), F32),
        scratch_shapes=[pltpu.VMEM((tm, D_MODEL), BF16)],
        compiler_params=pltpu.CompilerParams(
            dimension_semantics=("parallel", "arbitrary"), vmem_limit_bytes=VMEM_LIMIT),
        name="inproj",
    )(x, nw, w_packed)


def _outproj_kernel(y_ref, w_ref, x_ref, o_ref):
    o_ref[...] = x_ref[...] + _dot(_bf(y_ref[...]), w_ref[...])


def _outproj_final_kernel(y_ref, w_ref, x_ref, fw_ref, o_ref):
    x = x_ref[...] + _dot(_bf(y_ref[...]), w_ref[...])
    ms = jnp.mean(x * x, axis=-1, keepdims=True)
    o_ref[...] = x * lax.rsqrt(ms + EPS) * fw_ref[...]


def _outproj(y, w_out, x, layer, tm, final_w=None):
    n = x.shape[0]
    in_specs = [
        pl.BlockSpec((tm, D_MODEL), lambda i: (i, 0)),
        pl.BlockSpec((None, D_MODEL, D_MODEL), lambda i: (layer, 0, 0)),
        pl.BlockSpec((tm, D_MODEL), lambda i: (i, 0)),
    ]
    args = [y, w_out, x]
    body = _outproj_kernel
    if final_w is not None:
        in_specs.append(pl.BlockSpec((1, D_MODEL), lambda i: (0, 0)))
        args.append(final_w)
        body = _outproj_final_kernel
    return pl.pallas_call(
        body,
        grid=(n // tm,),
        in_specs=in_specs,
        out_specs=pl.BlockSpec((tm, D_MODEL), lambda i: (i, 0)),
        out_shape=jax.ShapeDtypeStruct((n, D_MODEL), F32),
        compiler_params=pltpu.CompilerParams(
            dimension_semantics=("parallel",), vmem_limit_bytes=VMEM_LIMIT),
        name="outproj",
    )(*args)


def _conv_silu(ext_ref, r0, rows, c0, width, w_ref, wc0, b_ref=None):
    acc = None
    for j in range(CONV_K):
        lo = r0 - (CONV_K - 1) + j
        term = ext_ref[lo:lo + rows, c0:c0 + width] * w_ref[j:j + 1, wc0:wc0 + width]
        acc = term if acc is None else acc + term
    if b_ref is not None:
        acc = acc + b_ref[0:1, wc0:wc0 + width]
    return _silu(acc)


def _conv_silu_tile(ext_ref, hist, c0, width, w_ref, wc0, b_ref=None):
    e = ext_ref[:, c0:c0 + width]
    acc = e * w_ref[0:1, wc0:wc0 + width]
    for j in range(1, CONV_K):
        acc = pltpu.roll(acc, 1, 0) + e * w_ref[j:j + 1, wc0:wc0 + width]
    acc = acc[hist:]
    if b_ref is not None:
        acc = acc + b_ref[0:1, wc0:wc0 + width]
    return _silu(acc)


def _l2norm(x):
    return x * lax.rsqrt(jnp.sum(x * x, axis=-1, keepdims=True) + EPS)


def _seg_cumsum(x, seg_len, row_in_seg):
    s = 1
    while s < seg_len:
        x = x + jnp.where(row_in_seg >= s, pltpu.roll(x, s, 0), 0.0)
        s *= 2
    return x


def _gates(small, gp_ref):
    sp = _softplus(small + gp_ref[1:2, :])
    gd = -jnp.exp(gp_ref[0:1, :]) * sp
    return _sigmoid(small), sp, gd


def _masked_decay(col, row, causal):
    return jnp.where(causal, jnp.exp(jnp.where(causal, col - row, 0.0)), 0.0)


def _gdn_head(qh, kh, vh, beta_c, gc_c, gc_r, causal, strict, valid_c, n_levels, seqs, get_state, set_state):
    eg = jnp.exp(gc_c)
    decay = _masked_decay(gc_c, gc_r, causal)
    kb, qb = _bf(kh), _bf(qh)
    kk = _dot_nt(kb, kb)
    nmat = jnp.where(strict, -(kk * decay * beta_c), 0.0)
    p = nmat
    for _ in range(n_levels):
        pb = _bf(p)
        p = _dot(pb, pb)
        nmat = nmat + p + _dot(_bf(nmat), _bf(p))
    rhs = jnp.concatenate([vh * beta_c, kh * (beta_c * eg)], axis=-1)
    sol = rhs + _dot(_bf(nmat), _bf(rhs))
    u, w = sol[:, :GDN_HEAD_DIM], sol[:, GDN_HEAD_DIM:]
    qk = jnp.where(causal, _dot_nt(qb, kb) * decay, 0.0)
    qdec = qh * eg
    o_parts, v_parts = [], []
    for (s0, sl, last, idx) in seqs:
        state = get_state(idx)
        sb = _bf(state)
        glast = gc_c[last:last + 1, :]
        vn = u[s0:s0 + sl] - _dot(_bf(w[s0:s0 + sl]), sb)
        kd = kh[s0:s0 + sl] * jnp.exp(glast - gc_c[s0:s0 + sl])
        if valid_c is not None:
            vn = jnp.where(valid_c[s0:s0 + sl], vn, 0.0)
            kd = jnp.where(valid_c[s0:s0 + sl], kd, 0.0)
        o_parts.append(_dot(_bf(qdec[s0:s0 + sl]), sb))
        v_parts.append(vn)
        set_state(idx, state * jnp.exp(glast) + _dot_tn(_bf(kd), _bf(vn)))
    o_state = o_parts[0] if len(o_parts) == 1 else jnp.concatenate(o_parts, axis=0)
    v_new = v_parts[0] if len(v_parts) == 1 else jnp.concatenate(v_parts, axis=0)
    return o_state + _dot(_bf(qk), _bf(v_new))


def _ssd_head(xs, dt_c, a_c, a_r, cb, cg, bg, causal, valid_c, seqs, get_state, set_state):
    seg = _masked_decay(a_c, a_r, causal)
    xdt = xs * dt_c
    y = _dot(_bf(cb * seg), _bf(xdt))
    cgb, bgb = _bf(cg), _bf(bg)
    ea = jnp.exp(a_c)
    parts = []
    for (s0, sl, last, idx) in seqs:
        state = get_state(idx)
        alast = a_c[last:last + 1, :]
        xin = xdt[s0:s0 + sl] * jnp.exp(alast - a_c[s0:s0 + sl])
        if valid_c is not None:
            xin = jnp.where(valid_c[s0:s0 + sl], xin, 0.0)
        parts.append(ea[s0:s0 + sl] * _dot_nt(cgb[s0:s0 + sl], _bf(state)))
        set_state(idx, state * jnp.exp(alast) + _dot_tn(_bf(xin), bgb[s0:s0 + sl]))
    y_off = parts[0] if len(parts) == 1 else jnp.concatenate(parts, axis=0)
    return y + y_off


def _gdn_finish(o, z, gnorm_ref):
    on = o * lax.rsqrt(jnp.mean(o * o, axis=-1, keepdims=True) + EPS) * gnorm_ref[0:1, :]
    return on * _silu(z)


def _ssd_finish(ys_ref, xs_ref, z_of, sd_ref, snorm_ref, store):
    for g in range(SSD_GROUPS):
        c0 = g * SSD_GROUP_DIM
        sl = slice(c0, c0 + SSD_GROUP_DIM)
        yz = (ys_ref[:, sl] + sd_ref[0:1, sl] * xs_ref[:, sl]) * _silu(z_of(c0, SSD_GROUP_DIM))
        yn = yz * lax.rsqrt(jnp.mean(yz * yz, axis=-1, keepdims=True) + EPS) * snorm_ref[0:1, sl]
        store(D_GDN + c0, yn)


def _mixer_prompt_kernel(proj_ref, gp_ref, gcw_ref, gnorm_ref, scw_ref, scb_ref, sd_ref, snorm_ref,
                         pw_ref, pscale_ref,
                         y_ref, sg_ref, ss_ref,
                         eq_ref, ex_ref, ep_ref, *, pos0):
    tt = PROMPT_TT
    n_chunks = tt // GDN_CHUNK
    hist, phist = SUBLANE, 2 * SUBLANE
    i = pl.program_id(1)

    @pl.when(i == 0)
    def _():
        sg_ref[...] = jnp.zeros(sg_ref.shape, F32)
        ss_ref[...] = jnp.zeros(ss_ref.shape, F32)
        eq_ref[0:hist, :] = jnp.zeros((hist, eq_ref.shape[1]), F32)
        ex_ref[0:hist, :] = jnp.zeros((hist, ex_ref.shape[1]), F32)
        ep_ref[0:phist, :] = jnp.zeros((phist, D_POOL), F32)

    eq_ref[hist:hist + tt, :] = proj_ref[:, C_QKV:C_QKV + 3 * D_GDN]
    ex_ref[hist:hist + tt, :] = proj_ref[:, C_XBC:C_XBC + SSD_CONV_CH]
    ep_ref[phist:phist + tt, :] = proj_ref[:, C_UP:C_UP + D_POOL]

    beta, sp, gd = _gates(proj_ref[:, C_SM:C_SM + LANE], gp_ref)
    row = lax.broadcasted_iota(jnp.int32, (tt, LANE), 0)
    gc = _seg_cumsum(gd, GDN_CHUNK, row & (GDN_CHUNK - 1))
    ac = _seg_cumsum(gd, SSD_CHUNK, row)

    def store_y(c0, val, r0=0):
        y_ref[r0:r0 + val.shape[0], c0:c0 + val.shape[1]] = val.astype(y_ref.dtype)

    ci = lax.broadcasted_iota(jnp.int32, (GDN_CHUNK, GDN_CHUNK), 0)
    cj = lax.broadcasted_iota(jnp.int32, (GDN_CHUNK, GDN_CHUNK), 1)
    causal_g, strict_g = ci >= cj, ci > cj
    gc_t = [gc[c * GDN_CHUNK:(c + 1) * GDN_CHUNK].T for c in range(n_chunks)]
    conv_q = functools.partial(_conv_silu_tile, eq_ref, hist)

    keys = [(c, h) for c in range(n_chunks) for h in range(GDN_HEADS)]
    nmat, pmat, rhs, qkb, qdec, kdec_t, glast = {}, {}, {}, {}, {}, {}, {}
    for h in range(GDN_HEADS):
        c0 = h * GDN_HEAD_DIM
        q_all = _l2norm(conv_q(c0, GDN_HEAD_DIM, gcw_ref, c0)) * (GDN_HEAD_DIM ** -0.5)
        k_all = _l2norm(conv_q(D_GDN + c0, GDN_HEAD_DIM, gcw_ref, D_GDN + c0))
        v_all = conv_q(2 * D_GDN + c0, GDN_HEAD_DIM, gcw_ref, 2 * D_GDN + c0)
        for c in range(n_chunks):
            rs = slice(c * GDN_CHUNK, (c + 1) * GDN_CHUNK)
            qh, kh, vh = q_all[rs], k_all[rs], v_all[rs]
            beta_c = beta[rs, L_BETA + h:L_BETA + h + 1]
            gc_c = gc[rs, L_GDN_G + h:L_GDN_G + h + 1]
            gc_r = gc_t[c][L_GDN_G + h:L_GDN_G + h + 1, :]
            gl = gc_c[GDN_CHUNK - 1:GDN_CHUNK, :]
            eg = jnp.exp(gc_c)
            decay = _masked_decay(gc_c, gc_r, causal_g)
            kb, qb = _bf(kh), _bf(qh)
            nmat[c, h] = jnp.where(strict_g, -(_dot_nt(kb, kb) * decay * beta_c), 0.0)
            qkb[c, h] = _bf(jnp.where(causal_g, _dot_nt(qb, kb) * decay, 0.0))
            rhs[c, h] = jnp.concatenate([vh * beta_c, kh * (beta_c * eg)], axis=-1)
            qdec[c, h] = qh * eg
            kdec_t[c, h] = _bf((kh * jnp.exp(gl - gc_c)).T)
            glast[c, h] = jnp.exp(gl)

    for key in keys:
        bb = _bf(nmat[key])
        pmat[key] = _dot(bb, bb)
    for _ in range(4):
        for key in keys:
            pb = _bf(pmat[key])
            r = _dot(jnp.concatenate([_bf(nmat[key]), pb], axis=0), pb)
            nmat[key] = nmat[key] + pmat[key] + r[:GDN_CHUNK]
            pmat[key] = r[GDN_CHUNK:]
    sol = {}
    for key in keys:
        nmat[key] = nmat[key] + pmat[key] + _dot(_bf(nmat[key]), _bf(pmat[key]))
    for key in keys:
        sol[key] = rhs[key] + _dot(_bf(nmat[key]), _bf(rhs[key]))

    for c in range(n_chunks):
        r0 = c * GDN_CHUNK
        for h in range(GDN_HEADS):
            key = (c, h)
            state = sg_ref[0, h]
            u, w = sol[key][:, :GDN_HEAD_DIM], sol[key][:, GDN_HEAD_DIM:]
            r1 = _dot(_bf(jnp.concatenate([w, qdec[key]], axis=0)), _bf(state))
            vnb = _bf(u - r1[:GDN_CHUNK])
            r2 = _dot(jnp.concatenate([qkb[key], kdec_t[key]], axis=0), vnb)
            sg_ref[0, h] = state * glast[key] + r2[GDN_CHUNK:]
            o = r1[GDN_CHUNK:] + r2[:GDN_CHUNK]
            c0 = h * GDN_HEAD_DIM
            z = proj_ref[r0:r0 + GDN_CHUNK, C_ZG + c0:C_ZG + c0 + GDN_HEAD_DIM]
            store_y(c0, _gdn_finish(o, z, gnorm_ref), r0)

    si = lax.broadcasted_iota(jnp.int32, (SSD_CHUNK, SSD_CHUNK), 0)
    sj = lax.broadcasted_iota(jnp.int32, (SSD_CHUNK, SSD_CHUNK), 1)
    causal_s = si >= sj
    lo_lanes = sj < SSD_HEAD_DIM
    lo_rows = si < SSD_HEAD_DIM
    ac_t = ac.T
    conv_x = functools.partial(_conv_silu_tile, ex_ref, hist)
    for g in range(SSD_GROUPS):
        bgb = _bf(conv_x(D_SSD + g * SSD_STATE, SSD_STATE, scw_ref, D_SSD + g * SSD_STATE, scb_ref))
        cgb = _bf(conv_x(D_SSD + (SSD_GROUPS + g) * SSD_STATE, SSD_STATE, scw_ref,
                         D_SSD + (SSD_GROUPS + g) * SSD_STATE, scb_ref))
        cb = _dot_nt(cgb, bgb)
        yz, ssq = [], None
        for pair in range(SSD_HEADS_PER_GROUP // 2):
            hd0 = g * SSD_HEADS_PER_GROUP + 2 * pair
            c0 = hd0 * SSD_HEAD_DIM
            la, lb = L_SSD_DT + hd0, L_SSD_DT + hd0 + 1
            xs2 = conv_x(c0, 2 * SSD_HEAD_DIM, scw_ref, c0, scb_ref)
            a2 = jnp.where(lo_lanes, ac[:, la:la + 1], ac[:, lb:lb + 1])
            xdt2 = xs2 * jnp.where(lo_lanes, sp[:, la:la + 1], sp[:, lb:lb + 1])
            alast2 = a2[tt - 1:tt, :]
            state2 = ss_ref[0, hd0:hd0 + 2].reshape(2 * SSD_HEAD_DIM, SSD_STATE)
            y = jnp.exp(a2) * _dot_nt(cgb, _bf(state2))
            seg_a = _masked_decay(ac[:, la:la + 1], ac_t[la:la + 1, :], causal_s)
            seg_b = _masked_decay(ac[:, lb:lb + 1], ac_t[lb:lb + 1, :], causal_s)
            y = y + _dot(_bf(cb * seg_a), _bf(jnp.where(lo_lanes, xdt2, 0.0)))
            y = y + _dot(_bf(cb * seg_b), _bf(jnp.where(lo_lanes, 0.0, xdt2)))
            upd = _dot_tn(_bf(xdt2 * jnp.exp(alast2 - a2)), bgb)
            elast = jnp.exp(jnp.where(lo_rows, ac[tt - 1:tt, la:la + 1], ac[tt - 1:tt, lb:lb + 1]))
            ss_ref[0, hd0:hd0 + 2] = (state2 * elast + upd).reshape(2, SSD_HEAD_DIM, SSD_STATE)
            z = proj_ref[:, C_ZS + c0:C_ZS + c0 + 2 * SSD_HEAD_DIM]
            t = (y + sd_ref[0:1, c0:c0 + 2 * SSD_HEAD_DIM] * xs2) * _silu(z)
            yz.append(t)
            s = jnp.sum(t * t, axis=-1, keepdims=True)
            ssq = s if ssq is None else ssq + s
        scale = lax.rsqrt(ssq * (1.0 / SSD_GROUP_DIM) + EPS)
        for pair, t in enumerate(yz):
            c0 = (g * SSD_HEADS_PER_GROUP + 2 * pair) * SSD_HEAD_DIM
            store_y(D_GDN + c0, t * scale * snorm_ref[0:1, c0:c0 + 2 * SSD_HEAD_DIM])

    pos = pos0 + i * tt + lax.broadcasted_iota(jnp.int32, (tt, 1), 0)
    for gi, win in enumerate(POOL_WINDOWS):
        sl = slice(gi * POOL_GROUP_DIM, (gi + 1) * POOL_GROUP_DIM)
        e = ep_ref[:, sl]
        acc, k = e, 1
        while k < win:
            acc = acc + pltpu.roll(acc, k, 0)
            k *= 2
        cnt = jnp.minimum(win, pos + 1).astype(F32)
        pooled = acc[phist:] / cnt - e[phist:]
        mixed = _dot(_bf(pooled), pw_ref[gi])
        z = proj_ref[:, C_ZP + gi * POOL_GROUP_DIM:C_ZP + (gi + 1) * POOL_GROUP_DIM]
        store_y(D_GDN + D_SSD + gi * POOL_GROUP_DIM, mixed * pscale_ref[0:1, sl] * _silu(z))

    eq_ref[0:hist, :] = eq_ref[tt:tt + hist, :]
    ex_ref[0:hist, :] = ex_ref[tt:tt + hist, :]
    ep_ref[0:phist, :] = ep_ref[tt:tt + phist, :]


def _layer_spec(shape):
    nd = len(shape)
    return lambda layer: pl.BlockSpec((None,) + tuple(shape[1:]), lambda *_: (layer,) + (0,) * (nd - 1))


def _param_specs(params, layer):
    return [_layer_spec(p.shape)(layer) for p in params]


def _mixer_prompt(proj, params, layer, batch, seq):
    tt = PROMPT_TT
    nt = seq // tt
    return pl.pallas_call(
        functools.partial(_mixer_prompt_kernel, pos0=0),
        grid=(batch, nt),
        in_specs=[pl.BlockSpec((tt, D_PACK), lambda b, i: (b * nt + i, 0))] + _param_specs(params, layer),
        out_specs=[
            pl.BlockSpec((tt, D_MODEL), lambda b, i: (b * nt + i, 0)),
            pl.BlockSpec((1, GDN_HEADS, GDN_HEAD_DIM, GDN_HEAD_DIM), lambda b, i: (b, 0, 0, 0)),
            pl.BlockSpec((1, SSD_HEADS, SSD_HEAD_DIM, SSD_STATE), lambda b, i: (b, 0, 0, 0)),
        ],
        out_shape=[
            jax.ShapeDtypeStruct((batch * seq, D_MODEL), BF16),
            jax.ShapeDtypeStruct((batch, GDN_HEADS, GDN_HEAD_DIM, GDN_HEAD_DIM), F32),
            jax.ShapeDtypeStruct((batch, SSD_HEADS, SSD_HEAD_DIM, SSD_STATE), F32),
        ],
        scratch_shapes=[
            pltpu.VMEM((SUBLANE + tt, 3 * D_GDN), F32),
            pltpu.VMEM((SUBLANE + tt, SSD_CONV_CH), F32),
            pltpu.VMEM((2 * SUBLANE + tt, D_POOL), F32),
        ],
        compiler_params=pltpu.CompilerParams(
            dimension_semantics=("parallel", "arbitrary"), vmem_limit_bytes=VMEM_LIMIT),
        name="mixer_prompt",
    )(proj, *params)


def _mixer_sample_kernel(proj_ref, sg_in, gcv_in, ss_in, scv_in, pool_in,
                         gp_ref, gcw_ref, gnorm_ref, scw_ref, scb_ref, sd_ref, snorm_ref, pw_ref, pscale_ref,
                         y_ref, sg_out, ss_out,
                         e_ref, ep_ref, ys_ref, xs_ref, ye_ref, *, pos0):
    nb, st, t_len = SAMPLE_NB, SEQ_TILE, SAMPLE_T
    rows = nb * st
    tok0 = st - t_len
    base = SUBLANE

    e_ref[...] = jnp.zeros(e_ref.shape, F32)
    ep_ref[...] = jnp.zeros(ep_ref.shape, F32)
    pool_rows = ep_ref.shape[0] // nb
    pool_tok0 = pool_rows - t_len
    for b in range(nb):
        r = base + b * st
        e_ref[r + tok0:r + st, :] = proj_ref[b * t_len:(b + 1) * t_len, :]
        e_ref[r + tok0 - (CONV_K - 1):r + tok0, C_QKV:C_QKV + 3 * D_GDN] = gcv_in[b]
        e_ref[r + tok0 - (CONV_K - 1):r + tok0, C_XBC:C_XBC + SSD_CONV_CH] = scv_in[b]
        pr = b * pool_rows
        ep_ref[pr + pool_tok0 - POOL_BUF:pr + pool_tok0, :] = pool_in[b]
        ep_ref[pr + pool_tok0:pr + pool_rows, :] = proj_ref[b * t_len:(b + 1) * t_len, C_UP:C_UP + D_POOL]

    def erows(c0, width):
        return e_ref[base:base + rows, c0:c0 + width]

    beta, sp, gd = _gates(erows(C_SM, LANE), gp_ref)
    row = lax.broadcasted_iota(jnp.int32, (rows, LANE), 0)
    cum = _seg_cumsum(gd, t_len, (row & (st - 1)) - tok0)
    cum_t = cum.T

    ri = lax.broadcasted_iota(jnp.int32, (rows, rows), 0)
    rj = lax.broadcasted_iota(jnp.int32, (rows, rows), 1)
    same = ((ri >> 3) == (rj >> 3)) & ((ri & (st - 1)) >= tok0) & ((rj & (st - 1)) >= tok0)
    causal, strict = same & (ri >= rj), same & (ri > rj)
    valid_c = (lax.broadcasted_iota(jnp.int32, (rows, 1), 0) & (st - 1)) >= tok0

    def store_y(c0, val):
        ye_ref[:, c0:c0 + val.shape[1]] = val

    for h in range(GDN_HEADS):
        conv = functools.partial(_conv_silu, e_ref, base, rows)
        c0 = h * GDN_HEAD_DIM
        qh = _l2norm(conv(C_QKV + c0, GDN_HEAD_DIM, gcw_ref, c0)) * (GDN_HEAD_DIM ** -0.5)
        kh = _l2norm(conv(C_QKV + D_GDN + c0, GDN_HEAD_DIM, gcw_ref, D_GDN + c0))
        vh = conv(C_QKV + 2 * D_GDN + c0, GDN_HEAD_DIM, gcw_ref, 2 * D_GDN + c0)

        def get_sg(b, h=h):
            return sg_in[b, h]

        def set_sg(b, val, h=h):
            sg_out[b, h] = val

        o = _gdn_head(
            qh, kh, vh,
            beta[:, L_BETA + h:L_BETA + h + 1],
            cum[:, L_GDN_G + h:L_GDN_G + h + 1],
            cum_t[L_GDN_G + h:L_GDN_G + h + 1, :],
            causal, strict, valid_c, 1,
            [(b * st, st, b * st + st - 1, b) for b in range(nb)], get_sg, set_sg)
        store_y(c0, _gdn_finish(o, erows(C_ZG + c0, GDN_HEAD_DIM), gnorm_ref))

    conv_x = functools.partial(_conv_silu, e_ref, base, rows)
    for g in range(SSD_GROUPS):
        bg = conv_x(C_XBC + D_SSD + g * SSD_STATE, SSD_STATE, scw_ref, D_SSD + g * SSD_STATE, scb_ref)
        cg = conv_x(C_XBC + D_SSD + (SSD_GROUPS + g) * SSD_STATE, SSD_STATE, scw_ref,
                    D_SSD + (SSD_GROUPS + g) * SSD_STATE, scb_ref)
        cb = _dot_nt(_bf(cg), _bf(bg))
        for pair in range(SSD_HEADS_PER_GROUP // 2):
            c0 = (g * SSD_HEADS_PER_GROUP + 2 * pair) * SSD_HEAD_DIM
            xs2 = conv_x(C_XBC + c0, 2 * SSD_HEAD_DIM, scw_ref, c0, scb_ref)
            xs_ref[:, c0:c0 + 2 * SSD_HEAD_DIM] = xs2
            for half in range(2):
                hd = g * SSD_HEADS_PER_GROUP + 2 * pair + half
                lane = L_SSD_DT + hd

                def get_ss(b, hd=hd):
                    return ss_in[b, hd]

                def set_ss(b, val, hd=hd):
                    ss_out[b, hd] = val

                y = _ssd_head(
                    xs2[:, half * SSD_HEAD_DIM:(half + 1) * SSD_HEAD_DIM],
                    sp[:, lane:lane + 1], cum[:, lane:lane + 1], cum_t[lane:lane + 1, :],
                    cb, cg, bg, causal, valid_c,
                    [(b * st, st, b * st + st - 1, b) for b in range(nb)], get_ss, set_ss)
                ys_ref[:, hd * SSD_HEAD_DIM:(hd + 1) * SSD_HEAD_DIM] = y

    _ssd_finish(ys_ref, xs_ref, lambda c0, w: erows(C_ZS + c0, w), sd_ref, snorm_ref, store_y)

    tpos = lax.broadcasted_iota(jnp.int32, (st, 1), 0) - tok0
    for gi, win in enumerate(POOL_WINDOWS):
        sl = slice(gi * POOL_GROUP_DIM, (gi + 1) * POOL_GROUP_DIM)
        cnt = jnp.minimum(win, pos0 + tpos + 1).astype(F32)
        tiles = []
        for b in range(nb):
            r = (b + 1) * pool_rows - st
            acc = ep_ref[r:r + st, sl]
            u = acc
            for k in range(1, win):
                acc = acc + ep_ref[r - k:r - k + st, sl]
            tiles.append(acc / cnt - u)
        pooled = jnp.concatenate(tiles, axis=0)
        mixed = _dot(_bf(pooled), pw_ref[gi])
        z = erows(C_ZP + gi * POOL_GROUP_DIM, POOL_GROUP_DIM)
        store_y(D_GDN + D_SSD + gi * POOL_GROUP_DIM, mixed * pscale_ref[0:1, sl] * _silu(z))

    for b in range(nb):
        y_ref[b * t_len:(b + 1) * t_len, :] = ye_ref[b * st + tok0:(b + 1) * st, :]


def _mixer_sample(proj, states, params, layer, n_seq):
    nb, st, t_len = SAMPLE_NB, SEQ_TILE, SAMPLE_T
    rows = nb * st
    state_gdn, state_gdn_conv, state_ssd, state_ssd_conv, state_pool = states
    pool_rows = 3 * SUBLANE
    assert pool_rows - t_len - POOL_BUF >= 1 and (pool_rows - t_len) % SUBLANE == st - t_len

    def seq_spec(tail):
        nd = len(tail)
        return pl.BlockSpec((None, nb) + tuple(tail), lambda i: (layer, i) + (0,) * nd)

    return pl.pallas_call(
        functools.partial(_mixer_sample_kernel, pos0=PAST_LEN),
        grid=(n_seq // nb,),
        in_specs=[
            pl.BlockSpec((nb * t_len, D_PACK), lambda i: (i, 0)),
            seq_spec(state_gdn.shape[2:]),
            seq_spec(state_gdn_conv.shape[2:]),
            seq_spec(state_ssd.shape[2:]),
            seq_spec(state_ssd_conv.shape[2:]),
            seq_spec(state_pool.shape[2:]),
        ] + _param_specs(params, layer),
        out_specs=[
            pl.BlockSpec((nb * t_len, D_MODEL), lambda i: (i, 0)),
            pl.BlockSpec((nb,) + state_gdn.shape[2:], lambda i: (i, 0, 0, 0)),
            pl.BlockSpec((nb,) + state_ssd.shape[2:], lambda i: (i, 0, 0, 0)),
        ],
        out_shape=[
            jax.ShapeDtypeStruct((n_seq * t_len, D_MODEL), F32),
            jax.ShapeDtypeStruct((n_seq,) + state_gdn.shape[2:], F32),
            jax.ShapeDtypeStruct((n_seq,) + state_ssd.shape[2:], F32),
        ],
        scratch_shapes=[
            pltpu.VMEM((SUBLANE + rows, D_PACK), F32),
            pltpu.VMEM((nb * pool_rows, D_POOL), F32),
            pltpu.VMEM((rows, D_SSD), F32),
            pltpu.VMEM((rows, D_SSD), F32),
            pltpu.VMEM((rows, D_MODEL), F32),
        ],
        compiler_params=pltpu.CompilerParams(
            dimension_semantics=("parallel",), vmem_limit_bytes=VMEM_LIMIT),
        name="mixer_sample",
    )(proj, state_gdn, state_gdn_conv, state_ssd, state_ssd_conv, state_pool, *params)


def _pack_w_in(w_in):
    offs = np.cumsum((0,) + IN_SIZES)

    def seg(i):
        return w_in[..., offs[i]:offs[i + 1]]

    pad = jnp.zeros(w_in.shape[:-1] + (LANE - 2 * GDN_HEADS - SSD_HEADS,), w_in.dtype)
    packed = jnp.concatenate([seg(0), seg(1), seg(4), seg(5), seg(7), seg(8), seg(2), seg(3), seg(6), pad], axis=-1)
    return packed.astype(BF16)


def _gate_params(gdn_a_log, gdn_dt_bias, ssd_a_log, ssd_dt_bias):
    def lanes(gdn, ssd):
        z = jnp.zeros((DEPTH, LANE), F32)
        z = z.at[:, L_GDN_G:L_GDN_G + GDN_HEADS].set(gdn)
        return z.at[:, L_SSD_DT:L_SSD_DT + SSD_HEADS].set(ssd)

    rows = jnp.stack([lanes(gdn_a_log, ssd_a_log), lanes(gdn_dt_bias, ssd_dt_bias)], axis=1)
    return jnp.concatenate([rows, jnp.zeros((DEPTH, SUBLANE - 2, LANE), F32)], axis=1)


def kernel(x_prompt, x_sample, state_gdn, state_gdn_conv, state_ssd, state_ssd_conv, state_pool, norm_w, w_in, gdn_conv_w, gdn_a_log, gdn_dt_bias, gdn_norm_w, ssd_conv_w, ssd_conv_b, ssd_a_log, ssd_dt_bias, ssd_d, ssd_norm_w, pool_w, pool_scale, w_out, final_norm_w):
    batch, seq, _ = x_prompt.shape
    n_seq, t_len, _ = x_sample.shape
    assert t_len == SAMPLE_T and seq % PROMPT_TT == 0 and n_seq % SAMPLE_NB == 0

    w_in_p = _pack_w_in(w_in)
    w_out_b = w_out.astype(BF16)
    params = (
        _gate_params(gdn_a_log, gdn_dt_bias, ssd_a_log, ssd_dt_bias),
        gdn_conv_w,
        gdn_norm_w[:, None, :],
        ssd_conv_w,
        ssd_conv_b[:, None, :],
        jnp.repeat(ssd_d, SSD_HEAD_DIM, axis=-1)[:, None, :],
        ssd_norm_w[:, None, :],
        pool_w.astype(BF16),
        pool_scale[:, None, :],
    )
    nw = norm_w[:, None, :]
    fw = final_norm_w[None, :]
    states = (state_gdn, state_gdn_conv, state_ssd, state_ssd_conv, state_pool)

    xp = x_prompt.reshape(batch * seq, D_MODEL)
    xs = x_sample.reshape(n_seq * t_len, D_MODEL)
    new_p = [[] for _ in range(5)]
    new_s = [[] for _ in range(5)]
    for layer in range(DEPTH):
        final = fw if layer == DEPTH - 1 else None

        proj_p = _inproj(xp, nw, w_in_p, layer, 512)
        y_p, gdn_p, ssd_p = _mixer_prompt(proj_p, params, layer, batch, seq)
        xp = _outproj(y_p, w_out_b, xp, layer, 512, final)
        pp = proj_p.reshape(batch, seq, D_PACK)
        new_p[0].append(gdn_p)
        new_p[1].append(pp[:, seq - (CONV_K - 1):, C_QKV:C_QKV + 3 * D_GDN])
        new_p[2].append(ssd_p)
        new_p[3].append(pp[:, seq - (CONV_K - 1):, C_XBC:C_XBC + SSD_CONV_CH])
        new_p[4].append(pp[:, seq - POOL_BUF:, C_UP:C_UP + D_POOL])

        proj_s = _inproj(xs, nw, w_in_p, layer, n_seq * t_len)
        y_s, gdn_s, ssd_s = _mixer_sample(proj_s, states, params, layer, n_seq)
        xs = _outproj(y_s, w_out_b, xs, layer, n_seq * t_len, final)
        ps = proj_s.reshape(n_seq, t_len, D_PACK)
        new_s[0].append(gdn_s)
        new_s[1].append(ps[:, t_len - (CONV_K - 1):, C_QKV:C_QKV + 3 * D_GDN])
        new_s[2].append(ssd_s)
        new_s[3].append(ps[:, t_len - (CONV_K - 1):, C_XBC:C_XBC + SSD_CONV_CH])
        new_s[4].append(jnp.concatenate([state_pool[layer][:, t_len:], ps[:, :, C_UP:C_UP + D_POOL]], axis=1))

    y_prompt = xp.reshape(batch, seq, D_MODEL)
    y_sample = xs.reshape(n_seq, t_len, D_MODEL)
    return (y_prompt, y_sample) + tuple(jnp.stack(a) for a in new_p) + tuple(jnp.stack(a) for a in new_s)
```

```python
import functools

import jax
import jax.numpy as jnp
import numpy as np
from jax import lax
from jax.experimental import pallas as pl
from jax.experimental.pallas import tpu as pltpu

F32 = jnp.float32
BF16 = jnp.bfloat16

D_MODEL = 2048
DEPTH = 4
PAST_LEN = 16384
D_GDN = 768
D_SSD = 768
D_POOL = 512
CONV_K = 4
GDN_HEAD_DIM = 128
GDN_HEADS = 6
GDN_CHUNK = 64
SSD_HEAD_DIM = 64
SSD_HEADS = 12
SSD_STATE = 128
SSD_GROUPS = 2
SSD_HEADS_PER_GROUP = 6
SSD_GROUP_DIM = D_SSD // SSD_GROUPS
SSD_CHUNK = 128
SSD_CONV_CH = 1280
POOL_WINDOWS = (2, 4, 8, 16)
POOL_GROUP_DIM = 128
POOL_BUF = 15
IN_SIZES = (3 * D_GDN, D_GDN, GDN_HEADS, GDN_HEADS, D_SSD, SSD_CONV_CH, SSD_HEADS, D_POOL, D_POOL)
EPS = 1e-6

C_QKV = 0
C_ZG = 2304
C_ZS = 3072
C_XBC = 3840
C_UP = 5120
C_ZP = 5632
C_SM = 6144
D_PACK = 6400
L_BETA = 0
L_GDN_G = 6
L_SSD_DT = 12

LANE = 128
SUBLANE = 8
MXU_WIDTH = 256
VMEM_LIMIT = 48 * 1024 * 1024
VMEM_LIMIT_PROJ = 56 * 1024 * 1024

PROJ_TN = 5 * MXU_WIDTH
PROJ_TM = 1024
PACK_TR = 256
PROMPT_TT = 128
SAMPLE_NB = 8
SAMPLE_T = 4
SEQ_TILE = 8


def _dot(a, b):
    return jnp.dot(a, b, preferred_element_type=F32)


def _dot_nt(a, b):
    return lax.dot_general(a, b, (((1,), (1,)), ((), ())), preferred_element_type=F32)


def _dot_tn(a, b):
    return lax.dot_general(a, b, (((0,), (0,)), ((), ())), preferred_element_type=F32)


def _sigmoid(x):
    return 0.5 + 0.5 * jnp.tanh(0.5 * x)


def _silu(x):
    hx = 0.5 * x
    return hx + hx * jnp.tanh(hx)


def _softplus(x):
    return jnp.maximum(x, 0.0) + jnp.log1p(jnp.exp(-jnp.abs(x)))


def _bf(x):
    return x.astype(BF16)


def _inproj_kernel(x_ref, nw_ref, w_ref, o_ref, h_ref):
    @pl.when(pl.program_id(1) == 0)
    def _():
        x = x_ref[...]
        ms = jnp.mean(x * x, axis=-1, keepdims=True)
        h_ref[...] = _bf(x * lax.rsqrt(ms + EPS) * nw_ref[...])

    o_ref[...] = _dot(h_ref[...], w_ref[...])


def _inproj(x, nw, w_packed, layer, tm):
    n = x.shape[0]
    return pl.pallas_call(
        _inproj_kernel,
        grid=(n // tm, D_PACK // PROJ_TN),
        in_specs=[
            pl.BlockSpec((tm, D_MODEL), lambda i, j: (i, 0)),
            pl.BlockSpec((None, 1, D_MODEL), lambda i, j: (layer, 0, 0)),
            pl.BlockSpec((None, D_MODEL, PROJ_TN), lambda i, j: (layer, 0, j)),
        ],
        out_specs=pl.BlockSpec((tm, PROJ_TN), lambda i, j: (i, j)),
        out_shape=jax.ShapeDtypeStruct((n, D_PACK), F32),
        scratch_shapes=[pltpu.VMEM((tm, D_MODEL), BF16)],
        compiler_params=pltpu.CompilerParams(
            dimension_semantics=("parallel", "arbitrary"), vmem_limit_bytes=VMEM_LIMIT_PROJ),
        name="inproj",
    )(x, nw, w_packed)


def _pack_kernel(w_ref, o_ref):
    offs = np.cumsum((0,) + IN_SIZES)

    def put(c_out, seg):
        o_ref[:, c_out:c_out + IN_SIZES[seg]] = _bf(w_ref[:, offs[seg]:offs[seg + 1]])

    put(C_QKV, 0)
    put(C_ZG, 1)
    put(C_ZS, 4)
    put(C_XBC, 5)
    put(C_UP, 7)
    put(C_ZP, 8)
    o_ref[:, C_SM:D_PACK] = jnp.zeros((o_ref.shape[0], D_PACK - C_SM), BF16)
    small = jnp.concatenate([w_ref[:, offs[s]:offs[s + 1]] for s in (2, 3, 6)], axis=1)
    o_ref[:, C_SM:C_SM + small.shape[1]] = _bf(small)


def _pack_w_in(w_in):
    d_in = w_in.shape[-1]
    return pl.pallas_call(
        _pack_kernel,
        grid=(DEPTH, D_MODEL // PACK_TR),
        in_specs=[pl.BlockSpec((None, PACK_TR, d_in), lambda l, i: (l, i, 0))],
        out_specs=pl.BlockSpec((None, PACK_TR, D_PACK), lambda l, i: (l, i, 0)),
        out_shape=jax.ShapeDtypeStruct((DEPTH, D_MODEL, D_PACK), BF16),
        compiler_params=pltpu.CompilerParams(
            dimension_semantics=("parallel", "parallel"), vmem_limit_bytes=VMEM_LIMIT),
        name="pack_w_in",
    )(w_in)


def _outproj_kernel(y_ref, w_ref, x_ref, o_ref):
    o_ref[...] = x_ref[...] + _dot(_bf(y_ref[...]), w_ref[...])


def _outproj_final_kernel(y_ref, w_ref, x_ref, fw_ref, o_ref):
    x = x_ref[...] + _dot(_bf(y_ref[...]), w_ref[...])
    ms = jnp.mean(x * x, axis=-1, keepdims=True)
    o_ref[...] = x * lax.rsqrt(ms + EPS) * fw_ref[...]


def _outproj(y, w_out, x, layer, tm, final_w=None):
    n = x.shape[0]
    in_specs = [
        pl.BlockSpec((tm, D_MODEL), lambda i: (i, 0)),
        pl.BlockSpec((None, D_MODEL, D_MODEL), lambda i: (layer, 0, 0)),
        pl.BlockSpec((tm, D_MODEL), lambda i: (i, 0)),
    ]
    args = [y, w_out, x]
    body = _outproj_kernel
    if final_w is not None:
        in_specs.append(pl.BlockSpec((1, D_MODEL), lambda i: (0, 0)))
        args.append(final_w)
        body = _outproj_final_kernel
    return pl.pallas_call(
        body,
        grid=(n // tm,),
        in_specs=in_specs,
        out_specs=pl.BlockSpec((tm, D_MODEL), lambda i: (i, 0)),
        out_shape=jax.ShapeDtypeStruct((n, D_MODEL), F32),
        compiler_params=pltpu.CompilerParams(
            dimension_semantics=("parallel",), vmem_limit_bytes=VMEM_LIMIT),
        name="outproj",
    )(*args)


def _conv_silu_tile(ext_ref, hist, c0, width, w_ref, wc0, b_ref=None):
    e = ext_ref[:, c0:c0 + width]
    acc = e * w_ref[0:1, wc0:wc0 + width]
    for j in range(1, CONV_K):
        acc = pltpu.roll(acc, 1, 0) + e * w_ref[j:j + 1, wc0:wc0 + width]
    acc = acc[hist:]
    if b_ref is not None:
        acc = acc + b_ref[0:1, wc0:wc0 + width]
    return _silu(acc)


def _l2norm(x):
    return x * lax.rsqrt(jnp.sum(x * x, axis=-1, keepdims=True) + EPS)


def _seg_cumsum(x, seg_len, row_in_seg):
    s = 1
    while s < seg_len:
        x = x + jnp.where(row_in_seg >= s, pltpu.roll(x, s, 0), 0.0)
        s *= 2
    return x


def _gates(small, gp_ref):
    sp = _softplus(small + gp_ref[1:2, :])
    gd = -jnp.exp(gp_ref[0:1, :]) * sp
    return _sigmoid(small), sp, gd


def _masked_decay(col, row, causal):
    return jnp.where(causal, jnp.exp(jnp.where(causal, col - row, 0.0)), 0.0)


def _gdn_finish(o, z, gnorm_ref):
    on = o * lax.rsqrt(jnp.mean(o * o, axis=-1, keepdims=True) + EPS) * gnorm_ref[0:1, :]
    return on * _silu(z)


def _mixer_prompt_kernel(proj_ref, gp_ref, gcw_ref, gnorm_ref, scw_ref, scb_ref, sd_ref, snorm_ref,
                         pw_ref, pscale_ref,
                         y_ref, sg_ref, ss_ref,
                         eq_ref, ex_ref, ep_ref, *, pos0):
    tt = PROMPT_TT
    n_chunks = tt // GDN_CHUNK
    hist, phist = SUBLANE, 2 * SUBLANE
    i = pl.program_id(1)

    @pl.when(i == 0)
    def _():
        sg_ref[...] = jnp.zeros(sg_ref.shape, F32)
        ss_ref[...] = jnp.zeros(ss_ref.shape, F32)
        eq_ref[0:hist, :] = jnp.zeros((hist, eq_ref.shape[1]), F32)
        ex_ref[0:hist, :] = jnp.zeros((hist, ex_ref.shape[1]), F32)
        ep_ref[0:phist, :] = jnp.zeros((phist, D_POOL), F32)

    eq_ref[hist:hist + tt, :] = proj_ref[:, C_QKV:C_QKV + 3 * D_GDN]
    ex_ref[hist:hist + tt, :] = proj_ref[:, C_XBC:C_XBC + SSD_CONV_CH]
    ep_ref[phist:phist + tt, :] = proj_ref[:, C_UP:C_UP + D_POOL]

    beta, sp, gd = _gates(proj_ref[:, C_SM:C_SM + LANE], gp_ref)
    row = lax.broadcasted_iota(jnp.int32, (tt, LANE), 0)
    gc = _seg_cumsum(gd, GDN_CHUNK, row & (GDN_CHUNK - 1))
    ac = _seg_cumsum(gd, SSD_CHUNK, row)

    def store_y(c0, val, r0=0):
        y_ref[r0:r0 + val.shape[0], c0:c0 + val.shape[1]] = val.astype(y_ref.dtype)

    ci = lax.broadcasted_iota(jnp.int32, (GDN_CHUNK, GDN_CHUNK), 0)
    cj = lax.broadcasted_iota(jnp.int32, (GDN_CHUNK, GDN_CHUNK), 1)
    causal_g, strict_g = ci >= cj, ci > cj
    gc_t = [gc[c * GDN_CHUNK:(c + 1) * GDN_CHUNK].T for c in range(n_chunks)]
    conv_q = functools.partial(_conv_silu_tile, eq_ref, hist)

    keys = [(c, h) for c in range(n_chunks) for h in range(GDN_HEADS)]
    nmat, pmat, rhs, qkb, qdec, kdec_t, glast = {}, {}, {}, {}, {}, {}, {}
    for h in range(GDN_HEADS):
        c0 = h * GDN_HEAD_DIM
        q_all = _l2norm(conv_q(c0, GDN_HEAD_DIM, gcw_ref, c0)) * (GDN_HEAD_DIM ** -0.5)
        k_all = _l2norm(conv_q(D_GDN + c0, GDN_HEAD_DIM, gcw_ref, D_GDN + c0))
        v_all = conv_q(2 * D_GDN + c0, GDN_HEAD_DIM, gcw_ref, 2 * D_GDN + c0)
        for c in range(n_chunks):
            rs = slice(c * GDN_CHUNK, (c + 1) * GDN_CHUNK)
            qh, kh, vh = q_all[rs], k_all[rs], v_all[rs]
            beta_c = beta[rs, L_BETA + h:L_BETA + h + 1]
            gc_c = gc[rs, L_GDN_G + h:L_GDN_G + h + 1]
            gc_r = gc_t[c][L_GDN_G + h:L_GDN_G + h + 1, :]
            gl = gc_c[GDN_CHUNK - 1:GDN_CHUNK, :]
            eg = jnp.exp(gc_c)
            decay = _masked_decay(gc_c, gc_r, causal_g)
            kb, qb = _bf(kh), _bf(qh)
            nmat[c, h] = jnp.where(strict_g, -(_dot_nt(kb, kb) * decay * beta_c), 0.0)
            qkb[c, h] = _bf(jnp.where(causal_g, _dot_nt(qb, kb) * decay, 0.0))
            rhs[c, h] = jnp.concatenate([vh * beta_c, kh * (beta_c * eg)], axis=-1)
            qdec[c, h] = qh * eg
            kdec_t[c, h] = _bf((kh * jnp.exp(gl - gc_c)).T)
            glast[c, h] = jnp.exp(gl)

    for key in keys:
        bb = _bf(nmat[key])
        pmat[key] = _dot(bb, bb)
    for _ in range(4):
        for key in keys:
            pb = _bf(pmat[key])
            r = _dot(jnp.concatenate([_bf(nmat[key]), pb], axis=0), pb)
            nmat[key] = nmat[key] + pmat[key] + r[:GDN_CHUNK]
            pmat[key] = r[GDN_CHUNK:]
    sol = {}
    for key in keys:
        nmat[key] = nmat[key] + pmat[key] + _dot(_bf(nmat[key]), _bf(pmat[key]))
    for key in keys:
        sol[key] = rhs[key] + _dot(_bf(nmat[key]), _bf(rhs[key]))

    for c in range(n_chunks):
        r0 = c * GDN_CHUNK
        for h in range(GDN_HEADS):
            key = (c, h)
            state = sg_ref[0, h]
            u, w = sol[key][:, :GDN_HEAD_DIM], sol[key][:, GDN_HEAD_DIM:]
            r1 = _dot(_bf(jnp.concatenate([w, qdec[key]], axis=0)), _bf(state))
            vnb = _bf(u - r1[:GDN_CHUNK])
            r2 = _dot(jnp.concatenate([qkb[key], kdec_t[key]], axis=0), vnb)
            sg_ref[0, h] = state * glast[key] + r2[GDN_CHUNK:]
            o = r1[GDN_CHUNK:] + r2[:GDN_CHUNK]
            c0 = h * GDN_HEAD_DIM
            z = proj_ref[r0:r0 + GDN_CHUNK, C_ZG + c0:C_ZG + c0 + GDN_HEAD_DIM]
            store_y(c0, _gdn_finish(o, z, gnorm_ref), r0)

    si = lax.broadcasted_iota(jnp.int32, (SSD_CHUNK, SSD_CHUNK), 0)
    sj = lax.broadcasted_iota(jnp.int32, (SSD_CHUNK, SSD_CHUNK), 1)
    causal_s = si >= sj
    lo_lanes = sj < SSD_HEAD_DIM
    lo_rows = si < SSD_HEAD_DIM
    ac_t = ac.T
    conv_x = functools.partial(_conv_silu_tile, ex_ref, hist)
    for g in range(SSD_GROUPS):
        bgb = _bf(conv_x(D_SSD + g * SSD_STATE, SSD_STATE, scw_ref, D_SSD + g * SSD_STATE, scb_ref))
        cgb = _bf(conv_x(D_SSD + (SSD_GROUPS + g) * SSD_STATE, SSD_STATE, scw_ref,
                         D_SSD + (SSD_GROUPS + g) * SSD_STATE, scb_ref))
        cb = _dot_nt(cgb, bgb)
        yz, ssq = [], None
        for pair in range(SSD_HEADS_PER_GROUP // 2):
            hd0 = g * SSD_HEADS_PER_GROUP + 2 * pair
            c0 = hd0 * SSD_HEAD_DIM
            la, lb = L_SSD_DT + hd0, L_SSD_DT + hd0 + 1
            xs2 = conv_x(c0, 2 * SSD_HEAD_DIM, scw_ref, c0, scb_ref)
            a2 = jnp.where(lo_lanes, ac[:, la:la + 1], ac[:, lb:lb + 1])
            xdt2 = xs2 * jnp.where(lo_lanes, sp[:, la:la + 1], sp[:, lb:lb + 1])
            alast2 = a2[tt - 1:tt, :]
            state2 = ss_ref[0, hd0:hd0 + 2].reshape(2 * SSD_HEAD_DIM, SSD_STATE)
            y = jnp.exp(a2) * _dot_nt(cgb, _bf(state2))
            seg_a = _masked_decay(ac[:, la:la + 1], ac_t[la:la + 1, :], causal_s)
            seg_b = _masked_decay(ac[:, lb:lb + 1], ac_t[lb:lb + 1, :], causal_s)
            y = y + _dot(_bf(cb * seg_a), _bf(jnp.where(lo_lanes, xdt2, 0.0)))
            y = y + _dot(_bf(cb * seg_b), _bf(jnp.where(lo_lanes, 0.0, xdt2)))
            upd = _dot_tn(_bf(xdt2 * jnp.exp(alast2 - a2)), bgb)
            elast = jnp.exp(jnp.where(lo_rows, ac[tt - 1:tt, la:la + 1], ac[tt - 1:tt, lb:lb + 1]))
            ss_ref[0, hd0:hd0 + 2] = (state2 * elast + upd).reshape(2, SSD_HEAD_DIM, SSD_STATE)
            z = proj_ref[:, C_ZS + c0:C_ZS + c0 + 2 * SSD_HEAD_DIM]
            t = (y + sd_ref[0:1, c0:c0 + 2 * SSD_HEAD_DIM] * xs2) * _silu(z)
            yz.append(t)
            s = jnp.sum(t * t, axis=-1, keepdims=True)
            ssq = s if ssq is None else ssq + s
        scale = lax.rsqrt(ssq * (1.0 / SSD_GROUP_DIM) + EPS)
        for pair, t in enumerate(yz):
            c0 = (g * SSD_HEADS_PER_GROUP + 2 * pair) * SSD_HEAD_DIM
            store_y(D_GDN + c0, t * scale * snorm_ref[0:1, c0:c0 + 2 * SSD_HEAD_DIM])

    pos = pos0 + i * tt + lax.broadcasted_iota(jnp.int32, (tt, 1), 0)
    for gi, win in enumerate(POOL_WINDOWS):
        sl = slice(gi * POOL_GROUP_DIM, (gi + 1) * POOL_GROUP_DIM)
        e = ep_ref[:, sl]
        acc, k = e, 1
        while k < win:
            acc = acc + pltpu.roll(acc, k, 0)
            k *= 2
        cnt = jnp.minimum(win, pos + 1).astype(F32)
        pooled = acc[phist:] / cnt - e[phist:]
        mixed = _dot(_bf(pooled), pw_ref[gi])
        z = proj_ref[:, C_ZP + gi * POOL_GROUP_DIM:C_ZP + (gi + 1) * POOL_GROUP_DIM]
        store_y(D_GDN + D_SSD + gi * POOL_GROUP_DIM, mixed * pscale_ref[0:1, sl] * _silu(z))

    eq_ref[0:hist, :] = eq_ref[tt:tt + hist, :]
    ex_ref[0:hist, :] = ex_ref[tt:tt + hist, :]
    ep_ref[0:phist, :] = ep_ref[tt:tt + phist, :]


def _layer_spec(shape):
    nd = len(shape)
    return lambda layer: pl.BlockSpec((None,) + tuple(shape[1:]), lambda *_: (layer,) + (0,) * (nd - 1))


def _param_specs(params, layer):
    return [_layer_spec(p.shape)(layer) for p in params]


def _mixer_prompt(proj, params, layer, batch, seq):
    tt = PROMPT_TT
    nt = seq // tt
    return pl.pallas_call(
        functools.partial(_mixer_prompt_kernel, pos0=0),
        grid=(batch, nt),
        in_specs=[pl.BlockSpec((tt, D_PACK), lambda b, i: (b * nt + i, 0))] + _param_specs(params, layer),
        out_specs=[
            pl.BlockSpec((tt, D_MODEL), lambda b, i: (b * nt + i, 0)),
            pl.BlockSpec((1, GDN_HEADS, GDN_HEAD_DIM, GDN_HEAD_DIM), lambda b, i: (b, 0, 0, 0)),
            pl.BlockSpec((1, SSD_HEADS, SSD_HEAD_DIM, SSD_STATE), lambda b, i: (b, 0, 0, 0)),
        ],
        out_shape=[
            jax.ShapeDtypeStruct((batch * seq, D_MODEL), BF16),
            jax.ShapeDtypeStruct((batch, GDN_HEADS, GDN_HEAD_DIM, GDN_HEAD_DIM), F32),
            jax.ShapeDtypeStruct((batch, SSD_HEADS, SSD_HEAD_DIM, SSD_STATE), F32),
        ],
        scratch_shapes=[
            pltpu.VMEM((SUBLANE + tt, 3 * D_GDN), F32),
            pltpu.VMEM((SUBLANE + tt, SSD_CONV_CH), F32),
            pltpu.VMEM((2 * SUBLANE + tt, D_POOL), F32),
        ],
        compiler_params=pltpu.CompilerParams(
            dimension_semantics=("parallel", "arbitrary"), vmem_limit_bytes=VMEM_LIMIT),
        name="mixer_prompt",
    )(proj, *params)


def _mixer_sample_kernel(*refs, n_alias, pos0):
    proj_ref, sg_in, gcv_in, ss_in, scv_in, pool_in = refs[:6]
    (gp_ref, gcw_ref, gnorm_ref, scw_ref, scb_ref, sd_ref, snorm_ref, pw_ref, pscale_ref,
     y_ref, sg_out, gcv_out, ss_out, scv_out, pool_out,
     e_ref, ep_ref, ye_ref) = refs[6 + n_alias:]
    nb, st, t_len = SAMPLE_NB, SEQ_TILE, SAMPLE_T
    rows = nb * st
    tok0 = st - t_len
    base = SUBLANE
    seq_rows = [slice(b * st, (b + 1) * st) for b in range(nb)]

    e_ref[...] = jnp.zeros(e_ref.shape, F32)
    ep_ref[...] = jnp.zeros(ep_ref.shape, F32)
    pool_rows = ep_ref.shape[0] // nb
    pool_tok0 = pool_rows - t_len
    for b in range(nb):
        r = base + b * st
        e_ref[r + tok0:r + st, :] = proj_ref[b * t_len:(b + 1) * t_len, :]
        e_ref[r + tok0 - (CONV_K - 1):r + tok0, C_QKV:C_QKV + 3 * D_GDN] = gcv_in[b]
        e_ref[r + tok0 - (CONV_K - 1):r + tok0, C_XBC:C_XBC + SSD_CONV_CH] = scv_in[b]
        pr = b * pool_rows
        ep_ref[pr + pool_tok0 - POOL_BUF:pr + pool_tok0, :] = pool_in[b]
        ep_ref[pr + pool_tok0:pr + pool_rows, :] = proj_ref[b * t_len:(b + 1) * t_len, C_UP:C_UP + D_POOL]

    def erows(c0, width):
        return e_ref[base:base + rows, c0:c0 + width]

    beta, sp, gd = _gates(erows(C_SM, LANE), gp_ref)
    row = lax.broadcasted_iota(jnp.int32, (rows, LANE), 0)
    cum = _seg_cumsum(gd, t_len, (row & (st - 1)) - tok0)
    cum_t = cum.T

    ri = lax.broadcasted_iota(jnp.int32, (rows, rows), 0)
    rj = lax.broadcasted_iota(jnp.int32, (rows, rows), 1)
    same = ((ri >> 3) == (rj >> 3)) & ((ri & (st - 1)) >= tok0) & ((rj & (st - 1)) >= tok0)
    causal, strict = same & (ri >= rj), same & (ri > rj)
    valid_c = (lax.broadcasted_iota(jnp.int32, (rows, 1), 0) & (st - 1)) >= tok0

    def store_y(c0, val):
        ye_ref[:, c0:c0 + val.shape[1]] = val

    conv_e = functools.partial(_conv_silu_tile, e_ref, base)
    for h in range(GDN_HEADS):
        c0 = h * GDN_HEAD_DIM
        qh = _l2norm(conv_e(C_QKV + c0, GDN_HEAD_DIM, gcw_ref, c0)) * (GDN_HEAD_DIM ** -0.5)
        kh = _l2norm(conv_e(C_QKV + D_GDN + c0, GDN_HEAD_DIM, gcw_ref, D_GDN + c0))
        vh = conv_e(C_QKV + 2 * D_GDN + c0, GDN_HEAD_DIM, gcw_ref, 2 * D_GDN + c0)
        beta_c = beta[:, L_BETA + h:L_BETA + h + 1]
        gc_c = cum[:, L_GDN_G + h:L_GDN_G + h + 1]
        eg = jnp.exp(gc_c)
        decay = _masked_decay(gc_c, cum_t[L_GDN_G + h:L_GDN_G + h + 1, :], causal)
        kb, qb = _bf(kh), _bf(qh)
        bmat = jnp.where(strict, -(_dot_nt(kb, kb) * decay * beta_c), 0.0)
        bb = _bf(bmat)
        p1 = _dot(bb, bb)
        nmat = bmat + p1 + _dot(bb, _bf(p1))
        rhs = jnp.concatenate([vh * beta_c, kh * (beta_c * eg)], axis=-1)
        sol = rhs + _dot(_bf(nmat), _bf(rhs))
        u, w = sol[:, :GDN_HEAD_DIM], sol[:, GDN_HEAD_DIM:]
        qkb = _bf(jnp.where(causal, _dot_nt(qb, kb) * decay, 0.0))
        qdec = qh * eg
        r1 = [_dot(_bf(jnp.concatenate([w[rs], qdec[rs]], axis=0)), _bf(sg_in[b, h]))
              for b, rs in enumerate(seq_rows)]
        w_state = jnp.concatenate([r[:st] for r in r1], axis=0)
        o_state = jnp.concatenate([r[st:] for r in r1], axis=0)
        vn = jnp.where(valid_c, u - w_state, 0.0)
        o = o_state + _dot(qkb, _bf(vn))
        store_y(c0, _gdn_finish(o, erows(C_ZG + c0, GDN_HEAD_DIM), gnorm_ref))
        for b, rs in enumerate(seq_rows):
            gl = gc_c[(b + 1) * st - 1:(b + 1) * st, :]
            kd = jnp.where(valid_c[rs], kh[rs] * jnp.exp(gl - gc_c[rs]), 0.0)
            sg_out[b, h] = sg_in[b, h] * jnp.exp(gl) + _dot_tn(_bf(kd), _bf(vn[rs]))

    lo_lanes = lax.broadcasted_iota(jnp.int32, (rows, 2 * SSD_HEAD_DIM), 1) < SSD_HEAD_DIM
    lo_rows = lax.broadcasted_iota(jnp.int32, (2 * SSD_HEAD_DIM, SSD_STATE), 0) < SSD_HEAD_DIM
    for g in range(SSD_GROUPS):
        bg = conv_e(C_XBC + D_SSD + g * SSD_STATE, SSD_STATE, scw_ref, D_SSD + g * SSD_STATE, scb_ref)
        cg = conv_e(C_XBC + D_SSD + (SSD_GROUPS + g) * SSD_STATE, SSD_STATE, scw_ref,
                    D_SSD + (SSD_GROUPS + g) * SSD_STATE, scb_ref)
        cb = _dot_nt(_bf(cg), _bf(bg))
        bgb = [_bf(bg[rs]) for rs in seq_rows]
        cgb = [_bf(cg[rs]) for rs in seq_rows]
        yz, ssq = [], None
        for pair in range(SSD_HEADS_PER_GROUP // 2):
            hd0 = g * SSD_HEADS_PER_GROUP + 2 * pair
            c0 = hd0 * SSD_HEAD_DIM
            la, lb = L_SSD_DT + hd0, L_SSD_DT + hd0 + 1
            xs2 = conv_e(C_XBC + c0, 2 * SSD_HEAD_DIM, scw_ref, c0, scb_ref)
            a2 = jnp.where(lo_lanes, cum[:, la:la + 1], cum[:, lb:lb + 1])
            xdt2 = xs2 * jnp.where(lo_lanes, sp[:, la:la + 1], sp[:, lb:lb + 1])
            seg_a = _masked_decay(cum[:, la:la + 1], cum_t[la:la + 1, :], causal)
            seg_b = _masked_decay(cum[:, lb:lb + 1], cum_t[lb:lb + 1, :], causal)
            y = _dot(_bf(cb * seg_a), _bf(jnp.where(lo_lanes, xdt2, 0.0)))
            y = y + _dot(_bf(cb * seg_b), _bf(jnp.where(lo_lanes, 0.0, xdt2)))
            y_off = [_dot_nt(cgb[b], _bf(ss_in[b, hd0:hd0 + 2].reshape(2 * SSD_HEAD_DIM, SSD_STATE)))
                     for b in range(nb)]
            y = y + jnp.exp(a2) * jnp.concatenate(y_off, axis=0)
            for b, rs in enumerate(seq_rows):
                last = (b + 1) * st - 1
                xin = jnp.where(valid_c[rs], xdt2[rs] * jnp.exp(a2[last:last + 1, :] - a2[rs]), 0.0)
                elast = jnp.exp(jnp.where(lo_rows, cum[last:last + 1, la:la + 1], cum[last:last + 1, lb:lb + 1]))
                state2 = ss_in[b, hd0:hd0 + 2].reshape(2 * SSD_HEAD_DIM, SSD_STATE)
                ss_out[b, hd0:hd0 + 2] = (state2 * elast + _dot_tn(_bf(xin), bgb[b])).reshape(
                    2, SSD_HEAD_DIM, SSD_STATE)
            t = (y + sd_ref[0:1, c0:c0 + 2 * SSD_HEAD_DIM] * xs2) * _silu(erows(C_ZS + c0, 2 * SSD_HEAD_DIM))
            yz.append(t)
            s = jnp.sum(t * t, axis=-1, keepdims=True)
            ssq = s if ssq is None else ssq + s
        scale = lax.rsqrt(ssq * (1.0 / SSD_GROUP_DIM) + EPS)
        for pair, t in enumerate(yz):
            c0 = (g * SSD_HEADS_PER_GROUP + 2 * pair) * SSD_HEAD_DIM
            store_y(D_GDN + c0, t * scale * snorm_ref[0:1, c0:c0 + 2 * SSD_HEAD_DIM])

    tpos = lax.broadcasted_iota(jnp.int32, (st, 1), 0) - tok0
    for gi, win in enumerate(POOL_WINDOWS):
        sl = slice(gi * POOL_GROUP_DIM, (gi + 1) * POOL_GROUP_DIM)
        cnt = jnp.minimum(win, pos0 + tpos + 1).astype(F32)
        tiles = []
        for b in range(nb):
            r = (b + 1) * pool_rows - st
            acc = ep_ref[r:r + st, sl]
            u = acc
            for k in range(1, win):
                acc = acc + ep_ref[r - k:r - k + st, sl]
            tiles.append(acc / cnt - u)
        pooled = jnp.concatenate(tiles, axis=0)
        mixed = _dot(_bf(pooled), pw_ref[gi])
        z = erows(C_ZP + gi * POOL_GROUP_DIM, POOL_GROUP_DIM)
        store_y(D_GDN + D_SSD + gi * POOL_GROUP_DIM, mixed * pscale_ref[0:1, sl] * _silu(z))

    for b in range(nb):
        r = base + b * st
        y_ref[b * t_len:(b + 1) * t_len, :] = ye_ref[b * st + tok0:(b + 1) * st, :]
        gcv_out[b] = e_ref[r + st - (CONV_K - 1):r + st, C_QKV:C_QKV + 3 * D_GDN]
        scv_out[b] = e_ref[r + st - (CONV_K - 1):r + st, C_XBC:C_XBC + SSD_CONV_CH]
        pool_out[b] = ep_ref[(b + 1) * pool_rows - POOL_BUF:(b + 1) * pool_rows, :]


def _mixer_sample(proj, states, params, layer, n_seq, prev_outs):
    nb, st, t_len = SAMPLE_NB, SEQ_TILE, SAMPLE_T
    rows = nb * st
    pool_rows = 3 * SUBLANE
    assert pool_rows - t_len - POOL_BUF >= 1 and (pool_rows - t_len) % SUBLANE == st - t_len
    assert CONV_K - 1 <= t_len
    prev_outs = () if prev_outs is None else tuple(prev_outs)
    n_state = len(states)

    def seq_spec(s):
        tail = tuple(s.shape[2:])
        return pl.BlockSpec((None, nb) + tail, lambda i: (layer, i) + (0,) * len(tail))

    return pl.pallas_call(
        functools.partial(_mixer_sample_kernel, n_alias=len(prev_outs), pos0=PAST_LEN),
        grid=(n_seq // nb,),
        in_specs=[pl.BlockSpec((nb * t_len, D_PACK), lambda i: (i, 0))]
        + [seq_spec(s) for s in states]
        + [pl.BlockSpec(memory_space=pl.ANY)] * len(prev_outs)
        + _param_specs(params, layer),
        out_specs=[pl.BlockSpec((nb * t_len, D_MODEL), lambda i: (i, 0))] + [seq_spec(s) for s in states],
        out_shape=[jax.ShapeDtypeStruct((n_seq * t_len, D_MODEL), F32)]
        + [jax.ShapeDtypeStruct(s.shape, F32) for s in states],
        input_output_aliases={1 + n_state + k: 1 + k for k in range(len(prev_outs))},
        scratch_shapes=[
            pltpu.VMEM((SUBLANE + rows, D_PACK), F32),
            pltpu.VMEM((nb * pool_rows, D_POOL), F32),
            pltpu.VMEM((rows, D_MODEL), F32),
        ],
        compiler_params=pltpu.CompilerParams(
            dimension_semantics=("parallel",), vmem_limit_bytes=VMEM_LIMIT),
        name="mixer_sample",
    )(proj, *states, *prev_outs, *params)


def _gate_params(gdn_a_log, gdn_dt_bias, ssd_a_log, ssd_dt_bias):
    def lanes(gdn, ssd):
        z = jnp.zeros((DEPTH, LANE), F32)
        z = z.at[:, L_GDN_G:L_GDN_G + GDN_HEADS].set(gdn)
        return z.at[:, L_SSD_DT:L_SSD_DT + SSD_HEADS].set(ssd)

    rows = jnp.stack([lanes(gdn_a_log, ssd_a_log), lanes(gdn_dt_bias, ssd_dt_bias)], axis=1)
    return jnp.concatenate([rows, jnp.zeros((DEPTH, SUBLANE - 2, LANE), F32)], axis=1)


def kernel(x_prompt, x_sample, state_gdn, state_gdn_conv, state_ssd, state_ssd_conv, state_pool, norm_w, w_in, gdn_conv_w, gdn_a_log, gdn_dt_bias, gdn_norm_w, ssd_conv_w, ssd_conv_b, ssd_a_log, ssd_dt_bias, ssd_d, ssd_norm_w, pool_w, pool_scale, w_out, final_norm_w):
    batch, seq, _ = x_prompt.shape
    n_seq, t_len, _ = x_sample.shape
    assert t_len == SAMPLE_T and seq % PROMPT_TT == 0 and n_seq % SAMPLE_NB == 0

    w_in_p = _pack_w_in(w_in)
    w_out_b = w_out.astype(BF16)
    params = (
        _gate_params(gdn_a_log, gdn_dt_bias, ssd_a_log, ssd_dt_bias),
        gdn_conv_w,
        gdn_norm_w[:, None, :],
        ssd_conv_w,
        ssd_conv_b[:, None, :],
        jnp.repeat(ssd_d, SSD_HEAD_DIM, axis=-1)[:, None, :],
        ssd_norm_w[:, None, :],
        pool_w.astype(BF16),
        pool_scale[:, None, :],
    )
    nw = norm_w[:, None, :]
    fw = final_norm_w[None, :]
    states = (state_gdn, state_gdn_conv, state_ssd, state_ssd_conv, state_pool)

    xp = x_prompt.reshape(batch * seq, D_MODEL)
    xs = x_sample.reshape(n_seq * t_len, D_MODEL)
    new_p = [[] for _ in range(5)]
    new_s = None
    for layer in range(DEPTH):
        final = fw if layer == DEPTH - 1 else None

        proj_p = _inproj(xp, nw, w_in_p, layer, PROJ_TM)
        y_p, gdn_p, ssd_p = _mixer_prompt(proj_p, params, layer, batch, seq)
        xp = _outproj(y_p, w_out_b, xp, layer, 512, final)
        pp = proj_p.reshape(batch, seq, D_PACK)
        new_p[0].append(gdn_p)
        new_p[1].append(pp[:, seq - (CONV_K - 1):, C_QKV:C_QKV + 3 * D_GDN])
        new_p[2].append(ssd_p)
        new_p[3].append(pp[:, seq - (CONV_K - 1):, C_XBC:C_XBC + SSD_CONV_CH])
        new_p[4].append(pp[:, seq - POOL_BUF:, C_UP:C_UP + D_POOL])

        proj_s = _inproj(xs, nw, w_in_p, layer, n_seq * t_len)
        y_s, *new_s = _mixer_sample(proj_s, states, params, layer, n_seq, new_s)
        xs = _outproj(y_s, w_out_b, xs, layer, n_seq * t_len, final)

    y_prompt = xp.reshape(batch, seq, D_MODEL)
    y_sample = xs.reshape(n_seq, t_len, D_MODEL)
    return (y_prompt, y_sample) + tuple(jnp.stack(a) for a in new_p) + tuple(new_s)
```

```python
import functools

import jax
import jax.numpy as jnp
import numpy as np
from jax import lax
from jax.experimental import pallas as pl
from jax.experimental.pallas import tpu as pltpu

F32 = jnp.float32
BF16 = jnp.bfloat16

D_MODEL = 2048
DEPTH = 4
PAST_LEN = 16384
D_GDN = 768
D_SSD = 768
D_POOL = 512
CONV_K = 4
GDN_HEAD_DIM = 128
GDN_HEADS = 6
GDN_CHUNK = 64
SSD_HEAD_DIM = 64
SSD_HEADS = 12
SSD_STATE = 128
SSD_GROUPS = 2
SSD_HEADS_PER_GROUP = 6
SSD_GROUP_DIM = D_SSD // SSD_GROUPS
SSD_CHUNK = 128
SSD_CONV_CH = 1280
POOL_WINDOWS = (2, 4, 8, 16)
POOL_GROUP_DIM = 128
POOL_BUF = 15
IN_SIZES = (3 * D_GDN, D_GDN, GDN_HEADS, GDN_HEADS, D_SSD, SSD_CONV_CH, SSD_HEADS, D_POOL, D_POOL)
EPS = 1e-6

C_QKV = 0
C_ZG = 2304
C_ZS = 3072
C_XBC = 3840
C_UP = 5120
C_ZP = 5632
C_SM = 6144
D_PACK = 6400
L_BETA = 0
L_GDN_G = 6
L_SSD_DT = 16

LANE = 128
SUBLANE = 8
MXU_WIDTH = 256
VMEM_LIMIT = 48 * 1024 * 1024
VMEM_LIMIT_PROJ = 56 * 1024 * 1024

PROJ_TN = 5 * MXU_WIDTH
PROJ_TM = 1024
PACK_TK = 256
PROMPT_TT = 128
SAMPLE_NB = 8
SAMPLE_T = 4


def _dot(a, b):
    return jnp.dot(a, b, preferred_element_type=F32)


def _dot_nt(a, b):
    return lax.dot_general(a, b, (((1,), (1,)), ((), ())), preferred_element_type=F32)


def _dot_tn(a, b):
    return lax.dot_general(a, b, (((0,), (0,)), ((), ())), preferred_element_type=F32)


def _sigmoid(x):
    return 0.5 + 0.5 * jnp.tanh(0.5 * x)


def _silu(x):
    hx = 0.5 * x
    return hx + hx * jnp.tanh(hx)


def _softplus(x):
    return jnp.maximum(x, 0.0) + jnp.log(1.0 + jnp.exp(-jnp.abs(x)))


def _bf(x):
    return x.astype(BF16)


def _inproj_kernel(x_ref, nw_ref, w_ref, o_ref, h_ref):
    @pl.when(pl.program_id(1) == 0)
    def _():
        x = x_ref[...]
        ms = jnp.mean(x * x, axis=-1, keepdims=True)
        h_ref[...] = _bf(x * lax.rsqrt(ms + EPS) * nw_ref[...])

    o_ref[...] = _dot_nt(h_ref[...], w_ref[...])


def _inproj(x, nw, w_packed_t, layer, tm):
    n = x.shape[0]
    return pl.pallas_call(
        _inproj_kernel,
        grid=(n // tm, D_PACK // PROJ_TN),
        in_specs=[
            pl.BlockSpec((tm, D_MODEL), lambda i, j: (i, 0)),
            pl.BlockSpec((None, 1, D_MODEL), lambda i, j: (layer, 0, 0)),
            pl.BlockSpec((None, PROJ_TN, D_MODEL), lambda i, j: (layer, j, 0)),
        ],
        out_specs=pl.BlockSpec((tm, PROJ_TN), lambda i, j: (i, j)),
        out_shape=jax.ShapeDtypeStruct((n, D_PACK), F32),
        scratch_shapes=[pltpu.VMEM((tm, D_MODEL), BF16)],
        compiler_params=pltpu.CompilerParams(
            dimension_semantics=("parallel", "arbitrary"), vmem_limit_bytes=VMEM_LIMIT_PROJ),
        name="inproj",
    )(x, nw, w_packed_t)


def _pack_kernel(w_ref, o_ref):
    offs = np.cumsum((0,) + IN_SIZES)

    def put(r_out, r_in, n):
        o_ref[r_out:r_out + n, :] = _bf(w_ref[r_in:r_in + n, :])

    for seg, r_out in ((0, C_QKV), (1, C_ZG), (4, C_ZS), (5, C_XBC), (7, C_UP), (8, C_ZP)):
        put(r_out, offs[seg], IN_SIZES[seg])
    o_ref[C_SM:D_PACK, :] = jnp.zeros((D_PACK - C_SM, o_ref.shape[1]), BF16)
    assert offs[3] == offs[2] + GDN_HEADS and L_GDN_G == L_BETA + GDN_HEADS
    put(C_SM + L_BETA, offs[2], 2 * GDN_HEADS)
    put(C_SM + L_SSD_DT, offs[6], SSD_HEADS)


def _pack_w_in(w_in_t):
    d_in = w_in_t.shape[1]
    return pl.pallas_call(
        _pack_kernel,
        grid=(DEPTH, D_MODEL // PACK_TK),
        in_specs=[pl.BlockSpec((None, d_in, PACK_TK), lambda l, i: (l, 0, i))],
        out_specs=pl.BlockSpec((None, D_PACK, PACK_TK), lambda l, i: (l, 0, i)),
        out_shape=jax.ShapeDtypeStruct((DEPTH, D_PACK, D_MODEL), BF16),
        compiler_params=pltpu.CompilerParams(
            dimension_semantics=("parallel", "parallel"), vmem_limit_bytes=VMEM_LIMIT),
        name="pack_w_in",
    )(w_in_t)


def _outproj_kernel(y_ref, w_ref, x_ref, o_ref):
    o_ref[...] = x_ref[...] + _dot(_bf(y_ref[...]), w_ref[...])


def _outproj_final_kernel(y_ref, w_ref, x_ref, fw_ref, o_ref):
    x = x_ref[...] + _dot(_bf(y_ref[...]), w_ref[...])
    ms = jnp.mean(x * x, axis=-1, keepdims=True)
    o_ref[...] = x * lax.rsqrt(ms + EPS) * fw_ref[...]


def _outproj(y, w_out, x, layer, tm, final_w=None):
    n = x.shape[0]
    in_specs = [
        pl.BlockSpec((tm, D_MODEL), lambda i: (i, 0)),
        pl.BlockSpec((None, D_MODEL, D_MODEL), lambda i: (layer, 0, 0)),
        pl.BlockSpec((tm, D_MODEL), lambda i: (i, 0)),
    ]
    args = [y, w_out, x]
    body = _outproj_kernel
    if final_w is not None:
        in_specs.append(pl.BlockSpec((1, D_MODEL), lambda i: (0, 0)))
        args.append(final_w)
        body = _outproj_final_kernel
    return pl.pallas_call(
        body,
        grid=(n // tm,),
        in_specs=in_specs,
        out_specs=pl.BlockSpec((tm, D_MODEL), lambda i: (i, 0)),
        out_shape=jax.ShapeDtypeStruct((n, D_MODEL), F32),
        compiler_params=pltpu.CompilerParams(
            dimension_semantics=("parallel",), vmem_limit_bytes=VMEM_LIMIT),
        name="outproj",
    )(*args)


def _conv_silu_tile(ext_ref, hist, c0, width, w_ref, wc0, b_ref=None):
    e = ext_ref[:, c0:c0 + width]
    acc = e * w_ref[0:1, wc0:wc0 + width]
    for j in range(1, CONV_K):
        acc = pltpu.roll(acc, 1, 0) + e * w_ref[j:j + 1, wc0:wc0 + width]
    acc = acc[hist:]
    if b_ref is not None:
        acc = acc + b_ref[0:1, wc0:wc0 + width]
    return _silu(acc)


def _l2norm(x):
    return x * lax.rsqrt(jnp.sum(x * x, axis=-1, keepdims=True) + EPS)


def _seg_cumsum(x, seg_len, row_in_seg):
    s = 1
    while s < seg_len:
        x = x + jnp.where(row_in_seg >= s, pltpu.roll(x, s, 0), 0.0)
        s *= 2
    return x


def _gates(small, gp_ref):
    sp = _softplus(small + gp_ref[1:2, :])
    gd = -jnp.exp(gp_ref[0:1, :]) * sp
    return _sigmoid(small), sp, gd


def _masked_decay(col, row, causal):
    return jnp.where(causal, jnp.exp(jnp.where(causal, col - row, 0.0)), 0.0)


def _gdn_finish(o, z, gnorm_ref):
    on = o * lax.rsqrt(jnp.mean(o * o, axis=-1, keepdims=True) + EPS) * gnorm_ref[0:1, :]
    return on * _silu(z)


def _mixer_prompt_kernel(proj_ref, gp_ref, gcw_ref, gnorm_ref, scw_ref, scb_ref, sd_ref, snorm_ref,
                         pw_ref, pscale_ref,
                         y_ref, sg_ref, ss_ref,
                         eq_ref, ex_ref, ep_ref, *, pos0):
    tt = PROMPT_TT
    n_chunks = tt // GDN_CHUNK
    hist, phist = SUBLANE, 2 * SUBLANE
    i = pl.program_id(1)

    @pl.when(i == 0)
    def _():
        sg_ref[...] = jnp.zeros(sg_ref.shape, F32)
        ss_ref[...] = jnp.zeros(ss_ref.shape, F32)
        eq_ref[0:hist, :] = jnp.zeros((hist, eq_ref.shape[1]), F32)
        ex_ref[0:hist, :] = jnp.zeros((hist, ex_ref.shape[1]), F32)
        ep_ref[0:phist, :] = jnp.zeros((phist, D_POOL), F32)

    eq_ref[hist:hist + tt, :] = proj_ref[:, C_QKV:C_QKV + 3 * D_GDN]
    ex_ref[hist:hist + tt, :] = proj_ref[:, C_XBC:C_XBC + SSD_CONV_CH]
    ep_ref[phist:phist + tt, :] = proj_ref[:, C_UP:C_UP + D_POOL]

    beta, sp, gd = _gates(proj_ref[:, C_SM:C_SM + LANE], gp_ref)
    row = lax.broadcasted_iota(jnp.int32, (tt, LANE), 0)
    gc = _seg_cumsum(gd, GDN_CHUNK, row & (GDN_CHUNK - 1))
    ac = _seg_cumsum(gd, SSD_CHUNK, row)

    def store_y(c0, val, r0=0):
        y_ref[r0:r0 + val.shape[0], c0:c0 + val.shape[1]] = val.astype(y_ref.dtype)

    ci = lax.broadcasted_iota(jnp.int32, (GDN_CHUNK, GDN_CHUNK), 0)
    cj = lax.broadcasted_iota(jnp.int32, (GDN_CHUNK, GDN_CHUNK), 1)
    causal_g, strict_g = ci >= cj, ci > cj
    gc_t = [gc[c * GDN_CHUNK:(c + 1) * GDN_CHUNK].T for c in range(n_chunks)]
    conv_q = functools.partial(_conv_silu_tile, eq_ref, hist)

    keys = [(c, h) for c in range(n_chunks) for h in range(GDN_HEADS)]
    nmat, pmat, rhs, qkb, qdec, kdec_t, glast = {}, {}, {}, {}, {}, {}, {}
    for h in range(GDN_HEADS):
        c0 = h * GDN_HEAD_DIM
        q_all = _l2norm(conv_q(c0, GDN_HEAD_DIM, gcw_ref, c0)) * (GDN_HEAD_DIM ** -0.5)
        k_all = _l2norm(conv_q(D_GDN + c0, GDN_HEAD_DIM, gcw_ref, D_GDN + c0))
        v_all = conv_q(2 * D_GDN + c0, GDN_HEAD_DIM, gcw_ref, 2 * D_GDN + c0)
        for c in range(n_chunks):
            rs = slice(c * GDN_CHUNK, (c + 1) * GDN_CHUNK)
            qh, kh, vh = q_all[rs], k_all[rs], v_all[rs]
            beta_c = beta[rs, L_BETA + h:L_BETA + h + 1]
            gc_c = gc[rs, L_GDN_G + h:L_GDN_G + h + 1]
            gc_r = gc_t[c][L_GDN_G + h:L_GDN_G + h + 1, :]
            gl = gc_c[GDN_CHUNK - 1:GDN_CHUNK, :]
            eg = jnp.exp(gc_c)
            decay = _masked_decay(gc_c, gc_r, causal_g)
            kb, qb = _bf(kh), _bf(qh)
            nmat[c, h] = jnp.where(strict_g, -(_dot_nt(kb, kb) * decay * beta_c), 0.0)
            qkb[c, h] = _bf(jnp.where(causal_g, _dot_nt(qb, kb) * decay, 0.0))
            rhs[c, h] = jnp.concatenate([vh * beta_c, kh * (beta_c * eg)], axis=-1)
            qdec[c, h] = qh * eg
            kdec_t[c, h] = _bf((kh * jnp.exp(gl - gc_c)).T)
            glast[c, h] = jnp.exp(gl)

    for key in keys:
        bb = _bf(nmat[key])
        pmat[key] = _dot(bb, bb)
    for _ in range(4):
        for key in keys:
            pb = _bf(pmat[key])
            r = _dot(jnp.concatenate([_bf(nmat[key]), pb], axis=0), pb)
            nmat[key] = nmat[key] + pmat[key] + r[:GDN_CHUNK]
            pmat[key] = r[GDN_CHUNK:]
    sol = {}
    for key in keys:
        nmat[key] = nmat[key] + pmat[key] + _dot(_bf(nmat[key]), _bf(pmat[key]))
    for key in keys:
        sol[key] = rhs[key] + _dot(_bf(nmat[key]), _bf(rhs[key]))

    for c in range(n_chunks):
        r0 = c * GDN_CHUNK
        for h in range(GDN_HEADS):
            key = (c, h)
            state = sg_ref[0, h]
            u, w = sol[key][:, :GDN_HEAD_DIM], sol[key][:, GDN_HEAD_DIM:]
            r1 = _dot(_bf(jnp.concatenate([w, qdec[key]], axis=0)), _bf(state))
            vnb = _bf(u - r1[:GDN_CHUNK])
            r2 = _dot(jnp.concatenate([qkb[key], kdec_t[key]], axis=0), vnb)
            sg_ref[0, h] = state * glast[key] + r2[GDN_CHUNK:]
            o = r1[GDN_CHUNK:] + r2[:GDN_CHUNK]
            c0 = h * GDN_HEAD_DIM
            z = proj_ref[r0:r0 + GDN_CHUNK, C_ZG + c0:C_ZG + c0 + GDN_HEAD_DIM]
            store_y(c0, _gdn_finish(o, z, gnorm_ref), r0)

    si = lax.broadcasted_iota(jnp.int32, (SSD_CHUNK, SSD_CHUNK), 0)
    sj = lax.broadcasted_iota(jnp.int32, (SSD_CHUNK, SSD_CHUNK), 1)
    causal_s = si >= sj
    lo_lanes = sj < SSD_HEAD_DIM
    lo_rows = si < SSD_HEAD_DIM
    ac_t = ac.T
    conv_x = functools.partial(_conv_silu_tile, ex_ref, hist)
    for g in range(SSD_GROUPS):
        bgb = _bf(conv_x(D_SSD + g * SSD_STATE, SSD_STATE, scw_ref, D_SSD + g * SSD_STATE, scb_ref))
        cgb = _bf(conv_x(D_SSD + (SSD_GROUPS + g) * SSD_STATE, SSD_STATE, scw_ref,
                         D_SSD + (SSD_GROUPS + g) * SSD_STATE, scb_ref))
        cb = _dot_nt(cgb, bgb)
        yz, ssq = [], None
        for pair in range(SSD_HEADS_PER_GROUP // 2):
            hd0 = g * SSD_HEADS_PER_GROUP + 2 * pair
            c0 = hd0 * SSD_HEAD_DIM
            la, lb = L_SSD_DT + hd0, L_SSD_DT + hd0 + 1
            xs2 = conv_x(c0, 2 * SSD_HEAD_DIM, scw_ref, c0, scb_ref)
            a2 = jnp.where(lo_lanes, ac[:, la:la + 1], ac[:, lb:lb + 1])
            xdt2 = xs2 * jnp.where(lo_lanes, sp[:, la:la + 1], sp[:, lb:lb + 1])
            alast2 = a2[tt - 1:tt, :]
            state2 = ss_ref[0, hd0:hd0 + 2].reshape(2 * SSD_HEAD_DIM, SSD_STATE)
            y = jnp.exp(a2) * _dot_nt(cgb, _bf(state2))
            seg_a = _masked_decay(ac[:, la:la + 1], ac_t[la:la + 1, :], causal_s)
            seg_b = _masked_decay(ac[:, lb:lb + 1], ac_t[lb:lb + 1, :], causal_s)
            y = y + _dot(_bf(cb * seg_a), _bf(jnp.where(lo_lanes, xdt2, 0.0)))
            y = y + _dot(_bf(cb * seg_b), _bf(jnp.where(lo_lanes, 0.0, xdt2)))
            upd = _dot_tn(_bf(xdt2 * jnp.exp(alast2 - a2)), bgb)
            elast = jnp.exp(jnp.where(lo_rows, ac[tt - 1:tt, la:la + 1], ac[tt - 1:tt, lb:lb + 1]))
            ss_ref[0, hd0:hd0 + 2] = (state2 * elast + upd).reshape(2, SSD_HEAD_DIM, SSD_STATE)
            z = proj_ref[:, C_ZS + c0:C_ZS + c0 + 2 * SSD_HEAD_DIM]
            t = (y + sd_ref[0:1, c0:c0 + 2 * SSD_HEAD_DIM] * xs2) * _silu(z)
            yz.append(t)
            s = jnp.sum(t * t, axis=-1, keepdims=True)
            ssq = s if ssq is None else ssq + s
        scale = lax.rsqrt(ssq * (1.0 / SSD_GROUP_DIM) + EPS)
        for pair, t in enumerate(yz):
            c0 = (g * SSD_HEADS_PER_GROUP + 2 * pair) * SSD_HEAD_DIM
            store_y(D_GDN + c0, t * scale * snorm_ref[0:1, c0:c0 + 2 * SSD_HEAD_DIM])

    pos = pos0 + i * tt + lax.broadcasted_iota(jnp.int32, (tt, 1), 0)
    for gi, win in enumerate(POOL_WINDOWS):
        sl = slice(gi * POOL_GROUP_DIM, (gi + 1) * POOL_GROUP_DIM)
        e = ep_ref[:, sl]
        acc, k = e, 1
        while k < win:
            acc = acc + pltpu.roll(acc, k, 0)
            k *= 2
        cnt = jnp.minimum(win, pos + 1).astype(F32)
        pooled = acc[phist:] / cnt - e[phist:]
        mixed = _dot(_bf(pooled), pw_ref[gi])
        z = proj_ref[:, C_ZP + gi * POOL_GROUP_DIM:C_ZP + (gi + 1) * POOL_GROUP_DIM]
        store_y(D_GDN + D_SSD + gi * POOL_GROUP_DIM, mixed * pscale_ref[0:1, sl] * _silu(z))

    eq_ref[0:hist, :] = eq_ref[tt:tt + hist, :]
    ex_ref[0:hist, :] = ex_ref[tt:tt + hist, :]
    ep_ref[0:phist, :] = ep_ref[tt:tt + phist, :]


def _layer_spec(shape):
    nd = len(shape)
    return lambda layer: pl.BlockSpec((None,) + tuple(shape[1:]), lambda *_: (layer,) + (0,) * (nd - 1))


def _param_specs(params, layer):
    return [_layer_spec(p.shape)(layer) for p in params]


def _mixer_prompt(proj, params, layer, batch, seq):
    tt = PROMPT_TT
    nt = seq // tt
    return pl.pallas_call(
        functools.partial(_mixer_prompt_kernel, pos0=0),
        grid=(batch, nt),
        in_specs=[pl.BlockSpec((tt, D_PACK), lambda b, i: (b * nt + i, 0))] + _param_specs(params, layer),
        out_specs=[
            pl.BlockSpec((tt, D_MODEL), lambda b, i: (b * nt + i, 0)),
            pl.BlockSpec((1, GDN_HEADS, GDN_HEAD_DIM, GDN_HEAD_DIM), lambda b, i: (b, 0, 0, 0)),
            pl.BlockSpec((1, SSD_HEADS, SSD_HEAD_DIM, SSD_STATE), lambda b, i: (b, 0, 0, 0)),
        ],
        out_shape=[
            jax.ShapeDtypeStruct((batch * seq, D_MODEL), BF16),
            jax.ShapeDtypeStruct((batch, GDN_HEADS, GDN_HEAD_DIM, GDN_HEAD_DIM), F32),
            jax.ShapeDtypeStruct((batch, SSD_HEADS, SSD_HEAD_DIM, SSD_STATE), F32),
        ],
        scratch_shapes=[
            pltpu.VMEM((SUBLANE + tt, 3 * D_GDN), F32),
            pltpu.VMEM((SUBLANE + tt, SSD_CONV_CH), F32),
            pltpu.VMEM((2 * SUBLANE + tt, D_POOL), F32),
        ],
        compiler_params=pltpu.CompilerParams(
            dimension_semantics=("parallel", "arbitrary"), vmem_limit_bytes=VMEM_LIMIT),
        name="mixer_prompt",
    )(proj, *params)


def _mixer_sample_kernel(*refs, n_alias, pos0):
    proj_ref, sg_in, gcv_in, ss_in, scv_in, pool_in = refs[:6]
    (gp_ref, gcw_ref, gnorm_ref, scw_ref, scb_ref, sd_ref, snorm_ref, pw_ref, pscale_ref,
     y_ref, sg_out, gcv_out, ss_out, scv_out, pool_out) = refs[6 + n_alias:]
    nb, t_len = SAMPLE_NB, SAMPLE_T
    assert nb == SUBLANE
    rows = nb * t_len
    last = slice(rows - nb, rows)

    def tokens(c0, width):
        return [proj_ref[t, :, c0:c0 + width] for t in range(t_len)]

    def stack(parts):
        return jnp.concatenate(parts, axis=0)

    def conv(hist_ref, col0, cw, width, w_ref, b_ref=None):
        u = [hist_ref[j, :, cw:cw + width] for j in range(CONV_K - 1)] + tokens(col0 + cw, width)
        outs = []
        for t in range(t_len):
            acc = u[t] * w_ref[0:1, cw:cw + width]
            for j in range(1, CONV_K):
                acc = acc + u[t + j] * w_ref[j:j + 1, cw:cw + width]
            if b_ref is not None:
                acc = acc + b_ref[0:1, cw:cw + width]
            outs.append(_silu(acc))
        return stack(outs)

    beta, sp, gd = _gates(stack(tokens(C_SM, LANE)), gp_ref)
    parts = [gd[0:nb]]
    for t in range(1, t_len):
        parts.append(parts[-1] + gd[t * nb:(t + 1) * nb])
    cum = stack(parts)
    cum_t = cum.T

    ri = lax.broadcasted_iota(jnp.int32, (rows, rows), 0)
    rj = lax.broadcasted_iota(jnp.int32, (rows, rows), 1)
    same = (ri & (nb - 1)) == (rj & (nb - 1))
    causal, strict = same & (ri >= rj), same & (ri > rj)
    seq1 = lax.broadcasted_iota(jnp.int32, (rows, 1), 0) & (nb - 1)
    seq2 = lax.broadcasted_iota(jnp.int32, (2 * rows, 1), 0) & (nb - 1)

    def block_diag(x):
        return jnp.concatenate([jnp.where(seq1 == b, x, 0.0) for b in range(nb)], axis=1)

    def store_y(c0, val):
        for t in range(t_len):
            y_ref[t, :, c0:c0 + val.shape[1]] = val[t * nb:(t + 1) * nb]

    heads = range(GDN_HEADS)
    bmat, bb, rhs, qkb, qdec, kd_t, egl = {}, {}, {}, {}, {}, {}, {}
    for h in heads:
        c0 = h * GDN_HEAD_DIM
        qh = _l2norm(conv(gcv_in, C_QKV, c0, GDN_HEAD_DIM, gcw_ref)) * (GDN_HEAD_DIM ** -0.5)
        kh = _l2norm(conv(gcv_in, C_QKV, D_GDN + c0, GDN_HEAD_DIM, gcw_ref))
        vh = conv(gcv_in, C_QKV, 2 * D_GDN + c0, GDN_HEAD_DIM, gcw_ref)
        beta_c = beta[:, L_BETA + h:L_BETA + h + 1]
        gc_c = cum[:, L_GDN_G + h:L_GDN_G + h + 1]
        eg = jnp.exp(gc_c)
        decay = _masked_decay(gc_c, cum_t[L_GDN_G + h:L_GDN_G + h + 1, :], causal)
        kb, qb = _bf(kh), _bf(qh)
        bmat[h] = jnp.where(strict, -(_dot_nt(kb, kb) * decay * beta_c), 0.0)
        bb[h] = _bf(bmat[h])
        qkb[h] = _bf(jnp.where(causal, _dot_nt(qb, kb) * decay, 0.0))
        rhs[h] = jnp.concatenate([vh * beta_c, kh * (beta_c * eg)], axis=-1)
        qdec[h] = qh * eg
        gl = gc_c[last]
        kd_t[h] = _bf((kh * jnp.exp(stack([gl] * t_len) - gc_c)).T)
        egl[h] = jnp.exp(gl)
    p1 = {h: _dot(bb[h], bb[h]) for h in heads}
    nmat = {h: bmat[h] + p1[h] + _dot(bb[h], _bf(p1[h])) for h in heads}
    sol = {h: rhs[h] + _dot(_bf(nmat[h]), _bf(rhs[h])) for h in heads}
    acc = {}
    for h in heads:
        lhs = _bf(stack([sol[h][:, GDN_HEAD_DIM:], qdec[h]]))
        a = jnp.zeros((2 * rows, GDN_HEAD_DIM), F32)
        for b in range(nb):
            a = jnp.where(seq2 == b, _dot(lhs, _bf(sg_in[b, h])), a)
        acc[h] = a
    vn = {h: sol[h][:, :GDN_HEAD_DIM] - acc[h][:rows] for h in heads}
    o = {h: acc[h][rows:] + _dot(qkb[h], _bf(vn[h])) for h in heads}
    upd = {h: _dot(kd_t[h], _bf(block_diag(vn[h]))) for h in heads}
    for h in heads:
        c0 = h * GDN_HEAD_DIM
        store_y(c0, _gdn_finish(o[h], stack(tokens(C_ZG + c0, GDN_HEAD_DIM)), gnorm_ref))
        for b in range(nb):
            sg_out[b, h] = (sg_in[b, h] * egl[h][b:b + 1, :]
                            + upd[h][:, b * GDN_HEAD_DIM:(b + 1) * GDN_HEAD_DIM])

    lo_lanes = lax.broadcasted_iota(jnp.int32, (rows, 2 * SSD_HEAD_DIM), 1) < SSD_HEAD_DIM
    lo_rows = lax.broadcasted_iota(jnp.int32, (2 * SSD_HEAD_DIM, SSD_STATE), 0) < SSD_HEAD_DIM
    pairs = range(SSD_HEADS // 2)
    n_gp = SSD_HEADS_PER_GROUP // 2
    cgb, cb, bg_bd = {}, {}, {}
    for g in range(SSD_GROUPS):
        bg = conv(scv_in, C_XBC, D_SSD + g * SSD_STATE, SSD_STATE, scw_ref, scb_ref)
        cg = conv(scv_in, C_XBC, D_SSD + (SSD_GROUPS + g) * SSD_STATE, SSD_STATE, scw_ref, scb_ref)
        cgb[g] = _bf(cg)
        cb[g] = _dot_nt(cgb[g], _bf(bg))
        bg_bd[g] = _bf(block_diag(bg))
    xs2, a2, y, xin_t = {}, {}, {}, {}
    for p in pairs:
        g, c0 = p // n_gp, 2 * p * SSD_HEAD_DIM
        la, lb = L_SSD_DT + 2 * p, L_SSD_DT + 2 * p + 1
        xs2[p] = conv(scv_in, C_XBC, c0, 2 * SSD_HEAD_DIM, scw_ref, scb_ref)
        a2[p] = jnp.where(lo_lanes, cum[:, la:la + 1], cum[:, lb:lb + 1])
        xdt2 = xs2[p] * jnp.where(lo_lanes, sp[:, la:la + 1], sp[:, lb:lb + 1])
        seg_a = _masked_decay(cum[:, la:la + 1], cum_t[la:la + 1, :], causal)
        seg_b = _masked_decay(cum[:, lb:lb + 1], cum_t[lb:lb + 1, :], causal)
        y[p] = (_dot(_bf(cb[g] * seg_a), _bf(jnp.where(lo_lanes, xdt2, 0.0)))
                + _dot(_bf(cb[g] * seg_b), _bf(jnp.where(lo_lanes, 0.0, xdt2))))
        xin_t[p] = _bf((xdt2 * jnp.exp(stack([a2[p][last]] * t_len) - a2[p])).T)
    acc = {}
    for p in pairs:
        a = jnp.zeros((rows, 2 * SSD_HEAD_DIM), F32)
        for b in range(nb):
            state2 = ss_in[b, 2 * p:2 * p + 2].reshape(2 * SSD_HEAD_DIM, SSD_STATE)
            a = jnp.where(seq1 == b, _dot_nt(cgb[p // n_gp], _bf(state2)), a)
        acc[p] = a
    upd = {p: _dot(xin_t[p], bg_bd[p // n_gp]) for p in pairs}
    for p in pairs:
        la, lb = L_SSD_DT + 2 * p, L_SSD_DT + 2 * p + 1
        for b in range(nb):
            r = rows - nb + b
            elast = jnp.exp(jnp.where(lo_rows, cum[r:r + 1, la:la + 1], cum[r:r + 1, lb:lb + 1]))
            state2 = ss_in[b, 2 * p:2 * p + 2].reshape(2 * SSD_HEAD_DIM, SSD_STATE)
            ss_out[b, 2 * p:2 * p + 2] = (state2 * elast + upd[p][:, b * SSD_STATE:(b + 1) * SSD_STATE]).reshape(
                2, SSD_HEAD_DIM, SSD_STATE)
    for g in range(SSD_GROUPS):
        yz, ssq = {}, None
        for p in range(g * n_gp, (g + 1) * n_gp):
            c0 = 2 * p * SSD_HEAD_DIM
            z = stack(tokens(C_ZS + c0, 2 * SSD_HEAD_DIM))
            yt = y[p] + jnp.exp(a2[p]) * acc[p]
            yz[p] = (yt + sd_ref[0:1, c0:c0 + 2 * SSD_HEAD_DIM] * xs2[p]) * _silu(z)
            s = jnp.sum(yz[p] * yz[p], axis=-1, keepdims=True)
            ssq = s if ssq is None else ssq + s
        scale = lax.rsqrt(ssq * (1.0 / SSD_GROUP_DIM) + EPS)
        for p, t in yz.items():
            c0 = 2 * p * SSD_HEAD_DIM
            store_y(D_GDN + c0, t * scale * snorm_ref[0:1, c0:c0 + 2 * SSD_HEAD_DIM])

    for gi, win in enumerate(POOL_WINDOWS):
        sl = slice(gi * POOL_GROUP_DIM, (gi + 1) * POOL_GROUP_DIM)
        x = [pool_in[j, :, sl] for j in range(POOL_BUF)] + tokens(C_UP + gi * POOL_GROUP_DIM, POOL_GROUP_DIM)
        pooled = []
        for t in range(t_len):
            acc = x[POOL_BUF + t]
            for k in range(1, win):
                acc = acc + x[POOL_BUF + t - k]
            pooled.append(acc / float(min(win, pos0 + t + 1)) - x[POOL_BUF + t])
        mixed = _dot(_bf(stack(pooled)), pw_ref[gi])
        z = stack(tokens(C_ZP + gi * POOL_GROUP_DIM, POOL_GROUP_DIM))
        store_y(D_GDN + D_SSD + gi * POOL_GROUP_DIM, mixed * pscale_ref[0:1, sl] * _silu(z))

    for j in range(CONV_K - 1):
        t = t_len - (CONV_K - 1) + j
        gcv_out[j] = proj_ref[t, :, C_QKV:C_QKV + 3 * D_GDN]
        scv_out[j] = proj_ref[t, :, C_XBC:C_XBC + SSD_CONV_CH]
    for j in range(POOL_BUF):
        src = j + t_len
        pool_out[j] = pool_in[src] if src < POOL_BUF else proj_ref[src - POOL_BUF, :, C_UP:C_UP + D_POOL]


def _mixer_sample(proj, states, params, layer, n_seq, prev_outs):
    nb, t_len = SAMPLE_NB, SAMPLE_T
    assert CONV_K - 1 <= t_len and proj.shape == (t_len, n_seq, D_PACK)
    prev_outs = () if prev_outs is None else tuple(prev_outs)
    n_state = len(states)

    def seq_spec(s):
        if s.shape[1] == n_seq:
            tail = tuple(s.shape[2:])
            return pl.BlockSpec((None, nb) + tail, lambda i: (layer, i) + (0,) * len(tail))
        assert s.shape[2] == n_seq and s.ndim == 4
        return pl.BlockSpec((None, s.shape[1], nb, s.shape[3]), lambda i: (layer, 0, i, 0))

    return pl.pallas_call(
        functools.partial(_mixer_sample_kernel, n_alias=len(prev_outs), pos0=PAST_LEN),
        grid=(n_seq // nb,),
        in_specs=[pl.BlockSpec((t_len, nb, D_PACK), lambda i: (0, i, 0))]
        + [seq_spec(s) for s in states]
        + [pl.BlockSpec(memory_space=pl.ANY)] * len(prev_outs)
        + _param_specs(params, layer),
        out_specs=[pl.BlockSpec((t_len, nb, D_MODEL), lambda i: (0, i, 0))] + [seq_spec(s) for s in states],
        out_shape=[jax.ShapeDtypeStruct((t_len, n_seq, D_MODEL), F32)]
        + [jax.ShapeDtypeStruct(s.shape, F32) for s in states],
        input_output_aliases={1 + n_state + k: 1 + k for k in range(len(prev_outs))},
        compiler_params=pltpu.CompilerParams(
            dimension_semantics=("parallel",), vmem_limit_bytes=VMEM_LIMIT),
        name="mixer_sample",
    )(proj, *states, *prev_outs, *params)


def _gate_params(gdn_a_log, gdn_dt_bias, ssd_a_log, ssd_dt_bias):
    def lanes(gdn, ssd):
        z = jnp.zeros((DEPTH, LANE), F32)
        z = z.at[:, L_GDN_G:L_GDN_G + GDN_HEADS].set(gdn)
        return z.at[:, L_SSD_DT:L_SSD_DT + SSD_HEADS].set(ssd)

    rows = jnp.stack([lanes(gdn_a_log, ssd_a_log), lanes(gdn_dt_bias, ssd_dt_bias)], axis=1)
    return jnp.concatenate([rows, jnp.zeros((DEPTH, SUBLANE - 2, LANE), F32)], axis=1)


def kernel(x_prompt, x_sample, state_gdn, state_gdn_conv, state_ssd, state_ssd_conv, state_pool, norm_w, w_in, gdn_conv_w, gdn_a_log, gdn_dt_bias, gdn_norm_w, ssd_conv_w, ssd_conv_b, ssd_a_log, ssd_dt_bias, ssd_d, ssd_norm_w, pool_w, pool_scale, w_out, final_norm_w):
    batch, seq, _ = x_prompt.shape
    n_seq, t_len, _ = x_sample.shape
    assert t_len == SAMPLE_T and seq % PROMPT_TT == 0 and n_seq % SAMPLE_NB == 0

    w_in_p = _pack_w_in(jnp.swapaxes(w_in, 1, 2))
    w_out_b = w_out.astype(BF16)
    params = (
        _gate_params(gdn_a_log, gdn_dt_bias, ssd_a_log, ssd_dt_bias),
        gdn_conv_w,
        gdn_norm_w[:, None, :],
        ssd_conv_w,
        ssd_conv_b[:, None, :],
        jnp.repeat(ssd_d, SSD_HEAD_DIM, axis=-1)[:, None, :],
        ssd_norm_w[:, None, :],
        pool_w.astype(BF16),
        pool_scale[:, None, :],
    )
    nw = norm_w[:, None, :]
    fw = final_norm_w[None, :]
    states = (state_gdn, jnp.swapaxes(state_gdn_conv, 1, 2), state_ssd, jnp.swapaxes(state_ssd_conv, 1, 2),
              jnp.swapaxes(state_pool, 1, 2))

    xp = x_prompt.reshape(batch * seq, D_MODEL)
    xs = jnp.swapaxes(x_sample, 0, 1).reshape(t_len * n_seq, D_MODEL)
    new_p = [[] for _ in range(5)]
    new_s = None
    for layer in range(DEPTH):
        final = fw if layer == DEPTH - 1 else None

        proj_p = _inproj(xp, nw, w_in_p, layer, PROJ_TM)
        y_p, gdn_p, ssd_p = _mixer_prompt(proj_p, params, layer, batch, seq)
        xp = _outproj(y_p, w_out_b, xp, layer, 512, final)
        pp = proj_p.reshape(batch, seq, D_PACK)
        new_p[0].append(gdn_p)
        new_p[1].append(pp[:, seq - (CONV_K - 1):, C_QKV:C_QKV + 3 * D_GDN])
        new_p[2].append(ssd_p)
        new_p[3].append(pp[:, seq - (CONV_K - 1):, C_XBC:C_XBC + SSD_CONV_CH])
        new_p[4].append(pp[:, seq - POOL_BUF:, C_UP:C_UP + D_POOL])

        proj_s = _inproj(xs, nw, w_in_p, layer, n_seq * t_len).reshape(t_len, n_seq, D_PACK)
        y_s, *new_s = _mixer_sample(proj_s, states, params, layer, n_seq, new_s)
        xs = _outproj(y_s.reshape(t_len * n_seq, D_MODEL), w_out_b, xs, layer, n_seq * t_len, final)

    new_s = [a if a.shape[1] == n_seq else jnp.swapaxes(a, 1, 2) for a in new_s]
    y_prompt = xp.reshape(batch, seq, D_MODEL)
    y_sample = jnp.swapaxes(xs.reshape(t_len, n_seq, D_MODEL), 0, 1)
    return (y_prompt, y_sample) + tuple(jnp.stack(a) for a in new_p) + tuple(new_s)
```

```python
import functools

import jax
import jax.numpy as jnp
import numpy as np
from jax import lax
from jax.experimental import pallas as pl
from jax.experimental.pallas import tpu as pltpu

F32 = jnp.float32
BF16 = jnp.bfloat16

D_MODEL = 2048
DEPTH = 4
PAST_LEN = 16384
D_GDN = 768
D_SSD = 768
D_POOL = 512
CONV_K = 4
GDN_HEAD_DIM = 128
GDN_HEADS = 6
GDN_CHUNK = 64
SSD_HEAD_DIM = 64
SSD_HEADS = 12
SSD_STATE = 128
SSD_GROUPS = 2
SSD_HEADS_PER_GROUP = 6
SSD_GROUP_DIM = D_SSD // SSD_GROUPS
SSD_CHUNK = 128
SSD_CONV_CH = 1280
POOL_WINDOWS = (2, 4, 8, 16)
POOL_GROUP_DIM = 128
POOL_BUF = 15
IN_SIZES = (3 * D_GDN, D_GDN, GDN_HEADS, GDN_HEADS, D_SSD, SSD_CONV_CH, SSD_HEADS, D_POOL, D_POOL)
EPS = 1e-6

C_QKV = 0
C_ZG = 2304
C_ZS = 3072
C_XBC = 3840
C_UP = 5120
C_ZP = 5632
C_SM = 6144
D_PACK = 6400
L_BETA = 0
L_GDN_G = 6
L_SSD_DT = 16

LANE = 128
SUBLANE = 8
MXU_WIDTH = 256
VMEM_LIMIT = 48 * 1024 * 1024
VMEM_LIMIT_PROJ = 56 * 1024 * 1024

PROJ_TN = 5 * MXU_WIDTH
PACK_TK = 256
PROMPT_TT = 128
SAMPLE_NB = 8
SAMPLE_T = 4


def _dot(a, b):
    return jnp.dot(a, b, preferred_element_type=F32)


def _dot_nt(a, b):
    return lax.dot_general(a, b, (((1,), (1,)), ((), ())), preferred_element_type=F32)


def _dot_tn(a, b):
    return lax.dot_general(a, b, (((0,), (0,)), ((), ())), preferred_element_type=F32)


def _sigmoid(x):
    return 0.5 + 0.5 * jnp.tanh(0.5 * x)


def _silu(x):
    hx = 0.5 * x
    return hx + hx * jnp.tanh(hx)


def _softplus(x):
    return jnp.maximum(x, 0.0) + jnp.log(1.0 + jnp.exp(-jnp.abs(x)))


def _bf(x):
    return x.astype(BF16)


def _inproj_kernel(x_ref, nw_ref, w_ref, o_ref, h_ref):
    @pl.when(pl.program_id(1) == 0)
    def _():
        x = x_ref[...]
        ms = jnp.mean(x * x, axis=-1, keepdims=True)
        h_ref[...] = _bf(x * lax.rsqrt(ms + EPS) * nw_ref[...])

    o_ref[...] = _dot(h_ref[...], w_ref[...])


def _inproj(x, nw, w_packed, layer, tm):
    n = x.shape[0]
    return pl.pallas_call(
        _inproj_kernel,
        grid=(n // tm, D_PACK // PROJ_TN),
        in_specs=[
            pl.BlockSpec((tm, D_MODEL), lambda i, j: (i, 0)),
            pl.BlockSpec((None, 1, D_MODEL), lambda i, j: (layer, 0, 0)),
            pl.BlockSpec((None, D_MODEL, PROJ_TN), lambda i, j: (layer, 0, j)),
        ],
        out_specs=pl.BlockSpec((tm, PROJ_TN), lambda i, j: (i, j)),
        out_shape=jax.ShapeDtypeStruct((n, D_PACK), F32),
        scratch_shapes=[pltpu.VMEM((tm, D_MODEL), BF16)],
        compiler_params=pltpu.CompilerParams(
            dimension_semantics=("parallel", "arbitrary"), vmem_limit_bytes=VMEM_LIMIT_PROJ),
        name="inproj",
    )(x, nw, w_packed)


def _pack_kernel(w_ref, o_ref):
    offs = np.cumsum((0,) + IN_SIZES)
    tk = o_ref.shape[0]

    for seg, c_out in ((0, C_QKV), (1, C_ZG), (4, C_ZS), (5, C_XBC), (7, C_UP), (8, C_ZP)):
        o_ref[:, c_out:c_out + IN_SIZES[seg]] = _bf(w_ref[offs[seg]:offs[seg + 1], :].T)
    assert offs[3] == offs[2] + GDN_HEADS and L_GDN_G == L_BETA + GDN_HEADS and L_BETA == 0
    assert offs[2] % SUBLANE == 0 and offs[2] + LANE <= offs[-1]
    lane = lax.broadcasted_iota(jnp.int32, (tk, LANE), 1)
    gates = w_ref[offs[2]:offs[2] + LANE, :].T
    dt_row0 = offs[6] - offs[6] % SUBLANE
    dt_lane0 = offs[6] - dt_row0
    assert dt_row0 + LANE <= offs[-1] and dt_lane0 <= L_SSD_DT
    dts = pltpu.roll(w_ref[dt_row0:dt_row0 + LANE, :].T, L_SSD_DT - dt_lane0, 1)
    small = jnp.where(lane < 2 * GDN_HEADS, gates,
                      jnp.where((lane >= L_SSD_DT) & (lane < L_SSD_DT + SSD_HEADS), dts, 0.0))
    o_ref[:, C_SM:C_SM + LANE] = _bf(small)
    o_ref[:, C_SM + LANE:D_PACK] = jnp.zeros((tk, D_PACK - C_SM - LANE), BF16)


def _pack_w_in(w_in_t):
    d_in = w_in_t.shape[1]
    return pl.pallas_call(
        _pack_kernel,
        grid=(DEPTH, D_MODEL // PACK_TK),
        in_specs=[pl.BlockSpec((None, d_in, PACK_TK), lambda l, i: (l, 0, i))],
        out_specs=pl.BlockSpec((None, PACK_TK, D_PACK), lambda l, i: (l, i, 0)),
        out_shape=jax.ShapeDtypeStruct((DEPTH, D_MODEL, D_PACK), BF16),
        compiler_params=pltpu.CompilerParams(
            dimension_semantics=("parallel", "parallel"), vmem_limit_bytes=VMEM_LIMIT),
        name="pack_w_in",
    )(w_in_t)


def _outproj_kernel(*refs, n_y, final):
    y_refs, (w_ref, x_ref), rest = refs[:n_y], refs[n_y:n_y + 2], refs[n_y + 2:]
    o_ref = rest[-1]
    if n_y == 1:
        y = y_refs[0][...]
    else:
        tt = PROMPT_TT
        y = jnp.concatenate([r[k * tt:(k + 1) * tt, :] for k in range(y_refs[0].shape[0] // tt) for r in y_refs],
                            axis=0)
    x = x_ref[...] + _dot(_bf(y), w_ref[...])
    if final:
        ms = jnp.mean(x * x, axis=-1, keepdims=True)
        x = x * lax.rsqrt(ms + EPS) * rest[0][...]
    o_ref[...] = x


def _outproj(ys, w_out, x, layer, tm, final_w=None):
    n = x.shape[0]
    ys = tuple(ys) if isinstance(ys, (tuple, list)) else (ys,)
    in_specs = [pl.BlockSpec((tm // len(ys), D_MODEL), lambda i: (i, 0)) for _ in ys] + [
        pl.BlockSpec((None, D_MODEL, D_MODEL), lambda i: (layer, 0, 0)),
        pl.BlockSpec((tm, D_MODEL), lambda i: (i, 0)),
    ]
    args = [*ys, w_out, x]
    if final_w is not None:
        in_specs.append(pl.BlockSpec((1, D_MODEL), lambda i: (0, 0)))
        args.append(final_w)
    return pl.pallas_call(
        functools.partial(_outproj_kernel, n_y=len(ys), final=final_w is not None),
        grid=(n // tm,),
        in_specs=in_specs,
        out_specs=pl.BlockSpec((tm, D_MODEL), lambda i: (i, 0)),
        out_shape=jax.ShapeDtypeStruct((n, D_MODEL), F32),
        compiler_params=pltpu.CompilerParams(
            dimension_semantics=("parallel",), vmem_limit_bytes=VMEM_LIMIT),
        name="outproj",
    )(*args)


def _conv_silu_tile(ext_ref, hist, c0, width, w_ref, wc0, b_ref=None):
    e = ext_ref[:, c0:c0 + width]
    acc = e * w_ref[0:1, wc0:wc0 + width]
    for j in range(1, CONV_K):
        acc = pltpu.roll(acc, 1, 0) + e * w_ref[j:j + 1, wc0:wc0 + width]
    acc = acc[hist:]
    if b_ref is not None:
        acc = acc + b_ref[0:1, wc0:wc0 + width]
    return _silu(acc)


def _l2norm(x):
    return x * lax.rsqrt(jnp.sum(x * x, axis=-1, keepdims=True) + EPS)


def _seg_cumsum(x, seg_len, row_in_seg):
    s = 1
    while s < seg_len:
        x = x + jnp.where(row_in_seg >= s, pltpu.roll(x, s, 0), 0.0)
        s *= 2
    return x


def _gates(small, gp_ref):
    sp = _softplus(small + gp_ref[1:2, :])
    gd = -jnp.exp(gp_ref[0:1, :]) * sp
    return _sigmoid(small), sp, gd


def _masked_decay(col, row, causal):
    return jnp.where(causal, jnp.exp(jnp.where(causal, col - row, 0.0)), 0.0)


def _gdn_finish(o, z, gnorm_ref):
    on = o * lax.rsqrt(jnp.mean(o * o, axis=-1, keepdims=True) + EPS) * gnorm_ref[0:1, :]
    return on * _silu(z)


HIST = SUBLANE
PHIST = 2 * SUBLANE


def _reset_prompt_state(sg_ref, ss_ref, eq_ref, ex_ref, ep_ref):
    sg_ref[...] = jnp.zeros(sg_ref.shape, F32)
    ss_ref[...] = jnp.zeros(ss_ref.shape, F32)
    eq_ref[0:HIST, :] = jnp.zeros((HIST, eq_ref.shape[1]), F32)
    ex_ref[0:HIST, :] = jnp.zeros((HIST, ex_ref.shape[1]), F32)
    ep_ref[0:PHIST, :] = jnp.zeros((PHIST, D_POOL), F32)


def _mixer_tile(proj_ref, y_ref, i, pos0, params, sg_ref, ss_ref, eq_ref, ex_ref, ep_ref):
    gp_ref, gcw_ref, gnorm_ref, scw_ref, scb_ref, sd_ref, snorm_ref, pw_ref, pscale_ref = params
    tt = PROMPT_TT
    n_chunks = tt // GDN_CHUNK
    hist, phist = HIST, PHIST

    eq_ref[hist:hist + tt, :] = proj_ref[:, C_QKV:C_QKV + 3 * D_GDN]
    ex_ref[hist:hist + tt, :] = proj_ref[:, C_XBC:C_XBC + SSD_CONV_CH]
    ep_ref[phist:phist + tt, :] = proj_ref[:, C_UP:C_UP + D_POOL]

    beta, sp, gd = _gates(proj_ref[:, C_SM:C_SM + LANE], gp_ref)
    row = lax.broadcasted_iota(jnp.int32, (tt, LANE), 0)
    gc = _seg_cumsum(gd, GDN_CHUNK, row & (GDN_CHUNK - 1))
    ac = _seg_cumsum(gd, SSD_CHUNK, row)

    def store_y(c0, val, r0=0):
        y_ref[r0:r0 + val.shape[0], c0:c0 + val.shape[1]] = val.astype(y_ref.dtype)

    ci = lax.broadcasted_iota(jnp.int32, (GDN_CHUNK, GDN_CHUNK), 0)
    cj = lax.broadcasted_iota(jnp.int32, (GDN_CHUNK, GDN_CHUNK), 1)
    causal_g, strict_g = ci >= cj, ci > cj
    gc_t = [gc[c * GDN_CHUNK:(c + 1) * GDN_CHUNK].T for c in range(n_chunks)]
    conv_q = functools.partial(_conv_silu_tile, eq_ref, hist)

    keys = [(c, h) for c in range(n_chunks) for h in range(GDN_HEADS)]
    nmat, pmat, rhs, qkb, qdec, kdec_t, glast = {}, {}, {}, {}, {}, {}, {}
    for h in range(GDN_HEADS):
        c0 = h * GDN_HEAD_DIM
        q_all = _l2norm(conv_q(c0, GDN_HEAD_DIM, gcw_ref, c0)) * (GDN_HEAD_DIM ** -0.5)
        k_all = _l2norm(conv_q(D_GDN + c0, GDN_HEAD_DIM, gcw_ref, D_GDN + c0))
        v_all = conv_q(2 * D_GDN + c0, GDN_HEAD_DIM, gcw_ref, 2 * D_GDN + c0)
        for c in range(n_chunks):
            rs = slice(c * GDN_CHUNK, (c + 1) * GDN_CHUNK)
            qh, kh, vh = q_all[rs], k_all[rs], v_all[rs]
            beta_c = beta[rs, L_BETA + h:L_BETA + h + 1]
            gc_c = gc[rs, L_GDN_G + h:L_GDN_G + h + 1]
            gc_r = gc_t[c][L_GDN_G + h:L_GDN_G + h + 1, :]
            gl = gc_c[GDN_CHUNK - 1:GDN_CHUNK, :]
            eg = jnp.exp(gc_c)
            decay = _masked_decay(gc_c, gc_r, causal_g)
            kb, qb = _bf(kh), _bf(qh)
            nmat[c, h] = jnp.where(strict_g, -(_dot_nt(kb, kb) * decay * beta_c), 0.0)
            qkb[c, h] = _bf(jnp.where(causal_g, _dot_nt(qb, kb) * decay, 0.0))
            rhs[c, h] = jnp.concatenate([vh * beta_c, kh * (beta_c * eg)], axis=-1)
            qdec[c, h] = qh * eg
            kdec_t[c, h] = _bf((kh * jnp.exp(gl - gc_c)).T)
            glast[c, h] = jnp.exp(gl)

    for key in keys:
        bb = _bf(nmat[key])
        pmat[key] = _dot(bb, bb)
    for _ in range(4):
        for key in keys:
            pb = _bf(pmat[key])
            r = _dot(jnp.concatenate([_bf(nmat[key]), pb], axis=0), pb)
            nmat[key] = nmat[key] + pmat[key] + r[:GDN_CHUNK]
            pmat[key] = r[GDN_CHUNK:]
    sol = {}
    for key in keys:
        nmat[key] = nmat[key] + pmat[key] + _dot(_bf(nmat[key]), _bf(pmat[key]))
    for key in keys:
        sol[key] = rhs[key] + _dot(_bf(nmat[key]), _bf(rhs[key]))

    for c in range(n_chunks):
        r0 = c * GDN_CHUNK
        for h in range(GDN_HEADS):
            key = (c, h)
            state = sg_ref[h]
            u, w = sol[key][:, :GDN_HEAD_DIM], sol[key][:, GDN_HEAD_DIM:]
            r1 = _dot(_bf(jnp.concatenate([w, qdec[key]], axis=0)), _bf(state))
            vnb = _bf(u - r1[:GDN_CHUNK])
            r2 = _dot(jnp.concatenate([qkb[key], kdec_t[key]], axis=0), vnb)
            sg_ref[h] = state * glast[key] + r2[GDN_CHUNK:]
            o = r1[GDN_CHUNK:] + r2[:GDN_CHUNK]
            c0 = h * GDN_HEAD_DIM
            z = proj_ref[r0:r0 + GDN_CHUNK, C_ZG + c0:C_ZG + c0 + GDN_HEAD_DIM]
            store_y(c0, _gdn_finish(o, z, gnorm_ref), r0)

    si = lax.broadcasted_iota(jnp.int32, (SSD_CHUNK, SSD_CHUNK), 0)
    sj = lax.broadcasted_iota(jnp.int32, (SSD_CHUNK, SSD_CHUNK), 1)
    causal_s = si >= sj
    lo_lanes = sj < SSD_HEAD_DIM
    lo_rows = si < SSD_HEAD_DIM
    ac_t = ac.T
    conv_x = functools.partial(_conv_silu_tile, ex_ref, hist)
    for g in range(SSD_GROUPS):
        bgb = _bf(conv_x(D_SSD + g * SSD_STATE, SSD_STATE, scw_ref, D_SSD + g * SSD_STATE, scb_ref))
        cgb = _bf(conv_x(D_SSD + (SSD_GROUPS + g) * SSD_STATE, SSD_STATE, scw_ref,
                         D_SSD + (SSD_GROUPS + g) * SSD_STATE, scb_ref))
        cb = _dot_nt(cgb, bgb)
        yz, ssq = [], None
        for pair in range(SSD_HEADS_PER_GROUP // 2):
            hd0 = g * SSD_HEADS_PER_GROUP + 2 * pair
            c0 = hd0 * SSD_HEAD_DIM
            la, lb = L_SSD_DT + hd0, L_SSD_DT + hd0 + 1
            xs2 = conv_x(c0, 2 * SSD_HEAD_DIM, scw_ref, c0, scb_ref)
            a2 = jnp.where(lo_lanes, ac[:, la:la + 1], ac[:, lb:lb + 1])
            xdt2 = xs2 * jnp.where(lo_lanes, sp[:, la:la + 1], sp[:, lb:lb + 1])
            alast2 = a2[tt - 1:tt, :]
            state2 = ss_ref[hd0:hd0 + 2].reshape(2 * SSD_HEAD_DIM, SSD_STATE)
            y = jnp.exp(a2) * _dot_nt(cgb, _bf(state2))
            seg_a = _masked_decay(ac[:, la:la + 1], ac_t[la:la + 1, :], causal_s)
            seg_b = _masked_decay(ac[:, lb:lb + 1], ac_t[lb:lb + 1, :], causal_s)
            y = y + _dot(_bf(cb * seg_a), _bf(jnp.where(lo_lanes, xdt2, 0.0)))
            y = y + _dot(_bf(cb * seg_b), _bf(jnp.where(lo_lanes, 0.0, xdt2)))
            upd = _dot_tn(_bf(xdt2 * jnp.exp(alast2 - a2)), bgb)
            elast = jnp.exp(jnp.where(lo_rows, ac[tt - 1:tt, la:la + 1], ac[tt - 1:tt, lb:lb + 1]))
            ss_ref[hd0:hd0 + 2] = (state2 * elast + upd).reshape(2, SSD_HEAD_DIM, SSD_STATE)
            z = proj_ref[:, C_ZS + c0:C_ZS + c0 + 2 * SSD_HEAD_DIM]
            t = (y + sd_ref[0:1, c0:c0 + 2 * SSD_HEAD_DIM] * xs2) * _silu(z)
            yz.append(t)
            s = jnp.sum(t * t, axis=-1, keepdims=True)
            ssq = s if ssq is None else ssq + s
        scale = lax.rsqrt(ssq * (1.0 / SSD_GROUP_DIM) + EPS)
        for pair, t in enumerate(yz):
            c0 = (g * SSD_HEADS_PER_GROUP + 2 * pair) * SSD_HEAD_DIM
            store_y(D_GDN + c0, t * scale * snorm_ref[0:1, c0:c0 + 2 * SSD_HEAD_DIM])

    pos = pos0 + i * tt + lax.broadcasted_iota(jnp.int32, (tt, 1), 0)
    for gi, win in enumerate(POOL_WINDOWS):
        sl = slice(gi * POOL_GROUP_DIM, (gi + 1) * POOL_GROUP_DIM)
        e = ep_ref[:, sl]
        acc, k = e, 1
        while k < win:
            acc = acc + pltpu.roll(acc, k, 0)
            k *= 2
        cnt = jnp.minimum(win, pos + 1).astype(F32)
        pooled = acc[phist:] / cnt - e[phist:]
        mixed = _dot(_bf(pooled), pw_ref[gi])
        z = proj_ref[:, C_ZP + gi * POOL_GROUP_DIM:C_ZP + (gi + 1) * POOL_GROUP_DIM]
        store_y(D_GDN + D_SSD + gi * POOL_GROUP_DIM, mixed * pscale_ref[0:1, sl] * _silu(z))

    eq_ref[0:hist, :] = eq_ref[tt:tt + hist, :]
    ex_ref[0:hist, :] = ex_ref[tt:tt + hist, :]
    ep_ref[0:phist, :] = ep_ref[tt:tt + phist, :]


def _layer_spec(shape):
    nd = len(shape)
    return lambda layer: pl.BlockSpec((None,) + tuple(shape[1:]), lambda *_: (layer,) + (0,) * (nd - 1))


def _param_specs(params, layer):
    return [_layer_spec(p.shape)(layer) for p in params]


def _layer_prompt_kernel(x_ref, nw_ref, w_ref, *refs, tiles_per_seq, pos0):
    params = refs[:9]
    (y_odd_ref, y_even_ref, sg_out, ss_out, gcv_out, scv_out, pool_out,
     p_even, p_odd, sg_ref, ss_ref, eq_ref, ex_ref, ep_ref) = refs[9:]
    tt = PROMPT_TT
    g = pl.program_id(0)
    state = (sg_ref, ss_ref, eq_ref, ex_ref, ep_ref)

    @pl.when(g == 0)
    def _():
        p_odd[...] = jnp.zeros(p_odd.shape, F32)
        _reset_prompt_state(*state)

    def project(rows, p_ref):
        x = x_ref[rows, :]
        ms = jnp.mean(x * x, axis=-1, keepdims=True)
        p_ref[...] = _dot(_bf(x * lax.rsqrt(ms + EPS) * nw_ref[...]), w_ref[...])

    project(slice(0, tt), p_even)
    i_odd = lax.rem(2 * g - 1 + tiles_per_seq, tiles_per_seq)
    _mixer_tile(p_odd, y_odd_ref, i_odd, pos0, params, *state)
    sg_out[0] = sg_ref[...]
    ss_out[0] = ss_ref[...]
    gcv_out[0] = eq_ref[HIST + tt - (CONV_K - 1):HIST + tt, :]
    scv_out[0] = ex_ref[HIST + tt - (CONV_K - 1):HIST + tt, :]
    pool_out[0] = ep_ref[PHIST + tt - POOL_BUF:PHIST + tt, :]

    i_even = lax.rem(2 * g, tiles_per_seq)

    @pl.when(i_even == 0)
    def _():
        _reset_prompt_state(*state)

    project(slice(tt, 2 * tt), p_odd)
    _mixer_tile(p_even, y_even_ref, i_even, pos0, params, *state)


def _layer_prompt(x, nw, w_packed, params, layer, batch, seq):
    tt = PROMPT_TT
    tiles_per_seq = seq // tt
    n_tiles = batch * tiles_per_seq
    assert tiles_per_seq % 2 == 0
    n_steps = n_tiles // 2 + 1

    def seq_of_odd_tile(g):
        return jnp.where(g == 0, batch, (2 * g - 1) // tiles_per_seq)

    def state_spec(tail):
        return pl.BlockSpec((1,) + tail, lambda g: (seq_of_odd_tile(g),) + (0,) * len(tail))

    tails = ((GDN_HEADS, GDN_HEAD_DIM, GDN_HEAD_DIM), (SSD_HEADS, SSD_HEAD_DIM, SSD_STATE),
             (CONV_K - 1, 3 * D_GDN), (CONV_K - 1, SSD_CONV_CH), (POOL_BUF, D_POOL))
    return pl.pallas_call(
        functools.partial(_layer_prompt_kernel, tiles_per_seq=tiles_per_seq, pos0=0),
        grid=(n_steps,),
        in_specs=[
            pl.BlockSpec((2 * tt, D_MODEL), lambda g: (jnp.minimum(g, n_steps - 2), 0)),
            pl.BlockSpec((None, 1, D_MODEL), lambda g: (layer, 0, 0)),
            pl.BlockSpec((None, D_MODEL, D_PACK), lambda g: (layer, 0, 0), pipeline_mode=pl.Buffered(1)),
        ] + _param_specs(params, layer),
        out_specs=[
            pl.BlockSpec((tt, D_MODEL), lambda g: (jnp.maximum(g - 1, 0), 0)),
            pl.BlockSpec((tt, D_MODEL), lambda g: (g, 0)),
        ] + [state_spec(t) for t in tails],
        out_shape=[
            jax.ShapeDtypeStruct((n_tiles // 2 * tt, D_MODEL), BF16),
            jax.ShapeDtypeStruct((n_steps * tt, D_MODEL), BF16),
        ] + [jax.ShapeDtypeStruct((batch + 1,) + t, F32) for t in tails],
        scratch_shapes=[
            pltpu.VMEM((tt, D_PACK), F32),
            pltpu.VMEM((tt, D_PACK), F32),
            pltpu.VMEM(tails[0], F32),
            pltpu.VMEM(tails[1], F32),
            pltpu.VMEM((HIST + tt, 3 * D_GDN), F32),
            pltpu.VMEM((HIST + tt, SSD_CONV_CH), F32),
            pltpu.VMEM((PHIST + tt, D_POOL), F32),
        ],
        compiler_params=pltpu.CompilerParams(
            dimension_semantics=("arbitrary",), vmem_limit_bytes=VMEM_LIMIT_PROJ),
        name="layer_prompt",
    )(x, nw, w_packed, *params)


def _mixer_sample_kernel(*refs, n_alias, pos0):
    proj_ref, sg_in, gcv_in, ss_in, scv_in, pool_in = refs[:6]
    (gp_ref, gcw_ref, gnorm_ref, scw_ref, scb_ref, sd_ref, snorm_ref, pw_ref, pscale_ref,
     y_ref, sg_out, gcv_out, ss_out, scv_out, pool_out) = refs[6 + n_alias:]
    nb, t_len = SAMPLE_NB, SAMPLE_T
    assert nb == SUBLANE
    rows = nb * t_len
    last = slice(rows - nb, rows)

    def tokens(c0, width):
        return [proj_ref[t, :, c0:c0 + width] for t in range(t_len)]

    def stack(parts):
        return jnp.concatenate(parts, axis=0)

    def conv(hist_ref, col0, cw, width, w_ref, b_ref=None):
        u = [hist_ref[j, :, cw:cw + width] for j in range(CONV_K - 1)] + tokens(col0 + cw, width)
        outs = []
        for t in range(t_len):
            acc = u[t] * w_ref[0:1, cw:cw + width]
            for j in range(1, CONV_K):
                acc = acc + u[t + j] * w_ref[j:j + 1, cw:cw + width]
            if b_ref is not None:
                acc = acc + b_ref[0:1, cw:cw + width]
            outs.append(_silu(acc))
        return stack(outs)

    beta, sp, gd = _gates(stack(tokens(C_SM, LANE)), gp_ref)
    parts = [gd[0:nb]]
    for t in range(1, t_len):
        parts.append(parts[-1] + gd[t * nb:(t + 1) * nb])
    cum = stack(parts)
    cum_t = cum.T

    ri = lax.broadcasted_iota(jnp.int32, (rows, rows), 0)
    rj = lax.broadcasted_iota(jnp.int32, (rows, rows), 1)
    same = (ri & (nb - 1)) == (rj & (nb - 1))
    causal, strict = same & (ri >= rj), same & (ri > rj)
    seq1 = lax.broadcasted_iota(jnp.int32, (rows, 1), 0) & (nb - 1)
    seq2 = lax.broadcasted_iota(jnp.int32, (2 * rows, 1), 0) & (nb - 1)

    def block_diag(x):
        return jnp.concatenate([jnp.where(seq1 == b, x, 0.0) for b in range(nb)], axis=1)

    def store_y(c0, val):
        for t in range(t_len):
            y_ref[t, :, c0:c0 + val.shape[1]] = val[t * nb:(t + 1) * nb]

    heads = range(GDN_HEADS)
    bmat, bb, rhs, qkb, qdec, kd_t, egl = {}, {}, {}, {}, {}, {}, {}
    for h in heads:
        c0 = h * GDN_HEAD_DIM
        qh = _l2norm(conv(gcv_in, C_QKV, c0, GDN_HEAD_DIM, gcw_ref)) * (GDN_HEAD_DIM ** -0.5)
        kh = _l2norm(conv(gcv_in, C_QKV, D_GDN + c0, GDN_HEAD_DIM, gcw_ref))
        vh = conv(gcv_in, C_QKV, 2 * D_GDN + c0, GDN_HEAD_DIM, gcw_ref)
        beta_c = beta[:, L_BETA + h:L_BETA + h + 1]
        gc_c = cum[:, L_GDN_G + h:L_GDN_G + h + 1]
        eg = jnp.exp(gc_c)
        decay = _masked_decay(gc_c, cum_t[L_GDN_G + h:L_GDN_G + h + 1, :], causal)
        kb, qb = _bf(kh), _bf(qh)
        bmat[h] = jnp.where(strict, -(_dot_nt(kb, kb) * decay * beta_c), 0.0)
        bb[h] = _bf(bmat[h])
        qkb[h] = _bf(jnp.where(causal, _dot_nt(qb, kb) * decay, 0.0))
        rhs[h] = jnp.concatenate([vh * beta_c, kh * (beta_c * eg)], axis=-1)
        qdec[h] = qh * eg
        gl = gc_c[last]
        kd_t[h] = _bf((kh * jnp.exp(stack([gl] * t_len) - gc_c)).T)
        egl[h] = jnp.exp(gl)
    p1 = {h: _dot(bb[h], bb[h]) for h in heads}
    nmat = {h: bmat[h] + p1[h] + _dot(bb[h], _bf(p1[h])) for h in heads}
    sol = {h: rhs[h] + _dot(_bf(nmat[h]), _bf(rhs[h])) for h in heads}
    acc = {}
    for h in heads:
        lhs = _bf(stack([sol[h][:, GDN_HEAD_DIM:], qdec[h]]))
        a = jnp.zeros((2 * rows, GDN_HEAD_DIM), F32)
        for b in range(nb):
            a = jnp.where(seq2 == b, _dot(lhs, _bf(sg_in[b, h])), a)
        acc[h] = a
    vn = {h: sol[h][:, :GDN_HEAD_DIM] - acc[h][:rows] for h in heads}
    o = {h: acc[h][rows:] + _dot(qkb[h], _bf(vn[h])) for h in heads}
    upd = {h: _dot(kd_t[h], _bf(block_diag(vn[h]))) for h in heads}
    for h in heads:
        c0 = h * GDN_HEAD_DIM
        store_y(c0, _gdn_finish(o[h], stack(tokens(C_ZG + c0, GDN_HEAD_DIM)), gnorm_ref))
        for b in range(nb):
            sg_out[b, h] = (sg_in[b, h] * egl[h][b:b + 1, :]
                            + upd[h][:, b * GDN_HEAD_DIM:(b + 1) * GDN_HEAD_DIM])

    lo_lanes = lax.broadcasted_iota(jnp.int32, (rows, 2 * SSD_HEAD_DIM), 1) < SSD_HEAD_DIM
    lo_rows = lax.broadcasted_iota(jnp.int32, (2 * SSD_HEAD_DIM, SSD_STATE), 0) < SSD_HEAD_DIM
    pairs = range(SSD_HEADS // 2)
    n_gp = SSD_HEADS_PER_GROUP // 2
    cgb, cb, bg_bd = {}, {}, {}
    for g in range(SSD_GROUPS):
        bg = conv(scv_in, C_XBC, D_SSD + g * SSD_STATE, SSD_STATE, scw_ref, scb_ref)
        cg = conv(scv_in, C_XBC, D_SSD + (SSD_GROUPS + g) * SSD_STATE, SSD_STATE, scw_ref, scb_ref)
        cgb[g] = _bf(cg)
        cb[g] = _dot_nt(cgb[g], _bf(bg))
        bg_bd[g] = _bf(block_diag(bg))
    xs2, a2, y, xin_t = {}, {}, {}, {}
    for p in pairs:
        g, c0 = p // n_gp, 2 * p * SSD_HEAD_DIM
        la, lb = L_SSD_DT + 2 * p, L_SSD_DT + 2 * p + 1
        xs2[p] = conv(scv_in, C_XBC, c0, 2 * SSD_HEAD_DIM, scw_ref, scb_ref)
        a2[p] = jnp.where(lo_lanes, cum[:, la:la + 1], cum[:, lb:lb + 1])
        xdt2 = xs2[p] * jnp.where(lo_lanes, sp[:, la:la + 1], sp[:, lb:lb + 1])
        seg_a = _masked_decay(cum[:, la:la + 1], cum_t[la:la + 1, :], causal)
        seg_b = _masked_decay(cum[:, lb:lb + 1], cum_t[lb:lb + 1, :], causal)
        y[p] = (_dot(_bf(cb[g] * seg_a), _bf(jnp.where(lo_lanes, xdt2, 0.0)))
                + _dot(_bf(cb[g] * seg_b), _bf(jnp.where(lo_lanes, 0.0, xdt2))))
        xin_t[p] = _bf((xdt2 * jnp.exp(stack([a2[p][last]] * t_len) - a2[p])).T)
    acc = {}
    for p in pairs:
        a = jnp.zeros((rows, 2 * SSD_HEAD_DIM), F32)
        for b in range(nb):
            state2 = ss_in[b, 2 * p:2 * p + 2].reshape(2 * SSD_HEAD_DIM, SSD_STATE)
            a = jnp.where(seq1 == b, _dot_nt(cgb[p // n_gp], _bf(state2)), a)
        acc[p] = a
    upd = {p: _dot(xin_t[p], bg_bd[p // n_gp]) for p in pairs}
    for p in pairs:
        la, lb = L_SSD_DT + 2 * p, L_SSD_DT + 2 * p + 1
        for b in range(nb):
            r = rows - nb + b
            elast = jnp.exp(jnp.where(lo_rows, cum[r:r + 1, la:la + 1], cum[r:r + 1, lb:lb + 1]))
            state2 = ss_in[b, 2 * p:2 * p + 2].reshape(2 * SSD_HEAD_DIM, SSD_STATE)
            ss_out[b, 2 * p:2 * p + 2] = (state2 * elast + upd[p][:, b * SSD_STATE:(b + 1) * SSD_STATE]).reshape(
                2, SSD_HEAD_DIM, SSD_STATE)
    for g in range(SSD_GROUPS):
        yz, ssq = {}, None
        for p in range(g * n_gp, (g + 1) * n_gp):
            c0 = 2 * p * SSD_HEAD_DIM
            z = stack(tokens(C_ZS + c0, 2 * SSD_HEAD_DIM))
            yt = y[p] + jnp.exp(a2[p]) * acc[p]
            yz[p] = (yt + sd_ref[0:1, c0:c0 + 2 * SSD_HEAD_DIM] * xs2[p]) * _silu(z)
            s = jnp.sum(yz[p] * yz[p], axis=-1, keepdims=True)
            ssq = s if ssq is None else ssq + s
        scale = lax.rsqrt(ssq * (1.0 / SSD_GROUP_DIM) + EPS)
        for p, t in yz.items():
            c0 = 2 * p * SSD_HEAD_DIM
            store_y(D_GDN + c0, t * scale * snorm_ref[0:1, c0:c0 + 2 * SSD_HEAD_DIM])

    for gi, win in enumerate(POOL_WINDOWS):
        sl = slice(gi * POOL_GROUP_DIM, (gi + 1) * POOL_GROUP_DIM)
        x = [pool_in[j, :, sl] for j in range(POOL_BUF)] + tokens(C_UP + gi * POOL_GROUP_DIM, POOL_GROUP_DIM)
        pooled = []
        for t in range(t_len):
            acc = x[POOL_BUF + t]
            for k in range(1, win):
                acc = acc + x[POOL_BUF + t - k]
            pooled.append(acc / float(min(win, pos0 + t + 1)) - x[POOL_BUF + t])
        mixed = _dot(_bf(stack(pooled)), pw_ref[gi])
        z = stack(tokens(C_ZP + gi * POOL_GROUP_DIM, POOL_GROUP_DIM))
        store_y(D_GDN + D_SSD + gi * POOL_GROUP_DIM, mixed * pscale_ref[0:1, sl] * _silu(z))

    for j in range(CONV_K - 1):
        t = t_len - (CONV_K - 1) + j
        gcv_out[j] = proj_ref[t, :, C_QKV:C_QKV + 3 * D_GDN]
        scv_out[j] = proj_ref[t, :, C_XBC:C_XBC + SSD_CONV_CH]
    for j in range(POOL_BUF):
        src = j + t_len
        pool_out[j] = pool_in[src] if src < POOL_BUF else proj_ref[src - POOL_BUF, :, C_UP:C_UP + D_POOL]


def _mixer_sample(proj, states, params, layer, n_seq, prev_outs):
    nb, t_len = SAMPLE_NB, SAMPLE_T
    assert CONV_K - 1 <= t_len and proj.shape == (t_len, n_seq, D_PACK)
    prev_outs = () if prev_outs is None else tuple(prev_outs)
    n_state = len(states)

    def seq_spec(s):
        if s.shape[1] == n_seq:
            tail = tuple(s.shape[2:])
            return pl.BlockSpec((None, nb) + tail, lambda i: (layer, i) + (0,) * len(tail))
        assert s.shape[2] == n_seq and s.ndim == 4
        return pl.BlockSpec((None, s.shape[1], nb, s.shape[3]), lambda i: (layer, 0, i, 0))

    return pl.pallas_call(
        functools.partial(_mixer_sample_kernel, n_alias=len(prev_outs), pos0=PAST_LEN),
        grid=(n_seq // nb,),
        in_specs=[pl.BlockSpec((t_len, nb, D_PACK), lambda i: (0, i, 0))]
        + [seq_spec(s) for s in states]
        + [pl.BlockSpec(memory_space=pl.ANY)] * len(prev_outs)
        + _param_specs(params, layer),
        out_specs=[pl.BlockSpec((t_len, nb, D_MODEL), lambda i: (0, i, 0))] + [seq_spec(s) for s in states],
        out_shape=[jax.ShapeDtypeStruct((t_len, n_seq, D_MODEL), F32)]
        + [jax.ShapeDtypeStruct(s.shape, F32) for s in states],
        input_output_aliases={1 + n_state + k: 1 + k for k in range(len(prev_outs))},
        compiler_params=pltpu.CompilerParams(
            dimension_semantics=("parallel",), vmem_limit_bytes=VMEM_LIMIT),
        name="mixer_sample",
    )(proj, *states, *prev_outs, *params)


def _gate_params(gdn_a_log, gdn_dt_bias, ssd_a_log, ssd_dt_bias):
    def lanes(gdn, ssd):
        z = jnp.zeros((DEPTH, LANE), F32)
        z = z.at[:, L_GDN_G:L_GDN_G + GDN_HEADS].set(gdn)
        return z.at[:, L_SSD_DT:L_SSD_DT + SSD_HEADS].set(ssd)

    rows = jnp.stack([lanes(gdn_a_log, ssd_a_log), lanes(gdn_dt_bias, ssd_dt_bias)], axis=1)
    return jnp.concatenate([rows, jnp.zeros((DEPTH, SUBLANE - 2, LANE), F32)], axis=1)


def kernel(x_prompt, x_sample, state_gdn, state_gdn_conv, state_ssd, state_ssd_conv, state_pool, norm_w, w_in, gdn_conv_w, gdn_a_log, gdn_dt_bias, gdn_norm_w, ssd_conv_w, ssd_conv_b, ssd_a_log, ssd_dt_bias, ssd_d, ssd_norm_w, pool_w, pool_scale, w_out, final_norm_w):
    batch, seq, _ = x_prompt.shape
    n_seq, t_len, _ = x_sample.shape
    assert t_len == SAMPLE_T and seq % PROMPT_TT == 0 and n_seq % SAMPLE_NB == 0

    w_in_p = _pack_w_in(jnp.swapaxes(w_in, 1, 2))
    w_out_b = w_out.astype(BF16)
    params = (
        _gate_params(gdn_a_log, gdn_dt_bias, ssd_a_log, ssd_dt_bias),
        gdn_conv_w,
        gdn_norm_w[:, None, :],
        ssd_conv_w,
        ssd_conv_b[:, None, :],
        jnp.repeat(ssd_d, SSD_HEAD_DIM, axis=-1)[:, None, :],
        ssd_norm_w[:, None, :],
        pool_w.astype(BF16),
        pool_scale[:, None, :],
    )
    nw = norm_w[:, None, :]
    fw = final_norm_w[None, :]
    states = (state_gdn, jnp.swapaxes(state_gdn_conv, 1, 2), state_ssd, jnp.swapaxes(state_ssd_conv, 1, 2),
              jnp.swapaxes(state_pool, 1, 2))

    xp = x_prompt.reshape(batch * seq, D_MODEL)
    xs = jnp.swapaxes(x_sample, 0, 1).reshape(t_len * n_seq, D_MODEL)
    new_p = [[] for _ in range(5)]
    new_s = None
    for layer in range(DEPTH):
        final = fw if layer == DEPTH - 1 else None

        y_odd, y_even, gdn_p, ssd_p, gcv_p, scv_p, pool_p = _layer_prompt(xp, nw, w_in_p, params, layer, batch, seq)
        xp = _outproj((y_even, y_odd), w_out_b, xp, layer, 512, final)
        for acc, st in zip(new_p, (gdn_p, gcv_p, ssd_p, scv_p, pool_p)):
            acc.append(st[:batch])

        proj_s = _inproj(xs, nw, w_in_p, layer, n_seq * t_len).reshape(t_len, n_seq, D_PACK)
        y_s, *new_s = _mixer_sample(proj_s, states, params, layer, n_seq, new_s)
        xs = _outproj(y_s.reshape(t_len * n_seq, D_MODEL), w_out_b, xs, layer, n_seq * t_len, final)

    new_s = [a if a.shape[1] == n_seq else jnp.swapaxes(a, 1, 2) for a in new_s]
    y_prompt = xp.reshape(batch, seq, D_MODEL)
    y_sample = jnp.swapaxes(xs.reshape(t_len, n_seq, D_MODEL), 0, 1)
    return (y_prompt, y_sample) + tuple(jnp.stack(a) for a in new_p) + tuple(new_s)
```

```python
import functools

import jax
import jax.numpy as jnp
import numpy as np
from jax import lax
from jax.experimental import pallas as pl
from jax.experimental.pallas import tpu as pltpu

F32 = jnp.float32
BF16 = jnp.bfloat16

D_MODEL = 2048
DEPTH = 4
PAST_LEN = 16384
D_GDN = 768
D_SSD = 768
D_POOL = 512
CONV_K = 4
GDN_HEAD_DIM = 128
GDN_HEADS = 6
GDN_CHUNK = 64
SSD_HEAD_DIM = 64
SSD_HEADS = 12
SSD_STATE = 128
SSD_GROUPS = 2
SSD_HEADS_PER_GROUP = 6
SSD_GROUP_DIM = D_SSD // SSD_GROUPS
SSD_CHUNK = 128
SSD_CONV_CH = 1280
POOL_WINDOWS = (2, 4, 8, 16)
POOL_GROUP_DIM = 128
POOL_BUF = 15
IN_SIZES = (3 * D_GDN, D_GDN, GDN_HEADS, GDN_HEADS, D_SSD, SSD_CONV_CH, SSD_HEADS, D_POOL, D_POOL)
EPS = 1e-6

C_QKV = 0
C_ZG = 2304
C_ZS = 3072
C_XBC = 3840
C_UP = 5120
C_ZP = 5632
C_SM = 6144
D_PACK = 6400
L_BETA = 0
L_GDN_G = 6
L_SSD_DT = 16

LANE = 128
SUBLANE = 8
MXU_WIDTH = 256
VMEM_LIMIT = 48 * 1024 * 1024
VMEM_LIMIT_PROJ = 56 * 1024 * 1024

PROJ_TN = 5 * MXU_WIDTH
PACK_TK = 256
PROMPT_TT = 128
SAMPLE_NB = 8
SAMPLE_T = 4


def _dot(a, b):
    return jnp.dot(a, b, preferred_element_type=F32)


def _dot_nt(a, b):
    return lax.dot_general(a, b, (((1,), (1,)), ((), ())), preferred_element_type=F32)


def _dot_tn(a, b):
    return lax.dot_general(a, b, (((0,), (0,)), ((), ())), preferred_element_type=F32)


def _sigmoid(x):
    return 0.5 + 0.5 * jnp.tanh(0.5 * x)


def _silu(x):
    hx = 0.5 * x
    return hx + hx * jnp.tanh(hx)


def _softplus(x):
    return jnp.maximum(x, 0.0) + jnp.log(1.0 + jnp.exp(-jnp.abs(x)))


def _bf(x):
    return x.astype(BF16)


def _inproj_kernel(x_ref, nw_ref, w_ref, o_ref, h_ref):
    @pl.when(pl.program_id(1) == 0)
    def _():
        x = x_ref[...]
        ms = jnp.mean(x * x, axis=-1, keepdims=True)
        h_ref[...] = _bf(x * lax.rsqrt(ms + EPS) * nw_ref[...])

    o_ref[...] = _dot(h_ref[...], w_ref[...])


def _inproj(x, nw, w_packed, layer, tm):
    n = x.shape[0]
    return pl.pallas_call(
        _inproj_kernel,
        grid=(n // tm, D_PACK // PROJ_TN),
        in_specs=[
            pl.BlockSpec((tm, D_MODEL), lambda i, j: (i, 0)),
            pl.BlockSpec((None, 1, D_MODEL), lambda i, j: (layer, 0, 0)),
            pl.BlockSpec((None, D_MODEL, PROJ_TN), lambda i, j: (layer, 0, j)),
        ],
        out_specs=pl.BlockSpec((tm, PROJ_TN), lambda i, j: (i, j)),
        out_shape=jax.ShapeDtypeStruct((n, D_PACK), F32),
        scratch_shapes=[pltpu.VMEM((tm, D_MODEL), BF16)],
        compiler_params=pltpu.CompilerParams(
            dimension_semantics=("parallel", "arbitrary"), vmem_limit_bytes=VMEM_LIMIT_PROJ),
        name="inproj",
    )(x, nw, w_packed)


def _pack_kernel(w_ref, o_ref):
    offs = np.cumsum((0,) + IN_SIZES)
    tk = o_ref.shape[0]

    for seg, c_out in ((0, C_QKV), (1, C_ZG), (4, C_ZS), (5, C_XBC), (7, C_UP), (8, C_ZP)):
        o_ref[:, c_out:c_out + IN_SIZES[seg]] = _bf(w_ref[offs[seg]:offs[seg + 1], :].T)
    assert offs[3] == offs[2] + GDN_HEADS and L_GDN_G == L_BETA + GDN_HEADS and L_BETA == 0
    assert offs[2] % SUBLANE == 0 and offs[2] + LANE <= offs[-1]
    lane = lax.broadcasted_iota(jnp.int32, (tk, LANE), 1)
    gates = w_ref[offs[2]:offs[2] + LANE, :].T
    dt_row0 = offs[6] - offs[6] % SUBLANE
    dt_lane0 = offs[6] - dt_row0
    assert dt_row0 + LANE <= offs[-1] and dt_lane0 <= L_SSD_DT
    dts = pltpu.roll(w_ref[dt_row0:dt_row0 + LANE, :].T, L_SSD_DT - dt_lane0, 1)
    small = jnp.where(lane < 2 * GDN_HEADS, gates,
                      jnp.where((lane >= L_SSD_DT) & (lane < L_SSD_DT + SSD_HEADS), dts, 0.0))
    o_ref[:, C_SM:C_SM + LANE] = _bf(small)
    o_ref[:, C_SM + LANE:D_PACK] = jnp.zeros((tk, D_PACK - C_SM - LANE), BF16)


def _pack_w_in(w_in_t):
    d_in = w_in_t.shape[1]
    return pl.pallas_call(
        _pack_kernel,
        grid=(DEPTH, D_MODEL // PACK_TK),
        in_specs=[pl.BlockSpec((None, d_in, PACK_TK), lambda l, i: (l, 0, i))],
        out_specs=pl.BlockSpec((None, PACK_TK, D_PACK), lambda l, i: (l, i, 0)),
        out_shape=jax.ShapeDtypeStruct((DEPTH, D_MODEL, D_PACK), BF16),
        compiler_params=pltpu.CompilerParams(
            dimension_semantics=("parallel", "parallel"), vmem_limit_bytes=VMEM_LIMIT),
        name="pack_w_in",
    )(w_in_t)


def _outproj_kernel(*refs, n_y, final):
    y_refs, (w32_ref, x_ref), rest = refs[:n_y], refs[n_y:n_y + 2], refs[n_y + 2:]
    o_ref, w_ref = rest[-2], rest[-1]

    @pl.when(pl.program_id(0) == 0)
    def _():
        w_ref[...] = _bf(w32_ref[...])

    if n_y == 1:
        y = y_refs[0][...]
    else:
        tt = PROMPT_TT
        y = jnp.concatenate([r[k * tt:(k + 1) * tt, :] for k in range(y_refs[0].shape[0] // tt) for r in y_refs],
                            axis=0)
    x = x_ref[...] + _dot(_bf(y), w_ref[...])
    if final:
        ms = jnp.mean(x * x, axis=-1, keepdims=True)
        x = x * lax.rsqrt(ms + EPS) * rest[0][...]
    o_ref[...] = x


def _outproj(ys, w_out, x, layer, tm, final_w=None):
    n = x.shape[0]
    ys = tuple(ys) if isinstance(ys, (tuple, list)) else (ys,)
    in_specs = [pl.BlockSpec((tm // len(ys), D_MODEL), lambda i: (i, 0)) for _ in ys] + [
        pl.BlockSpec((None, D_MODEL, D_MODEL), lambda i: (layer, 0, 0), pipeline_mode=pl.Buffered(1)),
        pl.BlockSpec((tm, D_MODEL), lambda i: (i, 0)),
    ]
    args = [*ys, w_out, x]
    if final_w is not None:
        in_specs.append(pl.BlockSpec((1, D_MODEL), lambda i: (0, 0)))
        args.append(final_w)
    return pl.pallas_call(
        functools.partial(_outproj_kernel, n_y=len(ys), final=final_w is not None),
        grid=(n // tm,),
        in_specs=in_specs,
        out_specs=pl.BlockSpec((tm, D_MODEL), lambda i: (i, 0)),
        out_shape=jax.ShapeDtypeStruct((n, D_MODEL), F32),
        scratch_shapes=[pltpu.VMEM((D_MODEL, D_MODEL), BF16)],
        compiler_params=pltpu.CompilerParams(
            dimension_semantics=("arbitrary",), vmem_limit_bytes=VMEM_LIMIT_PROJ),
        name="outproj",
    )(*args)


def _conv_silu_tile(ext_ref, hist, c0, width, w_ref, wc0, b_ref=None):
    e = ext_ref[:, c0:c0 + width]
    acc = e * w_ref[0:1, wc0:wc0 + width]
    for j in range(1, CONV_K):
        acc = pltpu.roll(acc, 1, 0) + e * w_ref[j:j + 1, wc0:wc0 + width]
    acc = acc[hist:]
    if b_ref is not None:
        acc = acc + b_ref[0:1, wc0:wc0 + width]
    return _silu(acc)


def _l2norm(x):
    return x * lax.rsqrt(jnp.sum(x * x, axis=-1, keepdims=True) + EPS)


def _seg_cumsum(x, seg_len, row_in_seg):
    s = 1
    while s < seg_len:
        x = x + jnp.where(row_in_seg >= s, pltpu.roll(x, s, 0), 0.0)
        s *= 2
    return x


def _gates(small, gp_ref):
    sp = _softplus(small + gp_ref[1:2, :])
    gd = -jnp.exp(gp_ref[0:1, :]) * sp
    return _sigmoid(small), sp, gd


def _masked_decay(col, row, causal):
    return jnp.where(causal, jnp.exp(jnp.where(causal, col - row, 0.0)), 0.0)


def _gdn_finish(o, z, gnorm_ref):
    on = o * lax.rsqrt(jnp.mean(o * o, axis=-1, keepdims=True) + EPS) * gnorm_ref[0:1, :]
    return on * _silu(z)


HIST = SUBLANE
PHIST = 2 * SUBLANE


def _reset_prompt_state(sg_ref, ss_ref, eq_ref, ex_ref, ep_ref):
    sg_ref[...] = jnp.zeros(sg_ref.shape, F32)
    ss_ref[...] = jnp.zeros(ss_ref.shape, F32)
    eq_ref[0:HIST, :] = jnp.zeros((HIST, eq_ref.shape[1]), F32)
    ex_ref[0:HIST, :] = jnp.zeros((HIST, ex_ref.shape[1]), F32)
    ep_ref[0:PHIST, :] = jnp.zeros((PHIST, D_POOL), F32)


def _mixer_tile(proj_ref, y_ref, i, pos0, params, sg_ref, ss_ref, eq_ref, ex_ref, ep_ref):
    gp_ref, gcw_ref, gnorm_ref, scw_ref, scb_ref, sd_ref, snorm_ref, pw_ref, pscale_ref = params
    tt = PROMPT_TT
    n_chunks = tt // GDN_CHUNK
    hist, phist = HIST, PHIST

    eq_ref[hist:hist + tt, :] = proj_ref[:, C_QKV:C_QKV + 3 * D_GDN]
    ex_ref[hist:hist + tt, :] = proj_ref[:, C_XBC:C_XBC + SSD_CONV_CH]
    ep_ref[phist:phist + tt, :] = proj_ref[:, C_UP:C_UP + D_POOL]

    beta, sp, gd = _gates(proj_ref[:, C_SM:C_SM + LANE], gp_ref)
    row = lax.broadcasted_iota(jnp.int32, (tt, LANE), 0)
    gc = _seg_cumsum(gd, GDN_CHUNK, row & (GDN_CHUNK - 1))
    ac = _seg_cumsum(gd, SSD_CHUNK, row)

    def store_y(c0, val, r0=0):
        y_ref[r0:r0 + val.shape[0], c0:c0 + val.shape[1]] = val.astype(y_ref.dtype)

    ci = lax.broadcasted_iota(jnp.int32, (GDN_CHUNK, GDN_CHUNK), 0)
    cj = lax.broadcasted_iota(jnp.int32, (GDN_CHUNK, GDN_CHUNK), 1)
    causal_g, strict_g = ci >= cj, ci > cj
    gc_t = [gc[c * GDN_CHUNK:(c + 1) * GDN_CHUNK].T for c in range(n_chunks)]
    conv_q = functools.partial(_conv_silu_tile, eq_ref, hist)

    keys = [(c, h) for c in range(n_chunks) for h in range(GDN_HEADS)]
    nmat, pmat, rhs, qkb, qdec, kdec_t, glast = {}, {}, {}, {}, {}, {}, {}
    for h in range(GDN_HEADS):
        c0 = h * GDN_HEAD_DIM
        q_all = _l2norm(conv_q(c0, GDN_HEAD_DIM, gcw_ref, c0)) * (GDN_HEAD_DIM ** -0.5)
        k_all = _l2norm(conv_q(D_GDN + c0, GDN_HEAD_DIM, gcw_ref, D_GDN + c0))
        v_all = conv_q(2 * D_GDN + c0, GDN_HEAD_DIM, gcw_ref, 2 * D_GDN + c0)
        for c in range(n_chunks):
            rs = slice(c * GDN_CHUNK, (c + 1) * GDN_CHUNK)
            qh, kh, vh = q_all[rs], k_all[rs], v_all[rs]
            beta_c = beta[rs, L_BETA + h:L_BETA + h + 1]
            gc_c = gc[rs, L_GDN_G + h:L_GDN_G + h + 1]
            gc_r = gc_t[c][L_GDN_G + h:L_GDN_G + h + 1, :]
            gl = gc_c[GDN_CHUNK - 1:GDN_CHUNK, :]
            eg = jnp.exp(gc_c)
            decay = _masked_decay(gc_c, gc_r, causal_g)
            kb, qb = _bf(kh), _bf(qh)
            nmat[c, h] = jnp.where(strict_g, -(_dot_nt(kb, kb) * decay * beta_c), 0.0)
            qkb[c, h] = _bf(jnp.where(causal_g, _dot_nt(qb, kb) * decay, 0.0))
            rhs[c, h] = jnp.concatenate([vh * beta_c, kh * (beta_c * eg)], axis=-1)
            qdec[c, h] = qh * eg
            kdec_t[c, h] = _bf((kh * jnp.exp(gl - gc_c)).T)
            glast[c, h] = jnp.exp(gl)

    for key in keys:
        bb = _bf(nmat[key])
        pmat[key] = _dot(bb, bb)
    for _ in range(4):
        for key in keys:
            pb = _bf(pmat[key])
            r = _dot(jnp.concatenate([_bf(nmat[key]), pb], axis=0), pb)
            nmat[key] = nmat[key] + pmat[key] + r[:GDN_CHUNK]
            pmat[key] = r[GDN_CHUNK:]
    sol = {}
    for key in keys:
        nmat[key] = nmat[key] + pmat[key] + _dot(_bf(nmat[key]), _bf(pmat[key]))
    for key in keys:
        sol[key] = rhs[key] + _dot(_bf(nmat[key]), _bf(rhs[key]))

    for c in range(n_chunks):
        r0 = c * GDN_CHUNK
        for h in range(GDN_HEADS):
            key = (c, h)
            state = sg_ref[h]
            u, w = sol[key][:, :GDN_HEAD_DIM], sol[key][:, GDN_HEAD_DIM:]
            r1 = _dot(_bf(jnp.concatenate([w, qdec[key]], axis=0)), _bf(state))
            vnb = _bf(u - r1[:GDN_CHUNK])
            r2 = _dot(jnp.concatenate([qkb[key], kdec_t[key]], axis=0), vnb)
            sg_ref[h] = state * glast[key] + r2[GDN_CHUNK:]
            o = r1[GDN_CHUNK:] + r2[:GDN_CHUNK]
            c0 = h * GDN_HEAD_DIM
            z = proj_ref[r0:r0 + GDN_CHUNK, C_ZG + c0:C_ZG + c0 + GDN_HEAD_DIM]
            store_y(c0, _gdn_finish(o, z, gnorm_ref), r0)

    si = lax.broadcasted_iota(jnp.int32, (SSD_CHUNK, SSD_CHUNK), 0)
    sj = lax.broadcasted_iota(jnp.int32, (SSD_CHUNK, SSD_CHUNK), 1)
    causal_s = si >= sj
    lo_lanes = sj < SSD_HEAD_DIM
    lo_rows = si < SSD_HEAD_DIM
    ac_t = ac.T
    conv_x = functools.partial(_conv_silu_tile, ex_ref, hist)
    for g in range(SSD_GROUPS):
        bgb = _bf(conv_x(D_SSD + g * SSD_STATE, SSD_STATE, scw_ref, D_SSD + g * SSD_STATE, scb_ref))
        cgb = _bf(conv_x(D_SSD + (SSD_GROUPS + g) * SSD_STATE, SSD_STATE, scw_ref,
                         D_SSD + (SSD_GROUPS + g) * SSD_STATE, scb_ref))
        cb = _dot_nt(cgb, bgb)
        yz, ssq = [], None
        for pair in range(SSD_HEADS_PER_GROUP // 2):
            hd0 = g * SSD_HEADS_PER_GROUP + 2 * pair
            c0 = hd0 * SSD_HEAD_DIM
            la, lb = L_SSD_DT + hd0, L_SSD_DT + hd0 + 1
            xs2 = conv_x(c0, 2 * SSD_HEAD_DIM, scw_ref, c0, scb_ref)
            a2 = jnp.where(lo_lanes, ac[:, la:la + 1], ac[:, lb:lb + 1])
            xdt2 = xs2 * jnp.where(lo_lanes, sp[:, la:la + 1], sp[:, lb:lb + 1])
            alast2 = a2[tt - 1:tt, :]
            state2 = ss_ref[hd0:hd0 + 2].reshape(2 * SSD_HEAD_DIM, SSD_STATE)
            y = jnp.exp(a2) * _dot_nt(cgb, _bf(state2))
            seg_a = _masked_decay(ac[:, la:la + 1], ac_t[la:la + 1, :], causal_s)
            seg_b = _masked_decay(ac[:, lb:lb + 1], ac_t[lb:lb + 1, :], causal_s)
            y = y + _dot(_bf(cb * seg_a), _bf(jnp.where(lo_lanes, xdt2, 0.0)))
            y = y + _dot(_bf(cb * seg_b), _bf(jnp.where(lo_lanes, 0.0, xdt2)))
            upd = _dot_tn(_bf(xdt2 * jnp.exp(alast2 - a2)), bgb)
            elast = jnp.exp(jnp.where(lo_rows, ac[tt - 1:tt, la:la + 1], ac[tt - 1:tt, lb:lb + 1]))
            ss_ref[hd0:hd0 + 2] = (state2 * elast + upd).reshape(2, SSD_HEAD_DIM, SSD_STATE)
            z = proj_ref[:, C_ZS + c0:C_ZS + c0 + 2 * SSD_HEAD_DIM]
            t = (y + sd_ref[0:1, c0:c0 + 2 * SSD_HEAD_DIM] * xs2) * _silu(z)
            yz.append(t)
            s = jnp.sum(t * t, axis=-1, keepdims=True)
            ssq = s if ssq is None else ssq + s
        scale = lax.rsqrt(ssq * (1.0 / SSD_GROUP_DIM) + EPS)
        for pair, t in enumerate(yz):
            c0 = (g * SSD_HEADS_PER_GROUP + 2 * pair) * SSD_HEAD_DIM
            store_y(D_GDN + c0, t * scale * snorm_ref[0:1, c0:c0 + 2 * SSD_HEAD_DIM])

    pos = pos0 + i * tt + lax.broadcasted_iota(jnp.int32, (tt, 1), 0)
    for gi, win in enumerate(POOL_WINDOWS):
        sl = slice(gi * POOL_GROUP_DIM, (gi + 1) * POOL_GROUP_DIM)
        e = ep_ref[:, sl]
        acc, k = e, 1
        while k < win:
            acc = acc + pltpu.roll(acc, k, 0)
            k *= 2
        cnt = jnp.minimum(win, pos + 1).astype(F32)
        pooled = acc[phist:] / cnt - e[phist:]
        mixed = _dot(_bf(pooled), pw_ref[gi])
        z = proj_ref[:, C_ZP + gi * POOL_GROUP_DIM:C_ZP + (gi + 1) * POOL_GROUP_DIM]
        store_y(D_GDN + D_SSD + gi * POOL_GROUP_DIM, mixed * pscale_ref[0:1, sl] * _silu(z))

    eq_ref[0:hist, :] = eq_ref[tt:tt + hist, :]
    ex_ref[0:hist, :] = ex_ref[tt:tt + hist, :]
    ep_ref[0:phist, :] = ep_ref[tt:tt + phist, :]


def _layer_spec(shape):
    nd = len(shape)
    return lambda layer: pl.BlockSpec((None,) + tuple(shape[1:]), lambda *_: (layer,) + (0,) * (nd - 1))


def _param_specs(params, layer):
    return [_layer_spec(p.shape)(layer) for p in params]


def _layer_prompt_kernel(x_ref, nw_ref, w_ref, *refs, tiles_per_seq, n_steps, pos0):
    params = refs[:9]
    (y_odd_ref, y_even_ref, sg_out, ss_out, gcv_out, scv_out, pool_out,
     p_even, p_odd, sg_ref, ss_ref, eq_ref, ex_ref, ep_ref) = refs[9:]
    tt = PROMPT_TT
    g = pl.program_id(0)
    state = (sg_ref, ss_ref, eq_ref, ex_ref, ep_ref)

    def project(rows, p_ref):
        x = x_ref[rows, :]
        ms = jnp.mean(x * x, axis=-1, keepdims=True)
        p_ref[...] = _dot(_bf(x * lax.rsqrt(ms + EPS) * nw_ref[...]), w_ref[...])

    @pl.when(g == 0)
    def _():
        project(slice(0, tt), p_even)

    @pl.when(g > 0)
    def _():
        project(slice(0, tt), p_even)
        _mixer_tile(p_odd, y_odd_ref, lax.rem(2 * g - 1, tiles_per_seq), pos0, params, *state)
        sg_out[0] = sg_ref[...]
        ss_out[0] = ss_ref[...]
        gcv_out[0] = eq_ref[HIST + tt - (CONV_K - 1):HIST + tt, :]
        scv_out[0] = ex_ref[HIST + tt - (CONV_K - 1):HIST + tt, :]
        pool_out[0] = ep_ref[PHIST + tt - POOL_BUF:PHIST + tt, :]

    @pl.when(g < n_steps - 1)
    def _():
        i_even = lax.rem(2 * g, tiles_per_seq)

        @pl.when(i_even == 0)
        def _():
            _reset_prompt_state(*state)

        project(slice(tt, 2 * tt), p_odd)
        _mixer_tile(p_even, y_even_ref, i_even, pos0, params, *state)


def _layer_prompt(x, nw, w_packed, params, layer, batch, seq):
    tt = PROMPT_TT
    tiles_per_seq = seq // tt
    n_tiles = batch * tiles_per_seq
    assert tiles_per_seq % 2 == 0
    n_steps = n_tiles // 2 + 1

    def seq_of_odd_tile(g):
        return jnp.maximum(2 * g - 1, 0) // tiles_per_seq

    def state_spec(tail):
        return pl.BlockSpec((1,) + tail, lambda g: (seq_of_odd_tile(g),) + (0,) * len(tail))

    tails = ((GDN_HEADS, GDN_HEAD_DIM, GDN_HEAD_DIM), (SSD_HEADS, SSD_HEAD_DIM, SSD_STATE),
             (CONV_K - 1, 3 * D_GDN), (CONV_K - 1, SSD_CONV_CH), (POOL_BUF, D_POOL))
    return pl.pallas_call(
        functools.partial(_layer_prompt_kernel, tiles_per_seq=tiles_per_seq, n_steps=n_steps, pos0=0),
        grid=(n_steps,),
        in_specs=[
            pl.BlockSpec((2 * tt, D_MODEL), lambda g: (jnp.minimum(g, n_steps - 2), 0)),
            pl.BlockSpec((None, 1, D_MODEL), lambda g: (layer, 0, 0)),
            pl.BlockSpec((None, D_MODEL, D_PACK), lambda g: (layer, 0, 0), pipeline_mode=pl.Buffered(1)),
        ] + _param_specs(params, layer),
        out_specs=[
            pl.BlockSpec((tt, D_MODEL), lambda g: (jnp.maximum(g - 1, 0), 0)),
            pl.BlockSpec((tt, D_MODEL), lambda g: (jnp.minimum(g, n_steps - 2), 0)),
        ] + [state_spec(t) for t in tails],
        out_shape=[jax.ShapeDtypeStruct((n_tiles // 2 * tt, D_MODEL), BF16)] * 2
        + [jax.ShapeDtypeStruct((batch,) + t, F32) for t in tails],
        scratch_shapes=[
            pltpu.VMEM((tt, D_PACK), F32),
            pltpu.VMEM((tt, D_PACK), F32),
            pltpu.VMEM(tails[0], F32),
            pltpu.VMEM(tails[1], F32),
            pltpu.VMEM((HIST + tt, 3 * D_GDN), F32),
            pltpu.VMEM((HIST + tt, SSD_CONV_CH), F32),
            pltpu.VMEM((PHIST + tt, D_POOL), F32),
        ],
        compiler_params=pltpu.CompilerParams(
            dimension_semantics=("arbitrary",), vmem_limit_bytes=VMEM_LIMIT_PROJ),
        name="layer_prompt",
    )(x, nw, w_packed, *params)


def _mixer_sample_kernel(*refs, n_alias, pos0):
    proj_ref, sg_in, gcv_in, ss_in, scv_in, pool_in = refs[:6]
    (gp_ref, gcw_ref, gnorm_ref, scw_ref, scb_ref, sd_ref, snorm_ref, pw_ref, pscale_ref,
     y_ref, sg_out, gcv_out, ss_out, scv_out, pool_out) = refs[6 + n_alias:]
    nb, t_len = SAMPLE_NB, SAMPLE_T
    assert nb == SUBLANE
    rows = nb * t_len
    last = slice(rows - nb, rows)

    def tokens(c0, width):
        return [proj_ref[t, :, c0:c0 + width] for t in range(t_len)]

    def stack(parts):
        return jnp.concatenate(parts, axis=0)

    def conv(hist_ref, col0, cw, width, w_ref, b_ref=None):
        u = [hist_ref[j, :, cw:cw + width] for j in range(CONV_K - 1)] + tokens(col0 + cw, width)
        outs = []
        for t in range(t_len):
            acc = u[t] * w_ref[0:1, cw:cw + width]
            for j in range(1, CONV_K):
                acc = acc + u[t + j] * w_ref[j:j + 1, cw:cw + width]
            if b_ref is not None:
                acc = acc + b_ref[0:1, cw:cw + width]
            outs.append(_silu(acc))
        return stack(outs)

    beta, sp, gd = _gates(stack(tokens(C_SM, LANE)), gp_ref)
    parts = [gd[0:nb]]
    for t in range(1, t_len):
        parts.append(parts[-1] + gd[t * nb:(t + 1) * nb])
    cum = stack(parts)
    cum_t = cum.T

    ri = lax.broadcasted_iota(jnp.int32, (rows, rows), 0)
    rj = lax.broadcasted_iota(jnp.int32, (rows, rows), 1)
    same = (ri & (nb - 1)) == (rj & (nb - 1))
    causal, strict = same & (ri >= rj), same & (ri > rj)
    seq1 = lax.broadcasted_iota(jnp.int32, (rows, 1), 0) & (nb - 1)
    seq2 = lax.broadcasted_iota(jnp.int32, (2 * rows, 1), 0) & (nb - 1)

    def block_diag(x):
        return jnp.concatenate([jnp.where(seq1 == b, x, 0.0) for b in range(nb)], axis=1)

    def store_y(c0, val):
        for t in range(t_len):
            y_ref[t, :, c0:c0 + val.shape[1]] = val[t * nb:(t + 1) * nb]

    heads = range(GDN_HEADS)
    bmat, bb, rhs, qkb, qdec, kd_t, egl = {}, {}, {}, {}, {}, {}, {}
    for h in heads:
        c0 = h * GDN_HEAD_DIM
        qh = _l2norm(conv(gcv_in, C_QKV, c0, GDN_HEAD_DIM, gcw_ref)) * (GDN_HEAD_DIM ** -0.5)
        kh = _l2norm(conv(gcv_in, C_QKV, D_GDN + c0, GDN_HEAD_DIM, gcw_ref))
        vh = conv(gcv_in, C_QKV, 2 * D_GDN + c0, GDN_HEAD_DIM, gcw_ref)
        beta_c = beta[:, L_BETA + h:L_BETA + h + 1]
        gc_c = cum[:, L_GDN_G + h:L_GDN_G + h + 1]
        eg = jnp.exp(gc_c)
        decay = _masked_decay(gc_c, cum_t[L_GDN_G + h:L_GDN_G + h + 1, :], causal)
        kb, qb = _bf(kh), _bf(qh)
        bmat[h] = jnp.where(strict, -(_dot_nt(kb, kb) * decay * beta_c), 0.0)
        bb[h] = _bf(bmat[h])
        qkb[h] = _bf(jnp.where(causal, _dot_nt(qb, kb) * decay, 0.0))
        rhs[h] = jnp.concatenate([vh * beta_c, kh * (beta_c * eg)], axis=-1)
        qdec[h] = qh * eg
        gl = gc_c[last]
        kd_t[h] = _bf((kh * jnp.exp(stack([gl] * t_len) - gc_c)).T)
        egl[h] = jnp.exp(gl)
    p1 = {h: _dot(bb[h], bb[h]) for h in heads}
    nmat = {h: bmat[h] + p1[h] + _dot(bb[h], _bf(p1[h])) for h in heads}
    sol = {h: rhs[h] + _dot(_bf(nmat[h]), _bf(rhs[h])) for h in heads}
    acc = {}
    for h in heads:
        lhs = _bf(stack([sol[h][:, GDN_HEAD_DIM:], qdec[h]]))
        a = jnp.zeros((2 * rows, GDN_HEAD_DIM), F32)
        for b in range(nb):
            a = jnp.where(seq2 == b, _dot(lhs, _bf(sg_in[b, h])), a)
        acc[h] = a
    vn = {h: sol[h][:, :GDN_HEAD_DIM] - acc[h][:rows] for h in heads}
    o = {h: acc[h][rows:] + _dot(qkb[h], _bf(vn[h])) for h in heads}
    upd = {h: _dot(kd_t[h], _bf(block_diag(vn[h]))) for h in heads}
    for h in heads:
        c0 = h * GDN_HEAD_DIM
        store_y(c0, _gdn_finish(o[h], stack(tokens(C_ZG + c0, GDN_HEAD_DIM)), gnorm_ref))
        for b in range(nb):
            sg_out[b, h] = (sg_in[b, h] * egl[h][b:b + 1, :]
                            + upd[h][:, b * GDN_HEAD_DIM:(b + 1) * GDN_HEAD_DIM])

    lo_lanes = lax.broadcasted_iota(jnp.int32, (rows, 2 * SSD_HEAD_DIM), 1) < SSD_HEAD_DIM
    lo_rows = lax.broadcasted_iota(jnp.int32, (2 * SSD_HEAD_DIM, SSD_STATE), 0) < SSD_HEAD_DIM
    pairs = range(SSD_HEADS // 2)
    n_gp = SSD_HEADS_PER_GROUP // 2
    cgb, cb, bg_bd = {}, {}, {}
    for g in range(SSD_GROUPS):
        bg = conv(scv_in, C_XBC, D_SSD + g * SSD_STATE, SSD_STATE, scw_ref, scb_ref)
        cg = conv(scv_in, C_XBC, D_SSD + (SSD_GROUPS + g) * SSD_STATE, SSD_STATE, scw_ref, scb_ref)
        cgb[g] = _bf(cg)
        cb[g] = _dot_nt(cgb[g], _bf(bg))
        bg_bd[g] = _bf(block_diag(bg))
    xs2, a2, y, xin_t = {}, {}, {}, {}
    for p in pairs:
        g, c0 = p // n_gp, 2 * p * SSD_HEAD_DIM
        la, lb = L_SSD_DT + 2 * p, L_SSD_DT + 2 * p + 1
        xs2[p] = conv(scv_in, C_XBC, c0, 2 * SSD_HEAD_DIM, scw_ref, scb_ref)
        a2[p] = jnp.where(lo_lanes, cum[:, la:la + 1], cum[:, lb:lb + 1])
        xdt2 = xs2[p] * jnp.where(lo_lanes, sp[:, la:la + 1], sp[:, lb:lb + 1])
        seg_a = _masked_decay(cum[:, la:la + 1], cum_t[la:la + 1, :], causal)
        seg_b = _masked_decay(cum[:, lb:lb + 1], cum_t[lb:lb + 1, :], causal)
        y[p] = (_dot(_bf(cb[g] * seg_a), _bf(jnp.where(lo_lanes, xdt2, 0.0)))
                + _dot(_bf(cb[g] * seg_b), _bf(jnp.where(lo_lanes, 0.0, xdt2))))
        xin_t[p] = _bf((xdt2 * jnp.exp(stack([a2[p][last]] * t_len) - a2[p])).T)
    acc = {}
    for p in pairs:
        a = jnp.zeros((rows, 2 * SSD_HEAD_DIM), F32)
        for b in range(nb):
            state2 = ss_in[b, 2 * p:2 * p + 2].reshape(2 * SSD_HEAD_DIM, SSD_STATE)
            a = jnp.where(seq1 == b, _dot_nt(cgb[p // n_gp], _bf(state2)), a)
        acc[p] = a
    upd = {p: _dot(xin_t[p], bg_bd[p // n_gp]) for p in pairs}
    for p in pairs:
        la, lb = L_SSD_DT + 2 * p, L_SSD_DT + 2 * p + 1
        for b in range(nb):
            r = rows - nb + b
            elast = jnp.exp(jnp.where(lo_rows, cum[r:r + 1, la:la + 1], cum[r:r + 1, lb:lb + 1]))
            state2 = ss_in[b, 2 * p:2 * p + 2].reshape(2 * SSD_HEAD_DIM, SSD_STATE)
            ss_out[b, 2 * p:2 * p + 2] = (state2 * elast + upd[p][:, b * SSD_STATE:(b + 1) * SSD_STATE]).reshape(
                2, SSD_HEAD_DIM, SSD_STATE)
    for g in range(SSD_GROUPS):
        yz, ssq = {}, None
        for p in range(g * n_gp, (g + 1) * n_gp):
            c0 = 2 * p * SSD_HEAD_DIM
            z = stack(tokens(C_ZS + c0, 2 * SSD_HEAD_DIM))
            yt = y[p] + jnp.exp(a2[p]) * acc[p]
            yz[p] = (yt + sd_ref[0:1, c0:c0 + 2 * SSD_HEAD_DIM] * xs2[p]) * _silu(z)
            s = jnp.sum(yz[p] * yz[p], axis=-1, keepdims=True)
            ssq = s if ssq is None else ssq + s
        scale = lax.rsqrt(ssq * (1.0 / SSD_GROUP_DIM) + EPS)
        for p, t in yz.items():
            c0 = 2 * p * SSD_HEAD_DIM
            store_y(D_GDN + c0, t * scale * snorm_ref[0:1, c0:c0 + 2 * SSD_HEAD_DIM])

    for gi, win in enumerate(POOL_WINDOWS):
        sl = slice(gi * POOL_GROUP_DIM, (gi + 1) * POOL_GROUP_DIM)
        x = [pool_in[j, :, sl] for j in range(POOL_BUF)] + tokens(C_UP + gi * POOL_GROUP_DIM, POOL_GROUP_DIM)
        pooled = []
        for t in range(t_len):
            acc = x[POOL_BUF + t]
            for k in range(1, win):
                acc = acc + x[POOL_BUF + t - k]
            pooled.append(acc / float(min(win, pos0 + t + 1)) - x[POOL_BUF + t])
        mixed = _dot(_bf(stack(pooled)), pw_ref[gi])
        z = stack(tokens(C_ZP + gi * POOL_GROUP_DIM, POOL_GROUP_DIM))
        store_y(D_GDN + D_SSD + gi * POOL_GROUP_DIM, mixed * pscale_ref[0:1, sl] * _silu(z))

    for j in range(CONV_K - 1):
        t = t_len - (CONV_K - 1) + j
        gcv_out[j] = proj_ref[t, :, C_QKV:C_QKV + 3 * D_GDN]
        scv_out[j] = proj_ref[t, :, C_XBC:C_XBC + SSD_CONV_CH]
    for j in range(POOL_BUF):
        src = j + t_len
        pool_out[j] = pool_in[src] if src < POOL_BUF else proj_ref[src - POOL_BUF, :, C_UP:C_UP + D_POOL]


def _mixer_sample(proj, states, params, layer, n_seq, prev_outs):
    nb, t_len = SAMPLE_NB, SAMPLE_T
    assert CONV_K - 1 <= t_len and proj.shape == (t_len, n_seq, D_PACK)
    prev_outs = () if prev_outs is None else tuple(prev_outs)
    n_state = len(states)

    def seq_spec(s):
        if s.shape[1] == n_seq:
            tail = tuple(s.shape[2:])
            return pl.BlockSpec((None, nb) + tail, lambda i: (layer, i) + (0,) * len(tail))
        assert s.shape[2] == n_seq and s.ndim == 4
        return pl.BlockSpec((None, s.shape[1], nb, s.shape[3]), lambda i: (layer, 0, i, 0))

    return pl.pallas_call(
        functools.partial(_mixer_sample_kernel, n_alias=len(prev_outs), pos0=PAST_LEN),
        grid=(n_seq // nb,),
        in_specs=[pl.BlockSpec((t_len, nb, D_PACK), lambda i: (0, i, 0))]
        + [seq_spec(s) for s in states]
        + [pl.BlockSpec(memory_space=pl.ANY)] * len(prev_outs)
        + _param_specs(params, layer),
        out_specs=[pl.BlockSpec((t_len, nb, D_MODEL), lambda i: (0, i, 0))] + [seq_spec(s) for s in states],
        out_shape=[jax.ShapeDtypeStruct((t_len, n_seq, D_MODEL), F32)]
        + [jax.ShapeDtypeStruct(s.shape, F32) for s in states],
        input_output_aliases={1 + n_state + k: 1 + k for k in range(len(prev_outs))},
        compiler_params=pltpu.CompilerParams(
            dimension_semantics=("parallel",), vmem_limit_bytes=VMEM_LIMIT),
        name="mixer_sample",
    )(proj, *states, *prev_outs, *params)


def _gate_params(gdn_a_log, gdn_dt_bias, ssd_a_log, ssd_dt_bias):
    def lanes(gdn, ssd):
        z = jnp.zeros((DEPTH, LANE), F32)
        z = z.at[:, L_GDN_G:L_GDN_G + GDN_HEADS].set(gdn)
        return z.at[:, L_SSD_DT:L_SSD_DT + SSD_HEADS].set(ssd)

    rows = jnp.stack([lanes(gdn_a_log, ssd_a_log), lanes(gdn_dt_bias, ssd_dt_bias)], axis=1)
    return jnp.concatenate([rows, jnp.zeros((DEPTH, SUBLANE - 2, LANE), F32)], axis=1)


def kernel(x_prompt, x_sample, state_gdn, state_gdn_conv, state_ssd, state_ssd_conv, state_pool, norm_w, w_in, gdn_conv_w, gdn_a_log, gdn_dt_bias, gdn_norm_w, ssd_conv_w, ssd_conv_b, ssd_a_log, ssd_dt_bias, ssd_d, ssd_norm_w, pool_w, pool_scale, w_out, final_norm_w):
    batch, seq, _ = x_prompt.shape
    n_seq, t_len, _ = x_sample.shape
    assert t_len == SAMPLE_T and seq % PROMPT_TT == 0 and n_seq % SAMPLE_NB == 0

    w_in_p = _pack_w_in(jnp.swapaxes(w_in, 1, 2))
    params = (
        _gate_params(gdn_a_log, gdn_dt_bias, ssd_a_log, ssd_dt_bias),
        gdn_conv_w,
        gdn_norm_w[:, None, :],
        ssd_conv_w,
        ssd_conv_b[:, None, :],
        jnp.repeat(ssd_d, SSD_HEAD_DIM, axis=-1)[:, None, :],
        ssd_norm_w[:, None, :],
        pool_w.astype(BF16),
        pool_scale[:, None, :],
    )
    nw = norm_w[:, None, :]
    fw = final_norm_w[None, :]
    states = (state_gdn, jnp.swapaxes(state_gdn_conv, 1, 2), state_ssd, jnp.swapaxes(state_ssd_conv, 1, 2),
              jnp.swapaxes(state_pool, 1, 2))

    xp = x_prompt.reshape(batch * seq, D_MODEL)
    xs = jnp.swapaxes(x_sample, 0, 1).reshape(t_len * n_seq, D_MODEL)
    new_p = [[] for _ in range(5)]
    new_s = None
    for layer in range(DEPTH):
        final = fw if layer == DEPTH - 1 else None

        y_odd, y_even, gdn_p, ssd_p, gcv_p, scv_p, pool_p = _layer_prompt(xp, nw, w_in_p, params, layer, batch, seq)
        xp = _outproj((y_even, y_odd), w_out, xp, layer, 512, final)
        for acc, st in zip(new_p, (gdn_p, gcv_p, ssd_p, scv_p, pool_p)):
            acc.append(st)

        proj_s = _inproj(xs, nw, w_in_p, layer, n_seq * t_len).reshape(t_len, n_seq, D_PACK)
        y_s, *new_s = _mixer_sample(proj_s, states, params, layer, n_seq, new_s)
        xs = _outproj(y_s.reshape(t_len * n_seq, D_MODEL), w_out, xs, layer, n_seq * t_len, final)

    new_s = [a if a.shape[1] == n_seq else jnp.swapaxes(a, 1, 2) for a in new_s]
    y_prompt = xp.reshape(batch, seq, D_MODEL)
    y_sample = jnp.swapaxes(xs.reshape(t_len, n_seq, D_MODEL), 0, 1)
    return (y_prompt, y_sample) + tuple(jnp.stack(a) for a in new_p) + tuple(new_s)
```

```python
import functools

import jax
import jax.numpy as jnp
import numpy as np
from jax import lax
from jax.experimental import pallas as pl
from jax.experimental.pallas import tpu as pltpu

F32 = jnp.float32
BF16 = jnp.bfloat16

D_MODEL = 2048
DEPTH = 4
PAST_LEN = 16384
D_GDN = 768
D_SSD = 768
D_POOL = 512
CONV_K = 4
GDN_HEAD_DIM = 128
GDN_HEADS = 6
GDN_CHUNK = 64
SSD_HEAD_DIM = 64
SSD_HEADS = 12
SSD_STATE = 128
SSD_GROUPS = 2
SSD_HEADS_PER_GROUP = 6
SSD_GROUP_DIM = D_SSD // SSD_GROUPS
SSD_CHUNK = 128
SSD_CONV_CH = 1280
POOL_WINDOWS = (2, 4, 8, 16)
POOL_GROUP_DIM = 128
POOL_BUF = 15
IN_SIZES = (3 * D_GDN, D_GDN, GDN_HEADS, GDN_HEADS, D_SSD, SSD_CONV_CH, SSD_HEADS, D_POOL, D_POOL)
EPS = 1e-6

C_QKV = 0
C_ZG = 2304
C_ZS = 3072
C_XBC = 3840
C_UP = 5120
C_ZP = 5632
C_SM = 6144
D_PACK = 6400
L_BETA = 0
L_GDN_G = 6
L_SSD_DT = 16

LANE = 128
SUBLANE = 8
MXU_WIDTH = 256
VMEM_LIMIT = 48 * 1024 * 1024
VMEM_LIMIT_PROJ = 56 * 1024 * 1024

PROJ_TN = 5 * MXU_WIDTH
PACK_TK = 256
PROMPT_TT = 128
SAMPLE_NB = 8
SAMPLE_T = 4


def _dot(a, b):
    return jnp.dot(a, b, preferred_element_type=F32)


def _dot_nt(a, b):
    return lax.dot_general(a, b, (((1,), (1,)), ((), ())), preferred_element_type=F32)


def _dot_tn(a, b):
    return lax.dot_general(a, b, (((0,), (0,)), ((), ())), preferred_element_type=F32)


def _sigmoid(x):
    return 0.5 + 0.5 * jnp.tanh(0.5 * x)


def _silu(x):
    hx = 0.5 * x
    return hx + hx * jnp.tanh(hx)


def _softplus(x):
    return jnp.maximum(x, 0.0) + jnp.log(1.0 + jnp.exp(-jnp.abs(x)))


def _bf(x):
    return x.astype(BF16)


def _inproj_kernel(x_ref, nw_ref, w_ref, o_ref, h_ref):
    @pl.when(pl.program_id(1) == 0)
    def _():
        x = x_ref[...]
        ms = jnp.mean(x * x, axis=-1, keepdims=True)
        h_ref[...] = _bf(x * lax.rsqrt(ms + EPS) * nw_ref[...])

    o_ref[...] = _dot(h_ref[...], w_ref[...])


def _inproj(x, nw, w_packed, layer, tm):
    n = x.shape[0]
    return pl.pallas_call(
        _inproj_kernel,
        grid=(n // tm, D_PACK // PROJ_TN),
        in_specs=[
            pl.BlockSpec((tm, D_MODEL), lambda i, j: (i, 0)),
            pl.BlockSpec((None, 1, D_MODEL), lambda i, j: (layer, 0, 0)),
            pl.BlockSpec((None, D_MODEL, PROJ_TN), lambda i, j: (layer, 0, j)),
        ],
        out_specs=pl.BlockSpec((tm, PROJ_TN), lambda i, j: (i, j)),
        out_shape=jax.ShapeDtypeStruct((n, D_PACK), F32),
        scratch_shapes=[pltpu.VMEM((tm, D_MODEL), BF16)],
        compiler_params=pltpu.CompilerParams(
            dimension_semantics=("parallel", "arbitrary"), vmem_limit_bytes=VMEM_LIMIT_PROJ),
        name="inproj",
    )(x, nw, w_packed)


def _pack_kernel(w_ref, o_ref):
    offs = np.cumsum((0,) + IN_SIZES)
    tk = o_ref.shape[0]

    for seg, c_out in ((0, C_QKV), (1, C_ZG), (4, C_ZS), (5, C_XBC), (7, C_UP), (8, C_ZP)):
        o_ref[:, c_out:c_out + IN_SIZES[seg]] = _bf(w_ref[offs[seg]:offs[seg + 1], :].T)
    assert offs[3] == offs[2] + GDN_HEADS and L_GDN_G == L_BETA + GDN_HEADS and L_BETA == 0
    assert offs[2] % SUBLANE == 0 and offs[2] + LANE <= offs[-1]
    lane = lax.broadcasted_iota(jnp.int32, (tk, LANE), 1)
    gates = w_ref[offs[2]:offs[2] + LANE, :].T
    dt_row0 = offs[6] - offs[6] % SUBLANE
    dt_lane0 = offs[6] - dt_row0
    assert dt_row0 + LANE <= offs[-1] and dt_lane0 <= L_SSD_DT
    dts = pltpu.roll(w_ref[dt_row0:dt_row0 + LANE, :].T, L_SSD_DT - dt_lane0, 1)
    small = jnp.where(lane < 2 * GDN_HEADS, gates,
                      jnp.where((lane >= L_SSD_DT) & (lane < L_SSD_DT + SSD_HEADS), dts, 0.0))
    o_ref[:, C_SM:C_SM + LANE] = _bf(small)
    o_ref[:, C_SM + LANE:D_PACK] = jnp.zeros((tk, D_PACK - C_SM - LANE), BF16)


def _pack_w_in(w_in_t):
    d_in = w_in_t.shape[1]
    return pl.pallas_call(
        _pack_kernel,
        grid=(DEPTH, D_MODEL // PACK_TK),
        in_specs=[pl.BlockSpec((None, d_in, PACK_TK), lambda l, i: (l, 0, i))],
        out_specs=pl.BlockSpec((None, PACK_TK, D_PACK), lambda l, i: (l, i, 0)),
        out_shape=jax.ShapeDtypeStruct((DEPTH, D_MODEL, D_PACK), BF16),
        compiler_params=pltpu.CompilerParams(
            dimension_semantics=("parallel", "parallel"), vmem_limit_bytes=VMEM_LIMIT),
        name="pack_w_in",
    )(w_in_t)


def _outproj_kernel(*refs, n_y, final):
    y_refs, (w32_ref, x_ref), rest = refs[:n_y], refs[n_y:n_y + 2], refs[n_y + 2:]
    o_ref, w_ref = rest[-2], rest[-1]

    @pl.when(pl.program_id(0) == 0)
    def _():
        w_ref[...] = _bf(w32_ref[...])

    if n_y == 1:
        y = y_refs[0][...]
    else:
        tt = PROMPT_TT
        y = jnp.concatenate([r[k * tt:(k + 1) * tt, :] for k in range(y_refs[0].shape[0] // tt) for r in y_refs],
                            axis=0)
    x = x_ref[...] + _dot(_bf(y), w_ref[...])
    if final:
        ms = jnp.mean(x * x, axis=-1, keepdims=True)
        x = x * lax.rsqrt(ms + EPS) * rest[0][...]
    o_ref[...] = x


def _outproj(ys, w_out, x, layer, tm, final_w=None):
    n = x.shape[0]
    ys = tuple(ys) if isinstance(ys, (tuple, list)) else (ys,)
    in_specs = [pl.BlockSpec((tm // len(ys), D_MODEL), lambda i: (i, 0)) for _ in ys] + [
        pl.BlockSpec((None, D_MODEL, D_MODEL), lambda i: (layer, 0, 0), pipeline_mode=pl.Buffered(1)),
        pl.BlockSpec((tm, D_MODEL), lambda i: (i, 0)),
    ]
    args = [*ys, w_out, x]
    if final_w is not None:
        in_specs.append(pl.BlockSpec((1, D_MODEL), lambda i: (0, 0)))
        args.append(final_w)
    return pl.pallas_call(
        functools.partial(_outproj_kernel, n_y=len(ys), final=final_w is not None),
        grid=(n // tm,),
        in_specs=in_specs,
        out_specs=pl.BlockSpec((tm, D_MODEL), lambda i: (i, 0)),
        out_shape=jax.ShapeDtypeStruct((n, D_MODEL), F32),
        scratch_shapes=[pltpu.VMEM((D_MODEL, D_MODEL), BF16)],
        compiler_params=pltpu.CompilerParams(
            dimension_semantics=("arbitrary",), vmem_limit_bytes=VMEM_LIMIT_PROJ),
        name="outproj",
    )(*args)


def _conv_silu_tile(ext_ref, hist, c0, width, w_ref, wc0, b_ref=None):
    e = ext_ref[:, c0:c0 + width]
    acc = e * w_ref[0:1, wc0:wc0 + width]
    for j in range(1, CONV_K):
        acc = pltpu.roll(acc, 1, 0) + e * w_ref[j:j + 1, wc0:wc0 + width]
    acc = acc[hist:]
    if b_ref is not None:
        acc = acc + b_ref[0:1, wc0:wc0 + width]
    return _silu(acc)


def _l2norm(x):
    return x * lax.rsqrt(jnp.sum(x * x, axis=-1, keepdims=True) + EPS)


def _seg_cumsum(x, seg_len, row_in_seg):
    s = 1
    while s < seg_len:
        x = x + jnp.where(row_in_seg >= s, pltpu.roll(x, s, 0), 0.0)
        s *= 2
    return x


def _gates(small, gp_ref):
    sp = _softplus(small + gp_ref[1:2, :])
    gd = -jnp.exp(gp_ref[0:1, :]) * sp
    return _sigmoid(small), sp, gd


def _masked_decay(col, row, causal):
    return jnp.where(causal, jnp.exp(jnp.where(causal, col - row, 0.0)), 0.0)


def _gdn_finish(o, z, gnorm_ref):
    on = o * lax.rsqrt(jnp.mean(o * o, axis=-1, keepdims=True) + EPS) * gnorm_ref[0:1, :]
    return on * _silu(z)


HIST = SUBLANE
PHIST = 2 * SUBLANE


def _reset_prompt_state(sg_ref, ss_ref, eq_ref, ex_ref, ep_ref):
    sg_ref[...] = jnp.zeros(sg_ref.shape, F32)
    ss_ref[...] = jnp.zeros(ss_ref.shape, F32)
    eq_ref[0:HIST, :] = jnp.zeros((HIST, eq_ref.shape[1]), F32)
    ex_ref[0:HIST, :] = jnp.zeros((HIST, ex_ref.shape[1]), F32)
    ep_ref[0:PHIST, :] = jnp.zeros((PHIST, D_POOL), F32)


def _mixer_tile(proj_ref, y_ref, i, pos0, params, sg_ref, ss_ref, eq_ref, ex_ref, ep_ref):
    gp_ref, gcw_ref, gnorm_ref, scw_ref, scb_ref, sd_ref, snorm_ref, pw_ref, pscale_ref = params
    tt = PROMPT_TT
    n_chunks = tt // GDN_CHUNK
    hist, phist = HIST, PHIST

    eq_ref[hist:hist + tt, :] = proj_ref[:, C_QKV:C_QKV + 3 * D_GDN]
    ex_ref[hist:hist + tt, :] = proj_ref[:, C_XBC:C_XBC + SSD_CONV_CH]
    ep_ref[phist:phist + tt, :] = proj_ref[:, C_UP:C_UP + D_POOL]

    beta, sp, gd = _gates(proj_ref[:, C_SM:C_SM + LANE], gp_ref)
    row = lax.broadcasted_iota(jnp.int32, (tt, LANE), 0)
    gc = _seg_cumsum(gd, GDN_CHUNK, row & (GDN_CHUNK - 1))
    ac = _seg_cumsum(gd, SSD_CHUNK, row)

    def store_y(c0, val, r0=0):
        y_ref[r0:r0 + val.shape[0], c0:c0 + val.shape[1]] = val.astype(y_ref.dtype)

    ci = lax.broadcasted_iota(jnp.int32, (GDN_CHUNK, GDN_CHUNK), 0)
    cj = lax.broadcasted_iota(jnp.int32, (GDN_CHUNK, GDN_CHUNK), 1)
    causal_g, strict_g = ci >= cj, ci > cj
    gc_t = [gc[c * GDN_CHUNK:(c + 1) * GDN_CHUNK].T for c in range(n_chunks)]
    conv_q = functools.partial(_conv_silu_tile, eq_ref, hist)

    keys = [(c, h) for c in range(n_chunks) for h in range(GDN_HEADS)]
    nmat, pmat, rhs, qkb, qdec, kdec_t, glast = {}, {}, {}, {}, {}, {}, {}
    for h in range(GDN_HEADS):
        c0 = h * GDN_HEAD_DIM
        q_all = _l2norm(conv_q(c0, GDN_HEAD_DIM, gcw_ref, c0)) * (GDN_HEAD_DIM ** -0.5)
        k_all = _l2norm(conv_q(D_GDN + c0, GDN_HEAD_DIM, gcw_ref, D_GDN + c0))
        v_all = conv_q(2 * D_GDN + c0, GDN_HEAD_DIM, gcw_ref, 2 * D_GDN + c0)
        for c in range(n_chunks):
            rs = slice(c * GDN_CHUNK, (c + 1) * GDN_CHUNK)
            qh, kh, vh = q_all[rs], k_all[rs], v_all[rs]
            beta_c = beta[rs, L_BETA + h:L_BETA + h + 1]
            gc_c = gc[rs, L_GDN_G + h:L_GDN_G + h + 1]
            gc_r = gc_t[c][L_GDN_G + h:L_GDN_G + h + 1, :]
            gl = gc_c[GDN_CHUNK - 1:GDN_CHUNK, :]
            eg = jnp.exp(gc_c)
            decay = _masked_decay(gc_c, gc_r, causal_g)
            kb, qb = _bf(kh), _bf(qh)
            nmat[c, h] = jnp.where(strict_g, -(_dot_nt(kb, kb) * decay * beta_c), 0.0)
            qkb[c, h] = _bf(jnp.where(causal_g, _dot_nt(qb, kb) * decay, 0.0))
            rhs[c, h] = jnp.concatenate([vh * beta_c, kh * (beta_c * eg)], axis=-1)
            qdec[c, h] = qh * eg
            kdec_t[c, h] = _bf((kh * jnp.exp(gl - gc_c)).T)
            glast[c, h] = jnp.exp(gl)

    for key in keys:
        bb = _bf(nmat[key])
        pmat[key] = _dot(bb, bb)
    for _ in range(4):
        for key in keys:
            pb = _bf(pmat[key])
            r = _dot(jnp.concatenate([_bf(nmat[key]), pb], axis=0), pb)
            nmat[key] = nmat[key] + pmat[key] + r[:GDN_CHUNK]
            pmat[key] = r[GDN_CHUNK:]
    sol = {}
    for key in keys:
        nmat[key] = nmat[key] + pmat[key] + _dot(_bf(nmat[key]), _bf(pmat[key]))
    for key in keys:
        sol[key] = rhs[key] + _dot(_bf(nmat[key]), _bf(rhs[key]))

    for c in range(n_chunks):
        r0 = c * GDN_CHUNK
        for h in range(GDN_HEADS):
            key = (c, h)
            state = sg_ref[h]
            u, w = sol[key][:, :GDN_HEAD_DIM], sol[key][:, GDN_HEAD_DIM:]
            r1 = _dot(_bf(jnp.concatenate([w, qdec[key]], axis=0)), _bf(state))
            vnb = _bf(u - r1[:GDN_CHUNK])
            r2 = _dot(jnp.concatenate([qkb[key], kdec_t[key]], axis=0), vnb)
            sg_ref[h] = state * glast[key] + r2[GDN_CHUNK:]
            o = r1[GDN_CHUNK:] + r2[:GDN_CHUNK]
            c0 = h * GDN_HEAD_DIM
            z = proj_ref[r0:r0 + GDN_CHUNK, C_ZG + c0:C_ZG + c0 + GDN_HEAD_DIM]
            store_y(c0, _gdn_finish(o, z, gnorm_ref), r0)

    si = lax.broadcasted_iota(jnp.int32, (SSD_CHUNK, SSD_CHUNK), 0)
    sj = lax.broadcasted_iota(jnp.int32, (SSD_CHUNK, SSD_CHUNK), 1)
    causal_s = si >= sj
    lo_lanes = sj < SSD_HEAD_DIM
    lo_rows = si < SSD_HEAD_DIM
    ac_t = ac.T
    conv_x = functools.partial(_conv_silu_tile, ex_ref, hist)
    for g in range(SSD_GROUPS):
        bgb = _bf(conv_x(D_SSD + g * SSD_STATE, SSD_STATE, scw_ref, D_SSD + g * SSD_STATE, scb_ref))
        cgb = _bf(conv_x(D_SSD + (SSD_GROUPS + g) * SSD_STATE, SSD_STATE, scw_ref,
                         D_SSD + (SSD_GROUPS + g) * SSD_STATE, scb_ref))
        cb = _dot_nt(cgb, bgb)
        yz, ssq = [], None
        for pair in range(SSD_HEADS_PER_GROUP // 2):
            hd0 = g * SSD_HEADS_PER_GROUP + 2 * pair
            c0 = hd0 * SSD_HEAD_DIM
            la, lb = L_SSD_DT + hd0, L_SSD_DT + hd0 + 1
            xs2 = conv_x(c0, 2 * SSD_HEAD_DIM, scw_ref, c0, scb_ref)
            a2 = jnp.where(lo_lanes, ac[:, la:la + 1], ac[:, lb:lb + 1])
            xdt2 = xs2 * jnp.where(lo_lanes, sp[:, la:la + 1], sp[:, lb:lb + 1])
            alast2 = a2[tt - 1:tt, :]
            state2 = ss_ref[hd0:hd0 + 2].reshape(2 * SSD_HEAD_DIM, SSD_STATE)
            y = jnp.exp(a2) * _dot_nt(cgb, _bf(state2))
            seg_a = _masked_decay(ac[:, la:la + 1], ac_t[la:la + 1, :], causal_s)
            seg_b = _masked_decay(ac[:, lb:lb + 1], ac_t[lb:lb + 1, :], causal_s)
            y = y + _dot(_bf(cb * seg_a), _bf(jnp.where(lo_lanes, xdt2, 0.0)))
            y = y + _dot(_bf(cb * seg_b), _bf(jnp.where(lo_lanes, 0.0, xdt2)))
            upd = _dot_tn(_bf(xdt2 * jnp.exp(alast2 - a2)), bgb)
            elast = jnp.exp(jnp.where(lo_rows, ac[tt - 1:tt, la:la + 1], ac[tt - 1:tt, lb:lb + 1]))
            ss_ref[hd0:hd0 + 2] = (state2 * elast + upd).reshape(2, SSD_HEAD_DIM, SSD_STATE)
            z = proj_ref[:, C_ZS + c0:C_ZS + c0 + 2 * SSD_HEAD_DIM]
            t = (y + sd_ref[0:1, c0:c0 + 2 * SSD_HEAD_DIM] * xs2) * _silu(z)
            yz.append(t)
            s = jnp.sum(t * t, axis=-1, keepdims=True)
            ssq = s if ssq is None else ssq + s
        scale = lax.rsqrt(ssq * (1.0 / SSD_GROUP_DIM) + EPS)
        for pair, t in enumerate(yz):
            c0 = (g * SSD_HEADS_PER_GROUP + 2 * pair) * SSD_HEAD_DIM
            store_y(D_GDN + c0, t * scale * snorm_ref[0:1, c0:c0 + 2 * SSD_HEAD_DIM])

    pos = pos0 + i * tt + lax.broadcasted_iota(jnp.int32, (tt, 1), 0)
    for gi, win in enumerate(POOL_WINDOWS):
        sl = slice(gi * POOL_GROUP_DIM, (gi + 1) * POOL_GROUP_DIM)
        e = ep_ref[:, sl]
        acc, k = e, 1
        while k < win:
            acc = acc + pltpu.roll(acc, k, 0)
            k *= 2
        cnt = jnp.minimum(win, pos + 1).astype(F32)
        pooled = acc[phist:] / cnt - e[phist:]
        mixed = _dot(_bf(pooled), pw_ref[gi])
        z = proj_ref[:, C_ZP + gi * POOL_GROUP_DIM:C_ZP + (gi + 1) * POOL_GROUP_DIM]
        store_y(D_GDN + D_SSD + gi * POOL_GROUP_DIM, mixed * pscale_ref[0:1, sl] * _silu(z))

    eq_ref[0:hist, :] = eq_ref[tt:tt + hist, :]
    ex_ref[0:hist, :] = ex_ref[tt:tt + hist, :]
    ep_ref[0:phist, :] = ep_ref[tt:tt + phist, :]


def _layer_spec(shape):
    nd = len(shape)
    return lambda layer: pl.BlockSpec((None,) + tuple(shape[1:]), lambda *_: (layer,) + (0,) * (nd - 1))


def _param_specs(params, layer):
    return [_layer_spec(p.shape)(layer) for p in params]


def _layer_prompt_kernel(x_ref, nw_ref, w_ref, *refs, tiles_per_seq, n_steps, pos0):
    params = refs[:9]
    (y_odd_ref, y_even_ref, sg_out, ss_out, gcv_out, scv_out, pool_out,
     p_even, p_odd, sg_ref, ss_ref, eq_ref, ex_ref, ep_ref) = refs[9:]
    tt = PROMPT_TT
    g = pl.program_id(0)
    state = (sg_ref, ss_ref, eq_ref, ex_ref, ep_ref)

    def project(rows, p_ref):
        x = x_ref[rows, :]
        ms = jnp.mean(x * x, axis=-1, keepdims=True)
        p_ref[...] = _dot(_bf(x * lax.rsqrt(ms + EPS) * nw_ref[...]), w_ref[...])

    @pl.when(g == 0)
    def _():
        project(slice(0, tt), p_even)

    @pl.when(g > 0)
    def _():
        project(slice(0, tt), p_even)
        _mixer_tile(p_odd, y_odd_ref, lax.rem(2 * g - 1, tiles_per_seq), pos0, params, *state)
        sg_out[0] = sg_ref[...]
        ss_out[0] = ss_ref[...]
        gcv_out[0] = eq_ref[HIST + tt - (CONV_K - 1):HIST + tt, :]
        scv_out[0] = ex_ref[HIST + tt - (CONV_K - 1):HIST + tt, :]
        pool_out[0] = ep_ref[PHIST + tt - POOL_BUF:PHIST + tt, :]

    @pl.when(g < n_steps - 1)
    def _():
        i_even = lax.rem(2 * g, tiles_per_seq)

        @pl.when(i_even == 0)
        def _():
            _reset_prompt_state(*state)

        project(slice(tt, 2 * tt), p_odd)
        _mixer_tile(p_even, y_even_ref, i_even, pos0, params, *state)


def _layer_prompt(x, nw, w_packed, params, layer, batch, seq):
    tt = PROMPT_TT
    tiles_per_seq = seq // tt
    n_tiles = batch * tiles_per_seq
    assert tiles_per_seq % 2 == 0
    n_steps = n_tiles // 2 + 1

    def seq_of_odd_tile(g):
        return jnp.maximum(2 * g - 1, 0) // tiles_per_seq

    def state_spec(tail):
        return pl.BlockSpec((1,) + tail, lambda g: (seq_of_odd_tile(g),) + (0,) * len(tail))

    tails = ((GDN_HEADS, GDN_HEAD_DIM, GDN_HEAD_DIM), (SSD_HEADS, SSD_HEAD_DIM, SSD_STATE),
             (CONV_K - 1, 3 * D_GDN), (CONV_K - 1, SSD_CONV_CH), (POOL_BUF, D_POOL))
    return pl.pallas_call(
        functools.partial(_layer_prompt_kernel, tiles_per_seq=tiles_per_seq, n_steps=n_steps, pos0=0),
        grid=(n_steps,),
        in_specs=[
            pl.BlockSpec((2 * tt, D_MODEL), lambda g: (jnp.minimum(g, n_steps - 2), 0)),
            pl.BlockSpec((None, 1, D_MODEL), lambda g: (layer, 0, 0)),
            pl.BlockSpec((None, D_MODEL, D_PACK), lambda g: (layer, 0, 0), pipeline_mode=pl.Buffered(1)),
        ] + _param_specs(params, layer),
        out_specs=[
            pl.BlockSpec((tt, D_MODEL), lambda g: (jnp.maximum(g - 1, 0), 0)),
            pl.BlockSpec((tt, D_MODEL), lambda g: (jnp.minimum(g, n_steps - 2), 0)),
        ] + [state_spec(t) for t in tails],
        out_shape=[jax.ShapeDtypeStruct((n_tiles // 2 * tt, D_MODEL), BF16)] * 2
        + [jax.ShapeDtypeStruct((batch,) + t, F32) for t in tails],
        scratch_shapes=[
            pltpu.VMEM((tt, D_PACK), F32),
            pltpu.VMEM((tt, D_PACK), F32),
            pltpu.VMEM(tails[0], F32),
            pltpu.VMEM(tails[1], F32),
            pltpu.VMEM((HIST + tt, 3 * D_GDN), F32),
            pltpu.VMEM((HIST + tt, SSD_CONV_CH), F32),
            pltpu.VMEM((PHIST + tt, D_POOL), F32),
        ],
        compiler_params=pltpu.CompilerParams(
            dimension_semantics=("arbitrary",), vmem_limit_bytes=VMEM_LIMIT_PROJ),
        name="layer_prompt",
    )(x, nw, w_packed, *params)


def _mixer_sample_kernel(*refs, n_alias, pos0):
    proj_ref, sg_in, gcv_in, ss_in, scv_in, pool_in = refs[:6]
    (gp_ref, gcw_ref, gnorm_ref, scw_ref, scb_ref, sd_ref, snorm_ref, pw_ref, pscale_ref,
     y_ref, sg_out, gcv_out, ss_out, scv_out, pool_out) = refs[6 + n_alias:]
    nb, t_len = SAMPLE_NB, SAMPLE_T
    assert nb == SUBLANE
    rows = nb * t_len
    last = slice(rows - nb, rows)

    def tokens(c0, width):
        return [proj_ref[t, :, c0:c0 + width] for t in range(t_len)]

    def stack(parts):
        return jnp.concatenate(parts, axis=0)

    def conv(hist_ref, col0, cw, width, w_ref, b_ref=None):
        u = [hist_ref[j, :, cw:cw + width] for j in range(CONV_K - 1)] + tokens(col0 + cw, width)
        outs = []
        for t in range(t_len):
            acc = u[t] * w_ref[0:1, cw:cw + width]
            for j in range(1, CONV_K):
                acc = acc + u[t + j] * w_ref[j:j + 1, cw:cw + width]
            if b_ref is not None:
                acc = acc + b_ref[0:1, cw:cw + width]
            outs.append(_silu(acc))
        return stack(outs)

    beta, sp, gd = _gates(stack(tokens(C_SM, LANE)), gp_ref)
    parts = [gd[0:nb]]
    for t in range(1, t_len):
        parts.append(parts[-1] + gd[t * nb:(t + 1) * nb])
    cum = stack(parts)
    cum_t = cum.T

    ri = lax.broadcasted_iota(jnp.int32, (rows, rows), 0)
    rj = lax.broadcasted_iota(jnp.int32, (rows, rows), 1)
    same = (ri & (nb - 1)) == (rj & (nb - 1))
    causal, strict = same & (ri >= rj), same & (ri > rj)
    seq1 = lax.broadcasted_iota(jnp.int32, (rows, 1), 0) & (nb - 1)
    seq2 = lax.broadcasted_iota(jnp.int32, (2 * rows, 1), 0) & (nb - 1)

    def block_diag(x):
        return jnp.concatenate([jnp.where(seq1 == b, x, 0.0) for b in range(nb)], axis=1)

    def store_y(c0, val):
        for t in range(t_len):
            y_ref[t, :, c0:c0 + val.shape[1]] = val[t * nb:(t + 1) * nb]

    heads = range(GDN_HEADS)
    bmat, bb, rhs, qkb, qdec, kd_t, egl = {}, {}, {}, {}, {}, {}, {}
    for h in heads:
        c0 = h * GDN_HEAD_DIM
        qh = _l2norm(conv(gcv_in, C_QKV, c0, GDN_HEAD_DIM, gcw_ref)) * (GDN_HEAD_DIM ** -0.5)
        kh = _l2norm(conv(gcv_in, C_QKV, D_GDN + c0, GDN_HEAD_DIM, gcw_ref))
        vh = conv(gcv_in, C_QKV, 2 * D_GDN + c0, GDN_HEAD_DIM, gcw_ref)
        beta_c = beta[:, L_BETA + h:L_BETA + h + 1]
        gc_c = cum[:, L_GDN_G + h:L_GDN_G + h + 1]
        eg = jnp.exp(gc_c)
        decay = _masked_decay(gc_c, cum_t[L_GDN_G + h:L_GDN_G + h + 1, :], causal)
        kb, qb = _bf(kh), _bf(qh)
        bmat[h] = jnp.where(strict, -(_dot_nt(kb, kb) * decay * beta_c), 0.0)
        bb[h] = _bf(bmat[h])
        qkb[h] = _bf(jnp.where(causal, _dot_nt(qb, kb) * decay, 0.0))
        rhs[h] = jnp.concatenate([vh * beta_c, kh * (beta_c * eg)], axis=-1)
        qdec[h] = qh * eg
        gl = gc_c[last]
        kd_t[h] = _bf((kh * jnp.exp(stack([gl] * t_len) - gc_c)).T)
        egl[h] = jnp.exp(gl)
    p1 = {h: _dot(bb[h], bb[h]) for h in heads}
    nmat = {h: bmat[h] + p1[h] + _dot(bb[h], _bf(p1[h])) for h in heads}
    sol = {h: rhs[h] + _dot(_bf(nmat[h]), _bf(rhs[h])) for h in heads}
    acc = {}
    for h in heads:
        lhs = _bf(stack([sol[h][:, GDN_HEAD_DIM:], qdec[h]]))
        a = jnp.zeros((2 * rows, GDN_HEAD_DIM), F32)
        for b in range(nb):
            a = jnp.where(seq2 == b, _dot(lhs, _bf(sg_in[b, h])), a)
        acc[h] = a
    vn = {h: sol[h][:, :GDN_HEAD_DIM] - acc[h][:rows] for h in heads}
    o = {h: acc[h][rows:] + _dot(qkb[h], _bf(vn[h])) for h in heads}
    upd = {h: _dot(kd_t[h], _bf(block_diag(vn[h]))) for h in heads}
    for h in heads:
        c0 = h * GDN_HEAD_DIM
        store_y(c0, _gdn_finish(o[h], stack(tokens(C_ZG + c0, GDN_HEAD_DIM)), gnorm_ref))
        for b in range(nb):
            sg_out[b, h] = (sg_in[b, h] * egl[h][b:b + 1, :]
                            + upd[h][:, b * GDN_HEAD_DIM:(b + 1) * GDN_HEAD_DIM])

    lo_lanes = lax.broadcasted_iota(jnp.int32, (rows, 2 * SSD_HEAD_DIM), 1) < SSD_HEAD_DIM
    lo_rows = lax.broadcasted_iota(jnp.int32, (2 * SSD_HEAD_DIM, SSD_STATE), 0) < SSD_HEAD_DIM
    pairs = range(SSD_HEADS // 2)
    n_gp = SSD_HEADS_PER_GROUP // 2
    cgb, cb, bg_bd = {}, {}, {}
    for g in range(SSD_GROUPS):
        bg = conv(scv_in, C_XBC, D_SSD + g * SSD_STATE, SSD_STATE, scw_ref, scb_ref)
        cg = conv(scv_in, C_XBC, D_SSD + (SSD_GROUPS + g) * SSD_STATE, SSD_STATE, scw_ref, scb_ref)
        cgb[g] = _bf(cg)
        cb[g] = _dot_nt(cgb[g], _bf(bg))
        bg_bd[g] = _bf(block_diag(bg))
    xs2, a2, y, xin_t = {}, {}, {}, {}
    for p in pairs:
        g, c0 = p // n_gp, 2 * p * SSD_HEAD_DIM
        la, lb = L_SSD_DT + 2 * p, L_SSD_DT + 2 * p + 1
        xs2[p] = conv(scv_in, C_XBC, c0, 2 * SSD_HEAD_DIM, scw_ref, scb_ref)
        a2[p] = jnp.where(lo_lanes, cum[:, la:la + 1], cum[:, lb:lb + 1])
        xdt2 = xs2[p] * jnp.where(lo_lanes, sp[:, la:la + 1], sp[:, lb:lb + 1])
        seg_a = _masked_decay(cum[:, la:la + 1], cum_t[la:la + 1, :], causal)
        seg_b = _masked_decay(cum[:, lb:lb + 1], cum_t[lb:lb + 1, :], causal)
        y[p] = (_dot(_bf(cb[g] * seg_a), _bf(jnp.where(lo_lanes, xdt2, 0.0)))
                + _dot(_bf(cb[g] * seg_b), _bf(jnp.where(lo_lanes, 0.0, xdt2))))
        xin_t[p] = _bf((xdt2 * jnp.exp(stack([a2[p][last]] * t_len) - a2[p])).T)
    acc = {}
    for p in pairs:
        a = jnp.zeros((rows, 2 * SSD_HEAD_DIM), F32)
        for b in range(nb):
            state2 = ss_in[b, 2 * p:2 * p + 2].reshape(2 * SSD_HEAD_DIM, SSD_STATE)
            a = jnp.where(seq1 == b, _dot_nt(cgb[p // n_gp], _bf(state2)), a)
        acc[p] = a
    upd = {p: _dot(xin_t[p], bg_bd[p // n_gp]) for p in pairs}
    for p in pairs:
        la, lb = L_SSD_DT + 2 * p, L_SSD_DT + 2 * p + 1
        for b in range(nb):
            r = rows - nb + b
            elast = jnp.exp(jnp.where(lo_rows, cum[r:r + 1, la:la + 1], cum[r:r + 1, lb:lb + 1]))
            state2 = ss_in[b, 2 * p:2 * p + 2].reshape(2 * SSD_HEAD_DIM, SSD_STATE)
            ss_out[b, 2 * p:2 * p + 2] = (state2 * elast + upd[p][:, b * SSD_STATE:(b + 1) * SSD_STATE]).reshape(
                2, SSD_HEAD_DIM, SSD_STATE)
    for g in range(SSD_GROUPS):
        yz, ssq = {}, None
        for p in range(g * n_gp, (g + 1) * n_gp):
            c0 = 2 * p * SSD_HEAD_DIM
            z = stack(tokens(C_ZS + c0, 2 * SSD_HEAD_DIM))
            yt = y[p] + jnp.exp(a2[p]) * acc[p]
            yz[p] = (yt + sd_ref[0:1, c0:c0 + 2 * SSD_HEAD_DIM] * xs2[p]) * _silu(z)
            s = jnp.sum(yz[p] * yz[p], axis=-1, keepdims=True)
            ssq = s if ssq is None else ssq + s
        scale = lax.rsqrt(ssq * (1.0 / SSD_GROUP_DIM) + EPS)
        for p, t in yz.items():
            c0 = 2 * p * SSD_HEAD_DIM
            store_y(D_GDN + c0, t * scale * snorm_ref[0:1, c0:c0 + 2 * SSD_HEAD_DIM])

    for gi, win in enumerate(POOL_WINDOWS):
        sl = slice(gi * POOL_GROUP_DIM, (gi + 1) * POOL_GROUP_DIM)
        x = [pool_in[j, :, sl] for j in range(POOL_BUF)] + tokens(C_UP + gi * POOL_GROUP_DIM, POOL_GROUP_DIM)
        pooled = []
        for t in range(t_len):
            acc = x[POOL_BUF + t]
            for k in range(1, win):
                acc = acc + x[POOL_BUF + t - k]
            pooled.append(acc / float(min(win, pos0 + t + 1)) - x[POOL_BUF + t])
        mixed = _dot(_bf(stack(pooled)), pw_ref[gi])
        z = stack(tokens(C_ZP + gi * POOL_GROUP_DIM, POOL_GROUP_DIM))
        store_y(D_GDN + D_SSD + gi * POOL_GROUP_DIM, mixed * pscale_ref[0:1, sl] * _silu(z))

    for j in range(CONV_K - 1):
        t = t_len - (CONV_K - 1) + j
        gcv_out[j] = proj_ref[t, :, C_QKV:C_QKV + 3 * D_GDN]
        scv_out[j] = proj_ref[t, :, C_XBC:C_XBC + SSD_CONV_CH]
    for j in range(POOL_BUF):
        src = j + t_len
        pool_out[j] = pool_in[src] if src < POOL_BUF else proj_ref[src - POOL_BUF, :, C_UP:C_UP + D_POOL]


def _mixer_sample(proj, states, params, layer, n_seq, prev_outs):
    nb, t_len = SAMPLE_NB, SAMPLE_T
    assert CONV_K - 1 <= t_len and proj.shape == (t_len, n_seq, D_PACK)
    prev_outs = tuple(jnp.zeros(s.shape, F32) for s in states) if prev_outs is None else tuple(prev_outs)
    n_state = len(states)

    def seq_spec(s):
        if s.shape[1] == n_seq:
            tail = tuple(s.shape[2:])
            return pl.BlockSpec((None, nb) + tail, lambda i: (layer, i) + (0,) * len(tail))
        assert s.shape[2] == n_seq and s.ndim == 4
        return pl.BlockSpec((None, s.shape[1], nb, s.shape[3]), lambda i: (layer, 0, i, 0))

    return pl.pallas_call(
        functools.partial(_mixer_sample_kernel, n_alias=len(prev_outs), pos0=PAST_LEN),
        grid=(n_seq // nb,),
        in_specs=[pl.BlockSpec((t_len, nb, D_PACK), lambda i: (0, i, 0))]
        + [seq_spec(s) for s in states]
        + [pl.BlockSpec(memory_space=pl.ANY)] * len(prev_outs)
        + _param_specs(params, layer),
        out_specs=[pl.BlockSpec((t_len, nb, D_MODEL), lambda i: (0, i, 0))] + [seq_spec(s) for s in states],
        out_shape=[jax.ShapeDtypeStruct((t_len, n_seq, D_MODEL), F32)]
        + [jax.ShapeDtypeStruct(s.shape, F32) for s in states],
        input_output_aliases={1 + n_state + k: 1 + k for k in range(len(prev_outs))},
        compiler_params=pltpu.CompilerParams(
            dimension_semantics=("parallel",), vmem_limit_bytes=VMEM_LIMIT),
        name="mixer_sample",
    )(proj, *states, *prev_outs, *params)


def _gate_params(gdn_a_log, gdn_dt_bias, ssd_a_log, ssd_dt_bias):
    def lanes(gdn, ssd):
        z = jnp.zeros((DEPTH, LANE), F32)
        z = z.at[:, L_GDN_G:L_GDN_G + GDN_HEADS].set(gdn)
        return z.at[:, L_SSD_DT:L_SSD_DT + SSD_HEADS].set(ssd)

    rows = jnp.stack([lanes(gdn_a_log, ssd_a_log), lanes(gdn_dt_bias, ssd_dt_bias)], axis=1)
    return jnp.concatenate([rows, jnp.zeros((DEPTH, SUBLANE - 2, LANE), F32)], axis=1)


def kernel(x_prompt, x_sample, state_gdn, state_gdn_conv, state_ssd, state_ssd_conv, state_pool, norm_w, w_in, gdn_conv_w, gdn_a_log, gdn_dt_bias, gdn_norm_w, ssd_conv_w, ssd_conv_b, ssd_a_log, ssd_dt_bias, ssd_d, ssd_norm_w, pool_w, pool_scale, w_out, final_norm_w):
    batch, seq, _ = x_prompt.shape
    n_seq, t_len, _ = x_sample.shape
    assert t_len == SAMPLE_T and seq % PROMPT_TT == 0 and n_seq % SAMPLE_NB == 0

    w_in_p = _pack_w_in(jnp.swapaxes(w_in, 1, 2))
    params = (
        _gate_params(gdn_a_log, gdn_dt_bias, ssd_a_log, ssd_dt_bias),
        gdn_conv_w,
        gdn_norm_w[:, None, :],
        ssd_conv_w,
        ssd_conv_b[:, None, :],
        jnp.repeat(ssd_d, SSD_HEAD_DIM, axis=-1)[:, None, :],
        ssd_norm_w[:, None, :],
        pool_w.astype(BF16),
        pool_scale[:, None, :],
    )
    nw = norm_w[:, None, :]
    fw = final_norm_w[None, :]
    states = (state_gdn, jnp.swapaxes(state_gdn_conv, 1, 2), state_ssd, jnp.swapaxes(state_ssd_conv, 1, 2),
              jnp.swapaxes(state_pool, 1, 2))

    xp = x_prompt.reshape(batch * seq, D_MODEL)
    xs = jnp.swapaxes(x_sample, 0, 1).reshape(t_len * n_seq, D_MODEL)
    new_p = [[] for _ in range(5)]
    new_s = None
    for layer in range(DEPTH):
        final = fw if layer == DEPTH - 1 else None

        y_odd, y_even, gdn_p, ssd_p, gcv_p, scv_p, pool_p = _layer_prompt(xp, nw, w_in_p, params, layer, batch, seq)
        xp = _outproj((y_even, y_odd), w_out, xp, layer, 512, final)
        for acc, st in zip(new_p, (gdn_p, gcv_p, ssd_p, scv_p, pool_p)):
            acc.append(st)

        proj_s = _inproj(xs, nw, w_in_p, layer, n_seq * t_len).reshape(t_len, n_seq, D_PACK)
        y_s, *new_s = _mixer_sample(proj_s, states, params, layer, n_seq, new_s)
        xs = _outproj(y_s.reshape(t_len * n_seq, D_MODEL), w_out, xs, layer, n_seq * t_len, final)

    new_s = [a if a.shape[1] == n_seq else jnp.swapaxes(a, 1, 2) for a in new_s]
    y_prompt = xp.reshape(batch, seq, D_MODEL)
    y_sample = jnp.swapaxes(xs.reshape(t_len, n_seq, D_MODEL), 0, 1)
    return (y_prompt, y_sample) + tuple(jnp.stack(a) for a in new_p) + tuple(new_s)
```

```python
import functools

import jax
import jax.numpy as jnp
import numpy as np
from jax import lax
from jax.experimental import pallas as pl
from jax.experimental.pallas import tpu as pltpu

F32 = jnp.float32
BF16 = jnp.bfloat16

D_MODEL = 2048
DEPTH = 4
PAST_LEN = 16384
D_GDN = 768
D_SSD = 768
D_POOL = 512
CONV_K = 4
GDN_HEAD_DIM = 128
GDN_HEADS = 6
GDN_CHUNK = 64
SSD_HEAD_DIM = 64
SSD_HEADS = 12
SSD_STATE = 128
SSD_GROUPS = 2
SSD_HEADS_PER_GROUP = 6
SSD_GROUP_DIM = D_SSD // SSD_GROUPS
SSD_CHUNK = 128
SSD_CONV_CH = 1280
POOL_WINDOWS = (2, 4, 8, 16)
POOL_GROUP_DIM = 128
POOL_BUF = 15
IN_SIZES = (3 * D_GDN, D_GDN, GDN_HEADS, GDN_HEADS, D_SSD, SSD_CONV_CH, SSD_HEADS, D_POOL, D_POOL)
EPS = 1e-6

C_QKV = 0
C_ZG = 2304
C_ZS = 3072
C_XBC = 3840
C_UP = 5120
C_ZP = 5632
C_SM = 6144
D_PACK = 6400
L_BETA = 0
L_GDN_G = 6
L_SSD_DT = 16

LANE = 128
SUBLANE = 8
MXU_WIDTH = 256
VMEM_LIMIT = 48 * 1024 * 1024
VMEM_LIMIT_PROJ = 56 * 1024 * 1024

PROJ_TN = 5 * MXU_WIDTH
PACK_TK = 256
PROMPT_TT = 128
SAMPLE_NB = 8
SAMPLE_T = 4


def _dot(a, b):
    return jnp.dot(a, b, preferred_element_type=F32)


def _dot_nt(a, b):
    return lax.dot_general(a, b, (((1,), (1,)), ((), ())), preferred_element_type=F32)


def _dot_tn(a, b):
    return lax.dot_general(a, b, (((0,), (0,)), ((), ())), preferred_element_type=F32)


def _sigmoid(x):
    return 0.5 + 0.5 * jnp.tanh(0.5 * x)


def _silu(x):
    hx = 0.5 * x
    return hx + hx * jnp.tanh(hx)


def _softplus(x):
    return jnp.maximum(x, 0.0) + jnp.log(1.0 + jnp.exp(-jnp.abs(x)))


def _bf(x):
    return x.astype(BF16)


def _inproj_kernel(x_ref, nw_ref, w_ref, o_ref, h_ref):
    @pl.when(pl.program_id(1) == 0)
    def _():
        x = x_ref[...]
        ms = jnp.mean(x * x, axis=-1, keepdims=True)
        h_ref[...] = _bf(x * lax.rsqrt(ms + EPS) * nw_ref[...])

    o_ref[...] = _dot(h_ref[...], w_ref[...])


def _inproj(x, nw, w_packed, layer, tm):
    n = x.shape[0]
    return pl.pallas_call(
        _inproj_kernel,
        grid=(n // tm, D_PACK // PROJ_TN),
        in_specs=[
            pl.BlockSpec((tm, D_MODEL), lambda i, j: (i, 0)),
            pl.BlockSpec((None, 1, D_MODEL), lambda i, j: (layer, 0, 0)),
            pl.BlockSpec((None, D_MODEL, PROJ_TN), lambda i, j: (layer, 0, j)),
        ],
        out_specs=pl.BlockSpec((tm, PROJ_TN), lambda i, j: (i, j)),
        out_shape=jax.ShapeDtypeStruct((n, D_PACK), F32),
        scratch_shapes=[pltpu.VMEM((tm, D_MODEL), BF16)],
        compiler_params=pltpu.CompilerParams(
            dimension_semantics=("parallel", "arbitrary"), vmem_limit_bytes=VMEM_LIMIT_PROJ),
        name="inproj",
    )(x, nw, w_packed)


def _pack_kernel(w_ref, o_ref):
    offs = np.cumsum((0,) + IN_SIZES)
    tk = o_ref.shape[0]

    for seg, c_out in ((0, C_QKV), (1, C_ZG), (4, C_ZS), (5, C_XBC), (7, C_UP), (8, C_ZP)):
        o_ref[:, c_out:c_out + IN_SIZES[seg]] = _bf(w_ref[offs[seg]:offs[seg + 1], :].T)
    assert offs[3] == offs[2] + GDN_HEADS and L_GDN_G == L_BETA + GDN_HEADS and L_BETA == 0
    assert offs[2] % SUBLANE == 0 and offs[2] + LANE <= offs[-1]
    lane = lax.broadcasted_iota(jnp.int32, (tk, LANE), 1)
    gates = w_ref[offs[2]:offs[2] + LANE, :].T
    dt_row0 = offs[6] - offs[6] % SUBLANE
    dt_lane0 = offs[6] - dt_row0
    assert dt_row0 + LANE <= offs[-1] and dt_lane0 <= L_SSD_DT
    dts = pltpu.roll(w_ref[dt_row0:dt_row0 + LANE, :].T, L_SSD_DT - dt_lane0, 1)
    small = jnp.where(lane < 2 * GDN_HEADS, gates,
                      jnp.where((lane >= L_SSD_DT) & (lane < L_SSD_DT + SSD_HEADS), dts, 0.0))
    o_ref[:, C_SM:C_SM + LANE] = _bf(small)
    o_ref[:, C_SM + LANE:D_PACK] = jnp.zeros((tk, D_PACK - C_SM - LANE), BF16)


def _pack_w_in(w_in_t):
    d_in = w_in_t.shape[1]
    return pl.pallas_call(
        _pack_kernel,
        grid=(DEPTH, D_MODEL // PACK_TK),
        in_specs=[pl.BlockSpec((None, d_in, PACK_TK), lambda l, i: (l, 0, i))],
        out_specs=pl.BlockSpec((None, PACK_TK, D_PACK), lambda l, i: (l, i, 0)),
        out_shape=jax.ShapeDtypeStruct((DEPTH, D_MODEL, D_PACK), BF16),
        compiler_params=pltpu.CompilerParams(
            dimension_semantics=("parallel", "parallel"), vmem_limit_bytes=VMEM_LIMIT),
        name="pack_w_in",
    )(w_in_t)


def _outproj_kernel(*refs, n_y, final):
    y_refs, (w32_ref, x_ref), rest = refs[:n_y], refs[n_y:n_y + 2], refs[n_y + 2:]
    o_ref, w_ref = rest[-2], rest[-1]

    @pl.when(pl.program_id(0) == 0)
    def _():
        w_ref[...] = _bf(w32_ref[...])

    if n_y == 1:
        y = y_refs[0][...]
    else:
        tt = PROMPT_TT
        y = jnp.concatenate([r[k * tt:(k + 1) * tt, :] for k in range(y_refs[0].shape[0] // tt) for r in y_refs],
                            axis=0)
    x = x_ref[...] + _dot(_bf(y), w_ref[...])
    if final:
        ms = jnp.mean(x * x, axis=-1, keepdims=True)
        x = x * lax.rsqrt(ms + EPS) * rest[0][...]
    o_ref[...] = x


def _outproj(ys, w_out, x, layer, tm, final_w=None):
    n = x.shape[0]
    ys = tuple(ys) if isinstance(ys, (tuple, list)) else (ys,)
    in_specs = [pl.BlockSpec((tm // len(ys), D_MODEL), lambda i: (i, 0)) for _ in ys] + [
        pl.BlockSpec((None, D_MODEL, D_MODEL), lambda i: (layer, 0, 0), pipeline_mode=pl.Buffered(1)),
        pl.BlockSpec((tm, D_MODEL), lambda i: (i, 0)),
    ]
    args = [*ys, w_out, x]
    if final_w is not None:
        in_specs.append(pl.BlockSpec((1, D_MODEL), lambda i: (0, 0)))
        args.append(final_w)
    return pl.pallas_call(
        functools.partial(_outproj_kernel, n_y=len(ys), final=final_w is not None),
        grid=(n // tm,),
        in_specs=in_specs,
        out_specs=pl.BlockSpec((tm, D_MODEL), lambda i: (i, 0)),
        out_shape=jax.ShapeDtypeStruct((n, D_MODEL), F32),
        scratch_shapes=[pltpu.VMEM((D_MODEL, D_MODEL), BF16)],
        compiler_params=pltpu.CompilerParams(
            dimension_semantics=("arbitrary",), vmem_limit_bytes=VMEM_LIMIT_PROJ),
        name="outproj",
    )(*args)


def _conv_silu_tile(ext_ref, hist, c0, width, w_ref, wc0, b_ref=None):
    e = ext_ref[:, c0:c0 + width]
    acc = e * w_ref[0:1, wc0:wc0 + width]
    for j in range(1, CONV_K):
        acc = pltpu.roll(acc, 1, 0) + e * w_ref[j:j + 1, wc0:wc0 + width]
    acc = acc[hist:]
    if b_ref is not None:
        acc = acc + b_ref[0:1, wc0:wc0 + width]
    return _silu(acc)


def _l2norm(x):
    return x * lax.rsqrt(jnp.sum(x * x, axis=-1, keepdims=True) + EPS)


def _seg_cumsum(x, seg_len, row_in_seg):
    s = 1
    while s < seg_len:
        x = x + jnp.where(row_in_seg >= s, pltpu.roll(x, s, 0), 0.0)
        s *= 2
    return x


def _gates(small, gp_ref):
    sp = _softplus(small + gp_ref[1:2, :])
    gd = -jnp.exp(gp_ref[0:1, :]) * sp
    return _sigmoid(small), sp, gd


def _masked_decay(col, row, causal):
    return jnp.where(causal, jnp.exp(jnp.where(causal, col - row, 0.0)), 0.0)


def _gdn_finish(o, z, gnorm_ref):
    on = o * lax.rsqrt(jnp.mean(o * o, axis=-1, keepdims=True) + EPS) * gnorm_ref[0:1, :]
    return on * _silu(z)


HIST = SUBLANE
PHIST = 2 * SUBLANE


def _reset_prompt_state(sg_ref, ss_ref, eq_ref, ex_ref, ep_ref):
    sg_ref[...] = jnp.zeros(sg_ref.shape, F32)
    ss_ref[...] = jnp.zeros(ss_ref.shape, F32)
    eq_ref[0:HIST, :] = jnp.zeros((HIST, eq_ref.shape[1]), F32)
    ex_ref[0:HIST, :] = jnp.zeros((HIST, ex_ref.shape[1]), F32)
    ep_ref[0:PHIST, :] = jnp.zeros((PHIST, D_POOL), F32)


def _mixer_tile(proj_ref, y_ref, i, pos0, params, sg_ref, ss_ref, eq_ref, ex_ref, ep_ref):
    gp_ref, gcw_ref, gnorm_ref, scw_ref, scb_ref, sd_ref, snorm_ref, pw_ref, pscale_ref = params
    tt = PROMPT_TT
    n_chunks = tt // GDN_CHUNK
    hist, phist = HIST, PHIST

    eq_ref[hist:hist + tt, :] = proj_ref[:, C_QKV:C_QKV + 3 * D_GDN]
    ex_ref[hist:hist + tt, :] = proj_ref[:, C_XBC:C_XBC + SSD_CONV_CH]
    ep_ref[phist:phist + tt, :] = proj_ref[:, C_UP:C_UP + D_POOL]

    beta, sp, gd = _gates(proj_ref[:, C_SM:C_SM + LANE], gp_ref)
    row = lax.broadcasted_iota(jnp.int32, (tt, LANE), 0)
    gc = _seg_cumsum(gd, GDN_CHUNK, row & (GDN_CHUNK - 1))
    ac = _seg_cumsum(gd, SSD_CHUNK, row)

    def store_y(c0, val, r0=0):
        y_ref[r0:r0 + val.shape[0], c0:c0 + val.shape[1]] = val.astype(y_ref.dtype)

    ci = lax.broadcasted_iota(jnp.int32, (GDN_CHUNK, GDN_CHUNK), 0)
    cj = lax.broadcasted_iota(jnp.int32, (GDN_CHUNK, GDN_CHUNK), 1)
    causal_g, strict_g = ci >= cj, ci > cj
    gc_t = [gc[c * GDN_CHUNK:(c + 1) * GDN_CHUNK].T for c in range(n_chunks)]
    conv_q = functools.partial(_conv_silu_tile, eq_ref, hist)

    keys = [(c, h) for c in range(n_chunks) for h in range(GDN_HEADS)]
    nmat, pmat, rhs, qkb, qdec, kdec_t, glast = {}, {}, {}, {}, {}, {}, {}
    for h in range(GDN_HEADS):
        c0 = h * GDN_HEAD_DIM
        q_all = _l2norm(conv_q(c0, GDN_HEAD_DIM, gcw_ref, c0)) * (GDN_HEAD_DIM ** -0.5)
        k_all = _l2norm(conv_q(D_GDN + c0, GDN_HEAD_DIM, gcw_ref, D_GDN + c0))
        v_all = conv_q(2 * D_GDN + c0, GDN_HEAD_DIM, gcw_ref, 2 * D_GDN + c0)
        for c in range(n_chunks):
            rs = slice(c * GDN_CHUNK, (c + 1) * GDN_CHUNK)
            qh, kh, vh = q_all[rs], k_all[rs], v_all[rs]
            beta_c = beta[rs, L_BETA + h:L_BETA + h + 1]
            gc_c = gc[rs, L_GDN_G + h:L_GDN_G + h + 1]
            gc_r = gc_t[c][L_GDN_G + h:L_GDN_G + h + 1, :]
            gl = gc_c[GDN_CHUNK - 1:GDN_CHUNK, :]
            eg = jnp.exp(gc_c)
            decay = _masked_decay(gc_c, gc_r, causal_g)
            kb, qb = _bf(kh), _bf(qh)
            nmat[c, h] = jnp.where(strict_g, -(_dot_nt(kb, kb) * decay * beta_c), 0.0)
            qkb[c, h] = _bf(jnp.where(causal_g, _dot_nt(qb, kb) * decay, 0.0))
            rhs[c, h] = jnp.concatenate([vh * beta_c, kh * (beta_c * eg)], axis=-1)
            qdec[c, h] = qh * eg
            kdec_t[c, h] = _bf((kh * jnp.exp(gl - gc_c)).T)
            glast[c, h] = jnp.exp(gl)

    for key in keys:
        bb = _bf(nmat[key])
        pmat[key] = _dot(bb, bb)
    for _ in range(4):
        for key in keys:
            pb = _bf(pmat[key])
            r = _dot(jnp.concatenate([_bf(nmat[key]), pb], axis=0), pb)
            nmat[key] = nmat[key] + pmat[key] + r[:GDN_CHUNK]
            pmat[key] = r[GDN_CHUNK:]
    sol = {}
    for key in keys:
        nmat[key] = nmat[key] + pmat[key] + _dot(_bf(nmat[key]), _bf(pmat[key]))
    for key in keys:
        sol[key] = rhs[key] + _dot(_bf(nmat[key]), _bf(rhs[key]))

    for c in range(n_chunks):
        r0 = c * GDN_CHUNK
        for h in range(GDN_HEADS):
            key = (c, h)
            state = sg_ref[h]
            u, w = sol[key][:, :GDN_HEAD_DIM], sol[key][:, GDN_HEAD_DIM:]
            r1 = _dot(_bf(jnp.concatenate([w, qdec[key]], axis=0)), _bf(state))
            vnb = _bf(u - r1[:GDN_CHUNK])
            r2 = _dot(jnp.concatenate([qkb[key], kdec_t[key]], axis=0), vnb)
            sg_ref[h] = state * glast[key] + r2[GDN_CHUNK:]
            o = r1[GDN_CHUNK:] + r2[:GDN_CHUNK]
            c0 = h * GDN_HEAD_DIM
            z = proj_ref[r0:r0 + GDN_CHUNK, C_ZG + c0:C_ZG + c0 + GDN_HEAD_DIM]
            store_y(c0, _gdn_finish(o, z, gnorm_ref), r0)

    si = lax.broadcasted_iota(jnp.int32, (SSD_CHUNK, SSD_CHUNK), 0)
    sj = lax.broadcasted_iota(jnp.int32, (SSD_CHUNK, SSD_CHUNK), 1)
    causal_s = si >= sj
    lo_lanes = sj < SSD_HEAD_DIM
    lo_rows = si < SSD_HEAD_DIM
    ac_t = ac.T
    conv_x = functools.partial(_conv_silu_tile, ex_ref, hist)
    for g in range(SSD_GROUPS):
        bgb = _bf(conv_x(D_SSD + g * SSD_STATE, SSD_STATE, scw_ref, D_SSD + g * SSD_STATE, scb_ref))
        cgb = _bf(conv_x(D_SSD + (SSD_GROUPS + g) * SSD_STATE, SSD_STATE, scw_ref,
                         D_SSD + (SSD_GROUPS + g) * SSD_STATE, scb_ref))
        cb = _dot_nt(cgb, bgb)
        yz, ssq = [], None
        for pair in range(SSD_HEADS_PER_GROUP // 2):
            hd0 = g * SSD_HEADS_PER_GROUP + 2 * pair
            c0 = hd0 * SSD_HEAD_DIM
            la, lb = L_SSD_DT + hd0, L_SSD_DT + hd0 + 1
            xs2 = conv_x(c0, 2 * SSD_HEAD_DIM, scw_ref, c0, scb_ref)
            a2 = jnp.where(lo_lanes, ac[:, la:la + 1], ac[:, lb:lb + 1])
            xdt2 = xs2 * jnp.where(lo_lanes, sp[:, la:la + 1], sp[:, lb:lb + 1])
            alast2 = a2[tt - 1:tt, :]
            state2 = ss_ref[hd0:hd0 + 2].reshape(2 * SSD_HEAD_DIM, SSD_STATE)
            y = jnp.exp(a2) * _dot_nt(cgb, _bf(state2))
            seg_a = _masked_decay(ac[:, la:la + 1], ac_t[la:la + 1, :], causal_s)
            seg_b = _masked_decay(ac[:, lb:lb + 1], ac_t[lb:lb + 1, :], causal_s)
            y = y + _dot(_bf(cb * seg_a), _bf(jnp.where(lo_lanes, xdt2, 0.0)))
            y = y + _dot(_bf(cb * seg_b), _bf(jnp.where(lo_lanes, 0.0, xdt2)))
            upd = _dot_tn(_bf(xdt2 * jnp.exp(alast2 - a2)), bgb)
            elast = jnp.exp(jnp.where(lo_rows, ac[tt - 1:tt, la:la + 1], ac[tt - 1:tt, lb:lb + 1]))
            ss_ref[hd0:hd0 + 2] = (state2 * elast + upd).reshape(2, SSD_HEAD_DIM, SSD_STATE)
            z = proj_ref[:, C_ZS + c0:C_ZS + c0 + 2 * SSD_HEAD_DIM]
            t = (y + sd_ref[0:1, c0:c0 + 2 * SSD_HEAD_DIM] * xs2) * _silu(z)
            yz.append(t)
            s = jnp.sum(t * t, axis=-1, keepdims=True)
            ssq = s if ssq is None else ssq + s
        scale = lax.rsqrt(ssq * (1.0 / SSD_GROUP_DIM) + EPS)
        for pair, t in enumerate(yz):
            c0 = (g * SSD_HEADS_PER_GROUP + 2 * pair) * SSD_HEAD_DIM
            store_y(D_GDN + c0, t * scale * snorm_ref[0:1, c0:c0 + 2 * SSD_HEAD_DIM])

    pos = pos0 + i * tt + lax.broadcasted_iota(jnp.int32, (tt, 1), 0)
    for gi, win in enumerate(POOL_WINDOWS):
        sl = slice(gi * POOL_GROUP_DIM, (gi + 1) * POOL_GROUP_DIM)
        e = ep_ref[:, sl]
        acc, k = e, 1
        while k < win:
            acc = acc + pltpu.roll(acc, k, 0)
            k *= 2
        cnt = jnp.minimum(win, pos + 1).astype(F32)
        pooled = acc[phist:] / cnt - e[phist:]
        mixed = _dot(_bf(pooled), pw_ref[gi])
        z = proj_ref[:, C_ZP + gi * POOL_GROUP_DIM:C_ZP + (gi + 1) * POOL_GROUP_DIM]
        store_y(D_GDN + D_SSD + gi * POOL_GROUP_DIM, mixed * pscale_ref[0:1, sl] * _silu(z))

    eq_ref[0:hist, :] = eq_ref[tt:tt + hist, :]
    ex_ref[0:hist, :] = ex_ref[tt:tt + hist, :]
    ep_ref[0:phist, :] = ep_ref[tt:tt + phist, :]


def _layer_spec(shape):
    nd = len(shape)
    return lambda layer: pl.BlockSpec((None,) + tuple(shape[1:]), lambda *_: (layer,) + (0,) * (nd - 1))


def _param_specs(params, layer):
    return [_layer_spec(p.shape)(layer) for p in params]


def _layer_prompt_kernel(x_ref, nw_ref, w_ref, *refs, tiles_per_seq, n_steps, pos0):
    params = refs[:9]
    (y_odd_ref, y_even_ref, sg_out, ss_out, gcv_out, scv_out, pool_out,
     p_even, p_odd, sg_ref, ss_ref, eq_ref, ex_ref, ep_ref) = refs[9:]
    tt = PROMPT_TT
    g = pl.program_id(0)
    state = (sg_ref, ss_ref, eq_ref, ex_ref, ep_ref)

    def project(rows, p_ref):
        x = x_ref[rows, :]
        ms = jnp.mean(x * x, axis=-1, keepdims=True)
        p_ref[...] = _dot(_bf(x * lax.rsqrt(ms + EPS) * nw_ref[...]), w_ref[...])

    @pl.when(g == 0)
    def _():
        project(slice(0, tt), p_even)

    @pl.when(g > 0)
    def _():
        project(slice(0, tt), p_even)
        _mixer_tile(p_odd, y_odd_ref, lax.rem(2 * g - 1, tiles_per_seq), pos0, params, *state)
        sg_out[0] = sg_ref[...]
        ss_out[0] = ss_ref[...]
        gcv_out[0] = eq_ref[HIST + tt - (CONV_K - 1):HIST + tt, :]
        scv_out[0] = ex_ref[HIST + tt - (CONV_K - 1):HIST + tt, :]
        pool_out[0] = ep_ref[PHIST + tt - POOL_BUF:PHIST + tt, :]

    @pl.when(g < n_steps - 1)
    def _():
        i_even = lax.rem(2 * g, tiles_per_seq)

        @pl.when(i_even == 0)
        def _():
            _reset_prompt_state(*state)

        project(slice(tt, 2 * tt), p_odd)
        _mixer_tile(p_even, y_even_ref, i_even, pos0, params, *state)


def _layer_prompt(x, nw, w_packed, params, layer, batch, seq):
    tt = PROMPT_TT
    tiles_per_seq = seq // tt
    n_tiles = batch * tiles_per_seq
    assert tiles_per_seq % 2 == 0
    n_steps = n_tiles // 2 + 1

    def seq_of_odd_tile(g):
        return jnp.maximum(2 * g - 1, 0) // tiles_per_seq

    def state_spec(tail):
        return pl.BlockSpec((1,) + tail, lambda g: (seq_of_odd_tile(g),) + (0,) * len(tail))

    tails = ((GDN_HEADS, GDN_HEAD_DIM, GDN_HEAD_DIM), (SSD_HEADS, SSD_HEAD_DIM, SSD_STATE),
             (CONV_K - 1, 3 * D_GDN), (CONV_K - 1, SSD_CONV_CH), (POOL_BUF, D_POOL))
    return pl.pallas_call(
        functools.partial(_layer_prompt_kernel, tiles_per_seq=tiles_per_seq, n_steps=n_steps, pos0=0),
        grid=(n_steps,),
        in_specs=[
            pl.BlockSpec((2 * tt, D_MODEL), lambda g: (jnp.minimum(g, n_steps - 2), 0)),
            pl.BlockSpec((None, 1, D_MODEL), lambda g: (layer, 0, 0)),
            pl.BlockSpec((None, D_MODEL, D_PACK), lambda g: (layer, 0, 0), pipeline_mode=pl.Buffered(1)),
        ] + _param_specs(params, layer),
        out_specs=[
            pl.BlockSpec((tt, D_MODEL), lambda g: (jnp.maximum(g - 1, 0), 0)),
            pl.BlockSpec((tt, D_MODEL), lambda g: (jnp.minimum(g, n_steps - 2), 0)),
        ] + [state_spec(t) for t in tails],
        out_shape=[jax.ShapeDtypeStruct((n_tiles // 2 * tt, D_MODEL), BF16)] * 2
        + [jax.ShapeDtypeStruct((batch,) + t, F32) for t in tails],
        scratch_shapes=[
            pltpu.VMEM((tt, D_PACK), F32),
            pltpu.VMEM((tt, D_PACK), F32),
            pltpu.VMEM(tails[0], F32),
            pltpu.VMEM(tails[1], F32),
            pltpu.VMEM((HIST + tt, 3 * D_GDN), F32),
            pltpu.VMEM((HIST + tt, SSD_CONV_CH), F32),
            pltpu.VMEM((PHIST + tt, D_POOL), F32),
        ],
        compiler_params=pltpu.CompilerParams(
            dimension_semantics=("arbitrary",), vmem_limit_bytes=VMEM_LIMIT_PROJ),
        name="layer_prompt",
    )(x, nw, w_packed, *params)


def _mixer_sample_kernel(*refs, n_alias, pos0):
    proj_ref, sg_in, gcv_in, ss_in, scv_in, pool_in = refs[:6]
    (gp_ref, gcw_ref, gnorm_ref, scw_ref, scb_ref, sd_ref, snorm_ref, pw_ref, pscale_ref,
     y_ref, sg_out, gcv_out, ss_out, scv_out, pool_out) = refs[6 + n_alias:]
    nb, t_len = SAMPLE_NB, SAMPLE_T
    assert nb == SUBLANE
    rows = nb * t_len
    last = slice(rows - nb, rows)

    def tokens(c0, width):
        return [proj_ref[t, :, c0:c0 + width] for t in range(t_len)]

    def stack(parts):
        return jnp.concatenate(parts, axis=0)

    def conv(hist_ref, col0, cw, width, w_ref, b_ref=None):
        u = [hist_ref[j, :, cw:cw + width] for j in range(CONV_K - 1)] + tokens(col0 + cw, width)
        outs = []
        for t in range(t_len):
            acc = u[t] * w_ref[0:1, cw:cw + width]
            for j in range(1, CONV_K):
                acc = acc + u[t + j] * w_ref[j:j + 1, cw:cw + width]
            if b_ref is not None:
                acc = acc + b_ref[0:1, cw:cw + width]
            outs.append(_silu(acc))
        return stack(outs)

    beta, sp, gd = _gates(stack(tokens(C_SM, LANE)), gp_ref)
    parts = [gd[0:nb]]
    for t in range(1, t_len):
        parts.append(parts[-1] + gd[t * nb:(t + 1) * nb])
    cum = stack(parts)
    cum_t = cum.T

    ri = lax.broadcasted_iota(jnp.int32, (rows, rows), 0)
    rj = lax.broadcasted_iota(jnp.int32, (rows, rows), 1)
    same = (ri & (nb - 1)) == (rj & (nb - 1))
    causal, strict = same & (ri >= rj), same & (ri > rj)
    seq1 = lax.broadcasted_iota(jnp.int32, (rows, 1), 0) & (nb - 1)
    seq2 = lax.broadcasted_iota(jnp.int32, (2 * rows, 1), 0) & (nb - 1)

    def block_diag(x):
        return jnp.concatenate([jnp.where(seq1 == b, x, 0.0) for b in range(nb)], axis=1)

    def store_y(c0, val):
        for t in range(t_len):
            y_ref[t, :, c0:c0 + val.shape[1]] = val[t * nb:(t + 1) * nb]

    heads = range(GDN_HEADS)
    bmat, bb, rhs, qkb, qdec, kd_t, egl = {}, {}, {}, {}, {}, {}, {}
    for h in heads:
        c0 = h * GDN_HEAD_DIM
        qh = _l2norm(conv(gcv_in, C_QKV, c0, GDN_HEAD_DIM, gcw_ref)) * (GDN_HEAD_DIM ** -0.5)
        kh = _l2norm(conv(gcv_in, C_QKV, D_GDN + c0, GDN_HEAD_DIM, gcw_ref))
        vh = conv(gcv_in, C_QKV, 2 * D_GDN + c0, GDN_HEAD_DIM, gcw_ref)
        beta_c = beta[:, L_BETA + h:L_BETA + h + 1]
        gc_c = cum[:, L_GDN_G + h:L_GDN_G + h + 1]
        eg = jnp.exp(gc_c)
        decay = _masked_decay(gc_c, cum_t[L_GDN_G + h:L_GDN_G + h + 1, :], causal)
        kb, qb = _bf(kh), _bf(qh)
        bmat[h] = jnp.where(strict, -(_dot_nt(kb, kb) * decay * beta_c), 0.0)
        bb[h] = _bf(bmat[h])
        qkb[h] = _bf(jnp.where(causal, _dot_nt(qb, kb) * decay, 0.0))
        rhs[h] = jnp.concatenate([vh * beta_c, kh * (beta_c * eg)], axis=-1)
        qdec[h] = qh * eg
        gl = gc_c[last]
        kd_t[h] = _bf((kh * jnp.exp(stack([gl] * t_len) - gc_c)).T)
        egl[h] = jnp.exp(gl)
    p1 = {h: _dot(bb[h], bb[h]) for h in heads}
    nmat = {h: bmat[h] + p1[h] + _dot(bb[h], _bf(p1[h])) for h in heads}
    sol = {h: rhs[h] + _dot(_bf(nmat[h]), _bf(rhs[h])) for h in heads}
    acc = {}
    for h in heads:
        lhs = _bf(stack([sol[h][:, GDN_HEAD_DIM:], qdec[h]]))
        a = jnp.zeros((2 * rows, GDN_HEAD_DIM), F32)
        for b in range(nb):
            a = jnp.where(seq2 == b, _dot(lhs, _bf(sg_in[b, h])), a)
        acc[h] = a
    vn = {h: sol[h][:, :GDN_HEAD_DIM] - acc[h][:rows] for h in heads}
    o = {h: acc[h][rows:] + _dot(qkb[h], _bf(vn[h])) for h in heads}
    upd = {h: _dot(kd_t[h], _bf(block_diag(vn[h]))) for h in heads}
    for h in heads:
        c0 = h * GDN_HEAD_DIM
        store_y(c0, _gdn_finish(o[h], stack(tokens(C_ZG + c0, GDN_HEAD_DIM)), gnorm_ref))
        for b in range(nb):
            sg_out[b, h] = (sg_in[b, h] * egl[h][b:b + 1, :]
                            + upd[h][:, b * GDN_HEAD_DIM:(b + 1) * GDN_HEAD_DIM])

    lo_lanes = lax.broadcasted_iota(jnp.int32, (rows, 2 * SSD_HEAD_DIM), 1) < SSD_HEAD_DIM
    lo_rows = lax.broadcasted_iota(jnp.int32, (2 * SSD_HEAD_DIM, SSD_STATE), 0) < SSD_HEAD_DIM
    pairs = range(SSD_HEADS // 2)
    n_gp = SSD_HEADS_PER_GROUP // 2
    cgb, cb, bg_bd = {}, {}, {}
    for g in range(SSD_GROUPS):
        bg = conv(scv_in, C_XBC, D_SSD + g * SSD_STATE, SSD_STATE, scw_ref, scb_ref)
        cg = conv(scv_in, C_XBC, D_SSD + (SSD_GROUPS + g) * SSD_STATE, SSD_STATE, scw_ref, scb_ref)
        cgb[g] = _bf(cg)
        cb[g] = _dot_nt(cgb[g], _bf(bg))
        bg_bd[g] = _bf(block_diag(bg))
    xs2, a2, y, xin_t = {}, {}, {}, {}
    for p in pairs:
        g, c0 = p // n_gp, 2 * p * SSD_HEAD_DIM
        la, lb = L_SSD_DT + 2 * p, L_SSD_DT + 2 * p + 1
        xs2[p] = conv(scv_in, C_XBC, c0, 2 * SSD_HEAD_DIM, scw_ref, scb_ref)
        a2[p] = jnp.where(lo_lanes, cum[:, la:la + 1], cum[:, lb:lb + 1])
        xdt2 = xs2[p] * jnp.where(lo_lanes, sp[:, la:la + 1], sp[:, lb:lb + 1])
        seg_a = _masked_decay(cum[:, la:la + 1], cum_t[la:la + 1, :], causal)
        seg_b = _masked_decay(cum[:, lb:lb + 1], cum_t[lb:lb + 1, :], causal)
        y[p] = (_dot(_bf(cb[g] * seg_a), _bf(jnp.where(lo_lanes, xdt2, 0.0)))
                + _dot(_bf(cb[g] * seg_b), _bf(jnp.where(lo_lanes, 0.0, xdt2))))
        xin_t[p] = _bf((xdt2 * jnp.exp(stack([a2[p][last]] * t_len) - a2[p])).T)
    acc = {}
    for p in pairs:
        a = jnp.zeros((rows, 2 * SSD_HEAD_DIM), F32)
        for b in range(nb):
            state2 = ss_in[b, 2 * p:2 * p + 2].reshape(2 * SSD_HEAD_DIM, SSD_STATE)
            a = jnp.where(seq1 == b, _dot_nt(cgb[p // n_gp], _bf(state2)), a)
        acc[p] = a
    upd = {p: _dot(xin_t[p], bg_bd[p // n_gp]) for p in pairs}
    for p in pairs:
        la, lb = L_SSD_DT + 2 * p, L_SSD_DT + 2 * p + 1
        for b in range(nb):
            r = rows - nb + b
            elast = jnp.exp(jnp.where(lo_rows, cum[r:r + 1, la:la + 1], cum[r:r + 1, lb:lb + 1]))
            state2 = ss_in[b, 2 * p:2 * p + 2].reshape(2 * SSD_HEAD_DIM, SSD_STATE)
            ss_out[b, 2 * p:2 * p + 2] = (state2 * elast + upd[p][:, b * SSD_STATE:(b + 1) * SSD_STATE]).reshape(
                2, SSD_HEAD_DIM, SSD_STATE)
    for g in range(SSD_GROUPS):
        yz, ssq = {}, None
        for p in range(g * n_gp, (g + 1) * n_gp):
            c0 = 2 * p * SSD_HEAD_DIM
            z = stack(tokens(C_ZS + c0, 2 * SSD_HEAD_DIM))
            yt = y[p] + jnp.exp(a2[p]) * acc[p]
            yz[p] = (yt + sd_ref[0:1, c0:c0 + 2 * SSD_HEAD_DIM] * xs2[p]) * _silu(z)
            s = jnp.sum(yz[p] * yz[p], axis=-1, keepdims=True)
            ssq = s if ssq is None else ssq + s
        scale = lax.rsqrt(ssq * (1.0 / SSD_GROUP_DIM) + EPS)
        for p, t in yz.items():
            c0 = 2 * p * SSD_HEAD_DIM
            store_y(D_GDN + c0, t * scale * snorm_ref[0:1, c0:c0 + 2 * SSD_HEAD_DIM])

    for gi, win in enumerate(POOL_WINDOWS):
        sl = slice(gi * POOL_GROUP_DIM, (gi + 1) * POOL_GROUP_DIM)
        x = [pool_in[j, :, sl] for j in range(POOL_BUF)] + tokens(C_UP + gi * POOL_GROUP_DIM, POOL_GROUP_DIM)
        pooled = []
        for t in range(t_len):
            acc = x[POOL_BUF + t]
            for k in range(1, win):
                acc = acc + x[POOL_BUF + t - k]
            pooled.append(acc / float(min(win, pos0 + t + 1)) - x[POOL_BUF + t])
        mixed = _dot(_bf(stack(pooled)), pw_ref[gi])
        z = stack(tokens(C_ZP + gi * POOL_GROUP_DIM, POOL_GROUP_DIM))
        store_y(D_GDN + D_SSD + gi * POOL_GROUP_DIM, mixed * pscale_ref[0:1, sl] * _silu(z))

    for j in range(CONV_K - 1):
        t = t_len - (CONV_K - 1) + j
        gcv_out[j] = proj_ref[t, :, C_QKV:C_QKV + 3 * D_GDN]
        scv_out[j] = proj_ref[t, :, C_XBC:C_XBC + SSD_CONV_CH]
    for j in range(POOL_BUF):
        src = j + t_len
        pool_out[j] = pool_in[src] if src < POOL_BUF else proj_ref[src - POOL_BUF, :, C_UP:C_UP + D_POOL]


def _zero_fill_kernel(*out_refs):
    for o_ref in out_refs:
        o_ref[...] = jnp.zeros(o_ref.shape, o_ref.dtype)


def _zeros_like_states(states, n_seq):
    nb = 2 * SAMPLE_NB

    def spec(s):
        if s.shape[1] == n_seq:
            tail = tuple(s.shape[2:])
            return pl.BlockSpec((1, nb) + tail, lambda l, i: (l, i) + (0,) * len(tail))
        return pl.BlockSpec((1, s.shape[1], nb, s.shape[3]), lambda l, i: (l, 0, i, 0))

    return pl.pallas_call(
        _zero_fill_kernel,
        grid=(DEPTH, n_seq // nb),
        out_specs=[spec(s) for s in states],
        out_shape=[jax.ShapeDtypeStruct(s.shape, F32) for s in states],
        compiler_params=pltpu.CompilerParams(
            dimension_semantics=("parallel", "parallel"), vmem_limit_bytes=VMEM_LIMIT),
        name="zero_fill",
    )()


def _mixer_sample(proj, states, params, layer, n_seq, prev_outs):
    nb, t_len = SAMPLE_NB, SAMPLE_T
    assert CONV_K - 1 <= t_len and proj.shape == (t_len, n_seq, D_PACK)
    prev_outs = tuple(_zeros_like_states(states, n_seq) if prev_outs is None else prev_outs)
    n_state = len(states)

    def seq_spec(s):
        if s.shape[1] == n_seq:
            tail = tuple(s.shape[2:])
            return pl.BlockSpec((None, nb) + tail, lambda i: (layer, i) + (0,) * len(tail))
        assert s.shape[2] == n_seq and s.ndim == 4
        return pl.BlockSpec((None, s.shape[1], nb, s.shape[3]), lambda i: (layer, 0, i, 0))

    return pl.pallas_call(
        functools.partial(_mixer_sample_kernel, n_alias=len(prev_outs), pos0=PAST_LEN),
        grid=(n_seq // nb,),
        in_specs=[pl.BlockSpec((t_len, nb, D_PACK), lambda i: (0, i, 0))]
        + [seq_spec(s) for s in states]
        + [pl.BlockSpec(memory_space=pl.ANY)] * len(prev_outs)
        + _param_specs(params, layer),
        out_specs=[pl.BlockSpec((t_len, nb, D_MODEL), lambda i: (0, i, 0))] + [seq_spec(s) for s in states],
        out_shape=[jax.ShapeDtypeStruct((t_len, n_seq, D_MODEL), F32)]
        + [jax.ShapeDtypeStruct(s.shape, F32) for s in states],
        input_output_aliases={1 + n_state + k: 1 + k for k in range(len(prev_outs))},
        compiler_params=pltpu.CompilerParams(
            dimension_semantics=("parallel",), vmem_limit_bytes=VMEM_LIMIT),
        name="mixer_sample",
    )(proj, *states, *prev_outs, *params)


def _gate_params(gdn_a_log, gdn_dt_bias, ssd_a_log, ssd_dt_bias):
    def lanes(gdn, ssd):
        z = jnp.zeros((DEPTH, LANE), F32)
        z = z.at[:, L_GDN_G:L_GDN_G + GDN_HEADS].set(gdn)
        return z.at[:, L_SSD_DT:L_SSD_DT + SSD_HEADS].set(ssd)

    rows = jnp.stack([lanes(gdn_a_log, ssd_a_log), lanes(gdn_dt_bias, ssd_dt_bias)], axis=1)
    return jnp.concatenate([rows, jnp.zeros((DEPTH, SUBLANE - 2, LANE), F32)], axis=1)


def kernel(x_prompt, x_sample, state_gdn, state_gdn_conv, state_ssd, state_ssd_conv, state_pool, norm_w, w_in, gdn_conv_w, gdn_a_log, gdn_dt_bias, gdn_norm_w, ssd_conv_w, ssd_conv_b, ssd_a_log, ssd_dt_bias, ssd_d, ssd_norm_w, pool_w, pool_scale, w_out, final_norm_w):
    batch, seq, _ = x_prompt.shape
    n_seq, t_len, _ = x_sample.shape
    assert t_len == SAMPLE_T and seq % PROMPT_TT == 0 and n_seq % SAMPLE_NB == 0

    w_in_p = _pack_w_in(jnp.swapaxes(w_in, 1, 2))
    params = (
        _gate_params(gdn_a_log, gdn_dt_bias, ssd_a_log, ssd_dt_bias),
        gdn_conv_w,
        gdn_norm_w[:, None, :],
        ssd_conv_w,
        ssd_conv_b[:, None, :],
        jnp.repeat(ssd_d, SSD_HEAD_DIM, axis=-1)[:, None, :],
        ssd_norm_w[:, None, :],
        pool_w.astype(BF16),
        pool_scale[:, None, :],
    )
    nw = norm_w[:, None, :]
    fw = final_norm_w[None, :]
    states = (state_gdn, jnp.swapaxes(state_gdn_conv, 1, 2), state_ssd, jnp.swapaxes(state_ssd_conv, 1, 2),
              jnp.swapaxes(state_pool, 1, 2))

    xp = x_prompt.reshape(batch * seq, D_MODEL)
    xs = jnp.swapaxes(x_sample, 0, 1).reshape(t_len * n_seq, D_MODEL)
    new_p = [[] for _ in range(5)]
    new_s = None
    for layer in range(DEPTH):
        final = fw if layer == DEPTH - 1 else None

        y_odd, y_even, gdn_p, ssd_p, gcv_p, scv_p, pool_p = _layer_prompt(xp, nw, w_in_p, params, layer, batch, seq)
        xp = _outproj((y_even, y_odd), w_out, xp, layer, 512, final)
        for acc, st in zip(new_p, (gdn_p, gcv_p, ssd_p, scv_p, pool_p)):
            acc.append(st)

        proj_s = _inproj(xs, nw, w_in_p, layer, n_seq * t_len).reshape(t_len, n_seq, D_PACK)
        y_s, *new_s = _mixer_sample(proj_s, states, params, layer, n_seq, new_s)
        xs = _outproj(y_s.reshape(t_len * n_seq, D_MODEL), w_out, xs, layer, n_seq * t_len, final)

    new_s = [a if a.shape[1] == n_seq else jnp.swapaxes(a, 1, 2) for a in new_s]
    y_prompt = xp.reshape(batch, seq, D_MODEL)
    y_sample = jnp.swapaxes(xs.reshape(t_len, n_seq, D_MODEL), 0, 1)
    return (y_prompt, y_sample) + tuple(jnp.stack(a) for a in new_p) + tuple(new_s)
```

```python
import functools

import jax
import jax.numpy as jnp
import numpy as np
from jax import lax
from jax.experimental import pallas as pl
from jax.experimental.pallas import tpu as pltpu

F32 = jnp.float32
BF16 = jnp.bfloat16

D_MODEL = 2048
DEPTH = 4
PAST_LEN = 16384
D_GDN = 768
D_SSD = 768
D_POOL = 512
CONV_K = 4
GDN_HEAD_DIM = 128
GDN_HEADS = 6
GDN_CHUNK = 64
SSD_HEAD_DIM = 64
SSD_HEADS = 12
SSD_STATE = 128
SSD_GROUPS = 2
SSD_HEADS_PER_GROUP = 6
SSD_GROUP_DIM = D_SSD // SSD_GROUPS
SSD_CHUNK = 128
SSD_CONV_CH = 1280
POOL_WINDOWS = (2, 4, 8, 16)
POOL_GROUP_DIM = 128
POOL_BUF = 15
IN_SIZES = (3 * D_GDN, D_GDN, GDN_HEADS, GDN_HEADS, D_SSD, SSD_CONV_CH, SSD_HEADS, D_POOL, D_POOL)
EPS = 1e-6

C_QKV = 0
C_ZG = 2304
C_ZS = 3072
C_XBC = 3840
C_UP = 5120
C_ZP = 5632
C_SM = 6144
D_PACK = 6400
L_BETA = 0
L_GDN_G = 6
L_SSD_DT = 16

LANE = 128
SUBLANE = 8
MXU_WIDTH = 256
VMEM_LIMIT = 48 * 1024 * 1024
VMEM_LIMIT_PROJ = 56 * 1024 * 1024

PROJ_TN = 5 * MXU_WIDTH
PACK_TK = 256
PROMPT_TT = 128
SAMPLE_NB = 8
SAMPLE_T = 4


def _dot(a, b):
    return jnp.dot(a, b, preferred_element_type=F32)


def _dot_nt(a, b):
    return lax.dot_general(a, b, (((1,), (1,)), ((), ())), preferred_element_type=F32)


def _dot_tn(a, b):
    return lax.dot_general(a, b, (((0,), (0,)), ((), ())), preferred_element_type=F32)


def _sigmoid(x):
    return 0.5 + 0.5 * jnp.tanh(0.5 * x)


def _silu(x):
    hx = 0.5 * x
    return hx + hx * jnp.tanh(hx)


def _softplus(x):
    return jnp.maximum(x, 0.0) + jnp.log(1.0 + jnp.exp(-jnp.abs(x)))


def _bf(x):
    return x.astype(BF16)


def _inproj_kernel(x_ref, nw_ref, w_ref, o_ref, h_ref):
    @pl.when(pl.program_id(1) == 0)
    def _():
        x = x_ref[...]
        ms = jnp.mean(x * x, axis=-1, keepdims=True)
        h_ref[...] = _bf(x * lax.rsqrt(ms + EPS) * nw_ref[...])

    o_ref[...] = _dot(h_ref[...], w_ref[...])


def _inproj(x, nw, w_packed, layer, tm):
    n = x.shape[0]
    return pl.pallas_call(
        _inproj_kernel,
        grid=(n // tm, D_PACK // PROJ_TN),
        in_specs=[
            pl.BlockSpec((tm, D_MODEL), lambda i, j: (i, 0)),
            pl.BlockSpec((None, 1, D_MODEL), lambda i, j: (layer, 0, 0)),
            pl.BlockSpec((None, D_MODEL, PROJ_TN), lambda i, j: (layer, 0, j)),
        ],
        out_specs=pl.BlockSpec((tm, PROJ_TN), lambda i, j: (i, j)),
        out_shape=jax.ShapeDtypeStruct((n, D_PACK), F32),
        scratch_shapes=[pltpu.VMEM((tm, D_MODEL), BF16)],
        compiler_params=pltpu.CompilerParams(
            dimension_semantics=("parallel", "arbitrary"), vmem_limit_bytes=VMEM_LIMIT_PROJ),
        name="inproj",
    )(x, nw, w_packed)


def _pack_kernel(w_ref, o_ref):
    offs = np.cumsum((0,) + IN_SIZES)
    tk = o_ref.shape[0]

    for seg, c_out in ((0, C_QKV), (1, C_ZG), (4, C_ZS), (5, C_XBC), (7, C_UP), (8, C_ZP)):
        o_ref[:, c_out:c_out + IN_SIZES[seg]] = _bf(w_ref[offs[seg]:offs[seg + 1], :].T)
    assert offs[3] == offs[2] + GDN_HEADS and L_GDN_G == L_BETA + GDN_HEADS and L_BETA == 0
    assert offs[2] % SUBLANE == 0 and offs[2] + LANE <= offs[-1]
    lane = lax.broadcasted_iota(jnp.int32, (tk, LANE), 1)
    gates = w_ref[offs[2]:offs[2] + LANE, :].T
    dt_row0 = offs[6] - offs[6] % SUBLANE
    dt_lane0 = offs[6] - dt_row0
    assert dt_row0 + LANE <= offs[-1] and dt_lane0 <= L_SSD_DT
    dts = pltpu.roll(w_ref[dt_row0:dt_row0 + LANE, :].T, L_SSD_DT - dt_lane0, 1)
    small = jnp.where(lane < 2 * GDN_HEADS, gates,
                      jnp.where((lane >= L_SSD_DT) & (lane < L_SSD_DT + SSD_HEADS), dts, 0.0))
    o_ref[:, C_SM:C_SM + LANE] = _bf(small)
    o_ref[:, C_SM + LANE:D_PACK] = jnp.zeros((tk, D_PACK - C_SM - LANE), BF16)


def _pack_w_in(w_in_t):
    d_in = w_in_t.shape[1]
    return pl.pallas_call(
        _pack_kernel,
        grid=(DEPTH, D_MODEL // PACK_TK),
        in_specs=[pl.BlockSpec((None, d_in, PACK_TK), lambda l, i: (l, 0, i))],
        out_specs=pl.BlockSpec((None, PACK_TK, D_PACK), lambda l, i: (l, i, 0)),
        out_shape=jax.ShapeDtypeStruct((DEPTH, D_MODEL, D_PACK), BF16),
        compiler_params=pltpu.CompilerParams(
            dimension_semantics=("parallel", "parallel"), vmem_limit_bytes=VMEM_LIMIT),
        name="pack_w_in",
    )(w_in_t)


def _outproj_kernel(*refs, n_y, final, n_zero):
    y_refs, (w32_ref, x_ref), rest = refs[:n_y], refs[n_y:n_y + 2], refs[n_y + 2:]
    w_ref = rest[-1]
    o_ref, zero_refs = rest[-2 - n_zero], rest[-1 - n_zero:-1]

    @pl.when(pl.program_id(0) == 0)
    def _():
        w_ref[...] = _bf(w32_ref[...])

    for z_ref in zero_refs:
        z_ref[...] = jnp.zeros(z_ref.shape, z_ref.dtype)

    if n_y == 1:
        y = y_refs[0][...]
    else:
        tt = PROMPT_TT
        y = jnp.concatenate([r[k * tt:(k + 1) * tt, :] for k in range(y_refs[0].shape[0] // tt) for r in y_refs],
                            axis=0)
    x = x_ref[...] + _dot(_bf(y), w_ref[...])
    if final:
        ms = jnp.mean(x * x, axis=-1, keepdims=True)
        x = x * lax.rsqrt(ms + EPS) * rest[0][...]
    o_ref[...] = x


def _outproj(ys, w_out, x, layer, tm, final_w=None, zero_like=(), n_seq=None):
    n = x.shape[0]
    ys = tuple(ys) if isinstance(ys, (tuple, list)) else (ys,)
    steps = n // tm

    def zero_spec(s):
        per_layer = steps // s.shape[0]
        assert per_layer * s.shape[0] == steps and n_seq % per_layer == 0
        nb = n_seq // per_layer
        if s.shape[1] == n_seq:
            tail = tuple(s.shape[2:])
            return pl.BlockSpec((1, nb) + tail, lambda i: (i // per_layer, i % per_layer) + (0,) * len(tail))
        assert s.shape[2] == n_seq and s.ndim == 4
        return pl.BlockSpec((1, s.shape[1], nb, s.shape[3]), lambda i: (i // per_layer, 0, i % per_layer, 0))

    in_specs = [pl.BlockSpec((tm // len(ys), D_MODEL), lambda i: (i, 0)) for _ in ys] + [
        pl.BlockSpec((None, D_MODEL, D_MODEL), lambda i: (layer, 0, 0), pipeline_mode=pl.Buffered(1)),
        pl.BlockSpec((tm, D_MODEL), lambda i: (i, 0)),
    ]
    args = [*ys, w_out, x]
    if final_w is not None:
        in_specs.append(pl.BlockSpec((1, D_MODEL), lambda i: (0, 0)))
        args.append(final_w)
    return pl.pallas_call(
        functools.partial(_outproj_kernel, n_y=len(ys), final=final_w is not None, n_zero=len(zero_like)),
        grid=(steps,),
        in_specs=in_specs,
        out_specs=[pl.BlockSpec((tm, D_MODEL), lambda i: (i, 0))] + [zero_spec(s) for s in zero_like],
        out_shape=[jax.ShapeDtypeStruct((n, D_MODEL), F32)]
        + [jax.ShapeDtypeStruct(s.shape, F32) for s in zero_like],
        scratch_shapes=[pltpu.VMEM((D_MODEL, D_MODEL), BF16)],
        compiler_params=pltpu.CompilerParams(
            dimension_semantics=("arbitrary",), vmem_limit_bytes=VMEM_LIMIT_PROJ),
        name="outproj",
    )(*args)


def _conv_silu_tile(ext_ref, hist, c0, width, w_ref, wc0, b_ref=None):
    e = ext_ref[:, c0:c0 + width]
    acc = e * w_ref[0:1, wc0:wc0 + width]
    for j in range(1, CONV_K):
        acc = pltpu.roll(acc, 1, 0) + e * w_ref[j:j + 1, wc0:wc0 + width]
    acc = acc[hist:]
    if b_ref is not None:
        acc = acc + b_ref[0:1, wc0:wc0 + width]
    return _silu(acc)


def _l2norm(x):
    return x * lax.rsqrt(jnp.sum(x * x, axis=-1, keepdims=True) + EPS)


def _seg_cumsum(x, seg_len, row_in_seg):
    s = 1
    while s < seg_len:
        x = x + jnp.where(row_in_seg >= s, pltpu.roll(x, s, 0), 0.0)
        s *= 2
    return x


def _gates(small, gp_ref):
    sp = _softplus(small + gp_ref[1:2, :])
    gd = -jnp.exp(gp_ref[0:1, :]) * sp
    return _sigmoid(small), sp, gd


def _masked_decay(col, row, causal):
    return jnp.where(causal, jnp.exp(jnp.where(causal, col - row, 0.0)), 0.0)


def _gdn_finish(o, z, gnorm_ref):
    on = o * lax.rsqrt(jnp.mean(o * o, axis=-1, keepdims=True) + EPS) * gnorm_ref[0:1, :]
    return on * _silu(z)


HIST = SUBLANE
PHIST = 2 * SUBLANE


def _reset_prompt_state(sg_ref, ss_ref, eq_ref, ex_ref, ep_ref):
    sg_ref[...] = jnp.zeros(sg_ref.shape, F32)
    ss_ref[...] = jnp.zeros(ss_ref.shape, F32)
    eq_ref[0:HIST, :] = jnp.zeros((HIST, eq_ref.shape[1]), F32)
    ex_ref[0:HIST, :] = jnp.zeros((HIST, ex_ref.shape[1]), F32)
    ep_ref[0:PHIST, :] = jnp.zeros((PHIST, D_POOL), F32)


def _mixer_tile(proj_ref, y_ref, i, pos0, params, sg_ref, ss_ref, eq_ref, ex_ref, ep_ref):
    gp_ref, gcw_ref, gnorm_ref, scw_ref, scb_ref, sd_ref, snorm_ref, pw_ref, pscale_ref = params
    tt = PROMPT_TT
    n_chunks = tt // GDN_CHUNK
    hist, phist = HIST, PHIST

    eq_ref[hist:hist + tt, :] = proj_ref[:, C_QKV:C_QKV + 3 * D_GDN]
    ex_ref[hist:hist + tt, :] = proj_ref[:, C_XBC:C_XBC + SSD_CONV_CH]
    ep_ref[phist:phist + tt, :] = proj_ref[:, C_UP:C_UP + D_POOL]

    beta, sp, gd = _gates(proj_ref[:, C_SM:C_SM + LANE], gp_ref)
    row = lax.broadcasted_iota(jnp.int32, (tt, LANE), 0)
    gc = _seg_cumsum(gd, GDN_CHUNK, row & (GDN_CHUNK - 1))
    ac = _seg_cumsum(gd, SSD_CHUNK, row)

    def store_y(c0, val, r0=0):
        y_ref[r0:r0 + val.shape[0], c0:c0 + val.shape[1]] = val.astype(y_ref.dtype)

    ci = lax.broadcasted_iota(jnp.int32, (GDN_CHUNK, GDN_CHUNK), 0)
    cj = lax.broadcasted_iota(jnp.int32, (GDN_CHUNK, GDN_CHUNK), 1)
    causal_g, strict_g = ci >= cj, ci > cj
    gc_t = [gc[c * GDN_CHUNK:(c + 1) * GDN_CHUNK].T for c in range(n_chunks)]
    conv_q = functools.partial(_conv_silu_tile, eq_ref, hist)

    keys = [(c, h) for c in range(n_chunks) for h in range(GDN_HEADS)]
    nmat, pmat, rhs, qkb, qdec, kdec_t, glast = {}, {}, {}, {}, {}, {}, {}
    for h in range(GDN_HEADS):
        c0 = h * GDN_HEAD_DIM
        q_all = _l2norm(conv_q(c0, GDN_HEAD_DIM, gcw_ref, c0)) * (GDN_HEAD_DIM ** -0.5)
        k_all = _l2norm(conv_q(D_GDN + c0, GDN_HEAD_DIM, gcw_ref, D_GDN + c0))
        v_all = conv_q(2 * D_GDN + c0, GDN_HEAD_DIM, gcw_ref, 2 * D_GDN + c0)
        for c in range(n_chunks):
            rs = slice(c * GDN_CHUNK, (c + 1) * GDN_CHUNK)
            qh, kh, vh = q_all[rs], k_all[rs], v_all[rs]
            beta_c = beta[rs, L_BETA + h:L_BETA + h + 1]
            gc_c = gc[rs, L_GDN_G + h:L_GDN_G + h + 1]
            gc_r = gc_t[c][L_GDN_G + h:L_GDN_G + h + 1, :]
            gl = gc_c[GDN_CHUNK - 1:GDN_CHUNK, :]
            eg = jnp.exp(gc_c)
            decay = _masked_decay(gc_c, gc_r, causal_g)
            kb, qb = _bf(kh), _bf(qh)
            nmat[c, h] = jnp.where(strict_g, -(_dot_nt(kb, kb) * decay * beta_c), 0.0)
            qkb[c, h] = _bf(jnp.where(causal_g, _dot_nt(qb, kb) * decay, 0.0))
            rhs[c, h] = jnp.concatenate([vh * beta_c, kh * (beta_c * eg)], axis=-1)
            qdec[c, h] = qh * eg
            kdec_t[c, h] = _bf((kh * jnp.exp(gl - gc_c)).T)
            glast[c, h] = jnp.exp(gl)

    for key in keys:
        bb = _bf(nmat[key])
        pmat[key] = _dot(bb, bb)
    for _ in range(4):
        for key in keys:
            pb = _bf(pmat[key])
            r = _dot(jnp.concatenate([_bf(nmat[key]), pb], axis=0), pb)
            nmat[key] = nmat[key] + pmat[key] + r[:GDN_CHUNK]
            pmat[key] = r[GDN_CHUNK:]
    sol = {}
    for key in keys:
        nmat[key] = nmat[key] + pmat[key] + _dot(_bf(nmat[key]), _bf(pmat[key]))
    for key in keys:
        sol[key] = rhs[key] + _dot(_bf(nmat[key]), _bf(rhs[key]))

    for c in range(n_chunks):
        r0 = c * GDN_CHUNK
        for h in range(GDN_HEADS):
            key = (c, h)
            state = sg_ref[h]
            u, w = sol[key][:, :GDN_HEAD_DIM], sol[key][:, GDN_HEAD_DIM:]
            r1 = _dot(_bf(jnp.concatenate([w, qdec[key]], axis=0)), _bf(state))
            vnb = _bf(u - r1[:GDN_CHUNK])
            r2 = _dot(jnp.concatenate([qkb[key], kdec_t[key]], axis=0), vnb)
            sg_ref[h] = state * glast[key] + r2[GDN_CHUNK:]
            o = r1[GDN_CHUNK:] + r2[:GDN_CHUNK]
            c0 = h * GDN_HEAD_DIM
            z = proj_ref[r0:r0 + GDN_CHUNK, C_ZG + c0:C_ZG + c0 + GDN_HEAD_DIM]
            store_y(c0, _gdn_finish(o, z, gnorm_ref), r0)

    si = lax.broadcasted_iota(jnp.int32, (SSD_CHUNK, SSD_CHUNK), 0)
    sj = lax.broadcasted_iota(jnp.int32, (SSD_CHUNK, SSD_CHUNK), 1)
    causal_s = si >= sj
    lo_lanes = sj < SSD_HEAD_DIM
    lo_rows = si < SSD_HEAD_DIM
    ac_t = ac.T
    conv_x = functools.partial(_conv_silu_tile, ex_ref, hist)
    for g in range(SSD_GROUPS):
        bgb = _bf(conv_x(D_SSD + g * SSD_STATE, SSD_STATE, scw_ref, D_SSD + g * SSD_STATE, scb_ref))
        cgb = _bf(conv_x(D_SSD + (SSD_GROUPS + g) * SSD_STATE, SSD_STATE, scw_ref,
                         D_SSD + (SSD_GROUPS + g) * SSD_STATE, scb_ref))
        cb = _dot_nt(cgb, bgb)
        yz, ssq = [], None
        for pair in range(SSD_HEADS_PER_GROUP // 2):
            hd0 = g * SSD_HEADS_PER_GROUP + 2 * pair
            c0 = hd0 * SSD_HEAD_DIM
            la, lb = L_SSD_DT + hd0, L_SSD_DT + hd0 + 1
            xs2 = conv_x(c0, 2 * SSD_HEAD_DIM, scw_ref, c0, scb_ref)
            a2 = jnp.where(lo_lanes, ac[:, la:la + 1], ac[:, lb:lb + 1])
            xdt2 = xs2 * jnp.where(lo_lanes, sp[:, la:la + 1], sp[:, lb:lb + 1])
            alast2 = a2[tt - 1:tt, :]
            state2 = ss_ref[hd0:hd0 + 2].reshape(2 * SSD_HEAD_DIM, SSD_STATE)
            y = jnp.exp(a2) * _dot_nt(cgb, _bf(state2))
            seg_a = _masked_decay(ac[:, la:la + 1], ac_t[la:la + 1, :], causal_s)
            seg_b = _masked_decay(ac[:, lb:lb + 1], ac_t[lb:lb + 1, :], causal_s)
            y = y + _dot(_bf(cb * seg_a), _bf(jnp.where(lo_lanes, xdt2, 0.0)))
            y = y + _dot(_bf(cb * seg_b), _bf(jnp.where(lo_lanes, 0.0, xdt2)))
            upd = _dot_tn(_bf(xdt2 * jnp.exp(alast2 - a2)), bgb)
            elast = jnp.exp(jnp.where(lo_rows, ac[tt - 1:tt, la:la + 1], ac[tt - 1:tt, lb:lb + 1]))
            ss_ref[hd0:hd0 + 2] = (state2 * elast + upd).reshape(2, SSD_HEAD_DIM, SSD_STATE)
            z = proj_ref[:, C_ZS + c0:C_ZS + c0 + 2 * SSD_HEAD_DIM]
            t = (y + sd_ref[0:1, c0:c0 + 2 * SSD_HEAD_DIM] * xs2) * _silu(z)
            yz.append(t)
            s = jnp.sum(t * t, axis=-1, keepdims=True)
            ssq = s if ssq is None else ssq + s
        scale = lax.rsqrt(ssq * (1.0 / SSD_GROUP_DIM) + EPS)
        for pair, t in enumerate(yz):
            c0 = (g * SSD_HEADS_PER_GROUP + 2 * pair) * SSD_HEAD_DIM
            store_y(D_GDN + c0, t * scale * snorm_ref[0:1, c0:c0 + 2 * SSD_HEAD_DIM])

    pos = pos0 + i * tt + lax.broadcasted_iota(jnp.int32, (tt, 1), 0)
    for gi, win in enumerate(POOL_WINDOWS):
        sl = slice(gi * POOL_GROUP_DIM, (gi + 1) * POOL_GROUP_DIM)
        e = ep_ref[:, sl]
        acc, k = e, 1
        while k < win:
            acc = acc + pltpu.roll(acc, k, 0)
            k *= 2
        cnt = jnp.minimum(win, pos + 1).astype(F32)
        pooled = acc[phist:] / cnt - e[phist:]
        mixed = _dot(_bf(pooled), pw_ref[gi])
        z = proj_ref[:, C_ZP + gi * POOL_GROUP_DIM:C_ZP + (gi + 1) * POOL_GROUP_DIM]
        store_y(D_GDN + D_SSD + gi * POOL_GROUP_DIM, mixed * pscale_ref[0:1, sl] * _silu(z))

    eq_ref[0:hist, :] = eq_ref[tt:tt + hist, :]
    ex_ref[0:hist, :] = ex_ref[tt:tt + hist, :]
    ep_ref[0:phist, :] = ep_ref[tt:tt + phist, :]


def _layer_spec(shape):
    nd = len(shape)
    return lambda layer: pl.BlockSpec((None,) + tuple(shape[1:]), lambda *_: (layer,) + (0,) * (nd - 1))


def _param_specs(params, layer):
    return [_layer_spec(p.shape)(layer) for p in params]


def _layer_prompt_kernel(x_ref, nw_ref, w_ref, *refs, tiles_per_seq, n_steps, pos0):
    params = refs[:9]
    (y_odd_ref, y_even_ref, sg_out, ss_out, gcv_out, scv_out, pool_out,
     p_even, p_odd, sg_ref, ss_ref, eq_ref, ex_ref, ep_ref) = refs[9:]
    tt = PROMPT_TT
    g = pl.program_id(0)
    state = (sg_ref, ss_ref, eq_ref, ex_ref, ep_ref)

    def project(rows, p_ref):
        x = x_ref[rows, :]
        ms = jnp.mean(x * x, axis=-1, keepdims=True)
        p_ref[...] = _dot(_bf(x * lax.rsqrt(ms + EPS) * nw_ref[...]), w_ref[...])

    @pl.when(g == 0)
    def _():
        project(slice(0, tt), p_even)

    @pl.when(g > 0)
    def _():
        project(slice(0, tt), p_even)
        _mixer_tile(p_odd, y_odd_ref, lax.rem(2 * g - 1, tiles_per_seq), pos0, params, *state)
        sg_out[0] = sg_ref[...]
        ss_out[0] = ss_ref[...]
        gcv_out[0] = eq_ref[HIST + tt - (CONV_K - 1):HIST + tt, :]
        scv_out[0] = ex_ref[HIST + tt - (CONV_K - 1):HIST + tt, :]
        pool_out[0] = ep_ref[PHIST + tt - POOL_BUF:PHIST + tt, :]

    @pl.when(g < n_steps - 1)
    def _():
        i_even = lax.rem(2 * g, tiles_per_seq)

        @pl.when(i_even == 0)
        def _():
            _reset_prompt_state(*state)

        project(slice(tt, 2 * tt), p_odd)
        _mixer_tile(p_even, y_even_ref, i_even, pos0, params, *state)


def _layer_prompt(x, nw, w_packed, params, layer, batch, seq):
    tt = PROMPT_TT
    tiles_per_seq = seq // tt
    n_tiles = batch * tiles_per_seq
    assert tiles_per_seq % 2 == 0
    n_steps = n_tiles // 2 + 1

    def seq_of_odd_tile(g):
        return jnp.maximum(2 * g - 1, 0) // tiles_per_seq

    def state_spec(tail):
        return pl.BlockSpec((1,) + tail, lambda g: (seq_of_odd_tile(g),) + (0,) * len(tail))

    tails = ((GDN_HEADS, GDN_HEAD_DIM, GDN_HEAD_DIM), (SSD_HEADS, SSD_HEAD_DIM, SSD_STATE),
             (CONV_K - 1, 3 * D_GDN), (CONV_K - 1, SSD_CONV_CH), (POOL_BUF, D_POOL))
    return pl.pallas_call(
        functools.partial(_layer_prompt_kernel, tiles_per_seq=tiles_per_seq, n_steps=n_steps, pos0=0),
        grid=(n_steps,),
        in_specs=[
            pl.BlockSpec((2 * tt, D_MODEL), lambda g: (jnp.minimum(g, n_steps - 2), 0)),
            pl.BlockSpec((None, 1, D_MODEL), lambda g: (layer, 0, 0)),
            pl.BlockSpec((None, D_MODEL, D_PACK), lambda g: (layer, 0, 0), pipeline_mode=pl.Buffered(1)),
        ] + _param_specs(params, layer),
        out_specs=[
            pl.BlockSpec((tt, D_MODEL), lambda g: (jnp.maximum(g - 1, 0), 0)),
            pl.BlockSpec((tt, D_MODEL), lambda g: (jnp.minimum(g, n_steps - 2), 0)),
        ] + [state_spec(t) for t in tails],
        out_shape=[jax.ShapeDtypeStruct((n_tiles // 2 * tt, D_MODEL), BF16)] * 2
        + [jax.ShapeDtypeStruct((batch,) + t, F32) for t in tails],
        scratch_shapes=[
            pltpu.VMEM((tt, D_PACK), F32),
            pltpu.VMEM((tt, D_PACK), F32),
            pltpu.VMEM(tails[0], F32),
            pltpu.VMEM(tails[1], F32),
            pltpu.VMEM((HIST + tt, 3 * D_GDN), F32),
            pltpu.VMEM((HIST + tt, SSD_CONV_CH), F32),
            pltpu.VMEM((PHIST + tt, D_POOL), F32),
        ],
        compiler_params=pltpu.CompilerParams(
            dimension_semantics=("arbitrary",), vmem_limit_bytes=VMEM_LIMIT_PROJ),
        name="layer_prompt",
    )(x, nw, w_packed, *params)


def _mixer_sample_kernel(*refs, n_alias, pos0):
    proj_ref, sg_in, gcv_in, ss_in, scv_in, pool_in = refs[:6]
    (gp_ref, gcw_ref, gnorm_ref, scw_ref, scb_ref, sd_ref, snorm_ref, pw_ref, pscale_ref,
     y_ref, sg_out, gcv_out, ss_out, scv_out, pool_out) = refs[6 + n_alias:]
    nb, t_len = SAMPLE_NB, SAMPLE_T
    assert nb == SUBLANE
    rows = nb * t_len
    last = slice(rows - nb, rows)

    def tokens(c0, width):
        return [proj_ref[t, :, c0:c0 + width] for t in range(t_len)]

    def stack(parts):
        return jnp.concatenate(parts, axis=0)

    def conv(hist_ref, col0, cw, width, w_ref, b_ref=None):
        u = [hist_ref[j, :, cw:cw + width] for j in range(CONV_K - 1)] + tokens(col0 + cw, width)
        outs = []
        for t in range(t_len):
            acc = u[t] * w_ref[0:1, cw:cw + width]
            for j in range(1, CONV_K):
                acc = acc + u[t + j] * w_ref[j:j + 1, cw:cw + width]
            if b_ref is not None:
                acc = acc + b_ref[0:1, cw:cw + width]
            outs.append(_silu(acc))
        return stack(outs)

    beta, sp, gd = _gates(stack(tokens(C_SM, LANE)), gp_ref)
    parts = [gd[0:nb]]
    for t in range(1, t_len):
        parts.append(parts[-1] + gd[t * nb:(t + 1) * nb])
    cum = stack(parts)
    cum_t = cum.T

    ri = lax.broadcasted_iota(jnp.int32, (rows, rows), 0)
    rj = lax.broadcasted_iota(jnp.int32, (rows, rows), 1)
    same = (ri & (nb - 1)) == (rj & (nb - 1))
    causal, strict = same & (ri >= rj), same & (ri > rj)
    seq1 = lax.broadcasted_iota(jnp.int32, (rows, 1), 0) & (nb - 1)
    seq2 = lax.broadcasted_iota(jnp.int32, (2 * rows, 1), 0) & (nb - 1)

    def block_diag(x):
        return jnp.concatenate([jnp.where(seq1 == b, x, 0.0) for b in range(nb)], axis=1)

    def store_y(c0, val):
        for t in range(t_len):
            y_ref[t, :, c0:c0 + val.shape[1]] = val[t * nb:(t + 1) * nb]

    heads = range(GDN_HEADS)
    bmat, bb, rhs, qkb, qdec, kd_t, egl = {}, {}, {}, {}, {}, {}, {}
    for h in heads:
        c0 = h * GDN_HEAD_DIM
        qh = _l2norm(conv(gcv_in, C_QKV, c0, GDN_HEAD_DIM, gcw_ref)) * (GDN_HEAD_DIM ** -0.5)
        kh = _l2norm(conv(gcv_in, C_QKV, D_GDN + c0, GDN_HEAD_DIM, gcw_ref))
        vh = conv(gcv_in, C_QKV, 2 * D_GDN + c0, GDN_HEAD_DIM, gcw_ref)
        beta_c = beta[:, L_BETA + h:L_BETA + h + 1]
        gc_c = cum[:, L_GDN_G + h:L_GDN_G + h + 1]
        eg = jnp.exp(gc_c)
        decay = _masked_decay(gc_c, cum_t[L_GDN_G + h:L_GDN_G + h + 1, :], causal)
        kb, qb = _bf(kh), _bf(qh)
        bmat[h] = jnp.where(strict, -(_dot_nt(kb, kb) * decay * beta_c), 0.0)
        bb[h] = _bf(bmat[h])
        qkb[h] = _bf(jnp.where(causal, _dot_nt(qb, kb) * decay, 0.0))
        rhs[h] = jnp.concatenate([vh * beta_c, kh * (beta_c * eg)], axis=-1)
        qdec[h] = qh * eg
        gl = gc_c[last]
        kd_t[h] = _bf((kh * jnp.exp(stack([gl] * t_len) - gc_c)).T)
        egl[h] = jnp.exp(gl)
    p1 = {h: _dot(bb[h], bb[h]) for h in heads}
    nmat = {h: bmat[h] + p1[h] + _dot(bb[h], _bf(p1[h])) for h in heads}
    sol = {h: rhs[h] + _dot(_bf(nmat[h]), _bf(rhs[h])) for h in heads}
    acc = {}
    for h in heads:
        lhs = _bf(stack([sol[h][:, GDN_HEAD_DIM:], qdec[h]]))
        a = jnp.zeros((2 * rows, GDN_HEAD_DIM), F32)
        for b in range(nb):
            a = jnp.where(seq2 == b, _dot(lhs, _bf(sg_in[b, h])), a)
        acc[h] = a
    vn = {h: sol[h][:, :GDN_HEAD_DIM] - acc[h][:rows] for h in heads}
    o = {h: acc[h][rows:] + _dot(qkb[h], _bf(vn[h])) for h in heads}
    upd = {h: _dot(kd_t[h], _bf(block_diag(vn[h]))) for h in heads}
    for h in heads:
        c0 = h * GDN_HEAD_DIM
        store_y(c0, _gdn_finish(o[h], stack(tokens(C_ZG + c0, GDN_HEAD_DIM)), gnorm_ref))
        for b in range(nb):
            sg_out[b, h] = (sg_in[b, h] * egl[h][b:b + 1, :]
                            + upd[h][:, b * GDN_HEAD_DIM:(b + 1) * GDN_HEAD_DIM])

    lo_lanes = lax.broadcasted_iota(jnp.int32, (rows, 2 * SSD_HEAD_DIM), 1) < SSD_HEAD_DIM
    lo_rows = lax.broadcasted_iota(jnp.int32, (2 * SSD_HEAD_DIM, SSD_STATE), 0) < SSD_HEAD_DIM
    pairs = range(SSD_HEADS // 2)
    n_gp = SSD_HEADS_PER_GROUP // 2
    cgb, cb, bg_bd = {}, {}, {}
    for g in range(SSD_GROUPS):
        bg = conv(scv_in, C_XBC, D_SSD + g * SSD_STATE, SSD_STATE, scw_ref, scb_ref)
        cg = conv(scv_in, C_XBC, D_SSD + (SSD_GROUPS + g) * SSD_STATE, SSD_STATE, scw_ref, scb_ref)
        cgb[g] = _bf(cg)
        cb[g] = _dot_nt(cgb[g], _bf(bg))
        bg_bd[g] = _bf(block_diag(bg))
    xs2, a2, y, xin_t = {}, {}, {}, {}
    for p in pairs:
        g, c0 = p // n_gp, 2 * p * SSD_HEAD_DIM
        la, lb = L_SSD_DT + 2 * p, L_SSD_DT + 2 * p + 1
        xs2[p] = conv(scv_in, C_XBC, c0, 2 * SSD_HEAD_DIM, scw_ref, scb_ref)
        a2[p] = jnp.where(lo_lanes, cum[:, la:la + 1], cum[:, lb:lb + 1])
        xdt2 = xs2[p] * jnp.where(lo_lanes, sp[:, la:la + 1], sp[:, lb:lb + 1])
        seg_a = _masked_decay(cum[:, la:la + 1], cum_t[la:la + 1, :], causal)
        seg_b = _masked_decay(cum[:, lb:lb + 1], cum_t[lb:lb + 1, :], causal)
        y[p] = (_dot(_bf(cb[g] * seg_a), _bf(jnp.where(lo_lanes, xdt2, 0.0)))
                + _dot(_bf(cb[g] * seg_b), _bf(jnp.where(lo_lanes, 0.0, xdt2))))
        xin_t[p] = _bf((xdt2 * jnp.exp(stack([a2[p][last]] * t_len) - a2[p])).T)
    acc = {}
    for p in pairs:
        a = jnp.zeros((rows, 2 * SSD_HEAD_DIM), F32)
        for b in range(nb):
            state2 = ss_in[b, 2 * p:2 * p + 2].reshape(2 * SSD_HEAD_DIM, SSD_STATE)
            a = jnp.where(seq1 == b, _dot_nt(cgb[p // n_gp], _bf(state2)), a)
        acc[p] = a
    upd = {p: _dot(xin_t[p], bg_bd[p // n_gp]) for p in pairs}
    for p in pairs:
        la, lb = L_SSD_DT + 2 * p, L_SSD_DT + 2 * p + 1
        for b in range(nb):
            r = rows - nb + b
            elast = jnp.exp(jnp.where(lo_rows, cum[r:r + 1, la:la + 1], cum[r:r + 1, lb:lb + 1]))
            state2 = ss_in[b, 2 * p:2 * p + 2].reshape(2 * SSD_HEAD_DIM, SSD_STATE)
            ss_out[b, 2 * p:2 * p + 2] = (state2 * elast + upd[p][:, b * SSD_STATE:(b + 1) * SSD_STATE]).reshape(
                2, SSD_HEAD_DIM, SSD_STATE)
    for g in range(SSD_GROUPS):
        yz, ssq = {}, None
        for p in range(g * n_gp, (g + 1) * n_gp):
            c0 = 2 * p * SSD_HEAD_DIM
            z = stack(tokens(C_ZS + c0, 2 * SSD_HEAD_DIM))
            yt = y[p] + jnp.exp(a2[p]) * acc[p]
            yz[p] = (yt + sd_ref[0:1, c0:c0 + 2 * SSD_HEAD_DIM] * xs2[p]) * _silu(z)
            s = jnp.sum(yz[p] * yz[p], axis=-1, keepdims=True)
            ssq = s if ssq is None else ssq + s
        scale = lax.rsqrt(ssq * (1.0 / SSD_GROUP_DIM) + EPS)
        for p, t in yz.items():
            c0 = 2 * p * SSD_HEAD_DIM
            store_y(D_GDN + c0, t * scale * snorm_ref[0:1, c0:c0 + 2 * SSD_HEAD_DIM])

    for gi, win in enumerate(POOL_WINDOWS):
        sl = slice(gi * POOL_GROUP_DIM, (gi + 1) * POOL_GROUP_DIM)
        x = [pool_in[j, :, sl] for j in range(POOL_BUF)] + tokens(C_UP + gi * POOL_GROUP_DIM, POOL_GROUP_DIM)
        pooled = []
        for t in range(t_len):
            acc = x[POOL_BUF + t]
            for k in range(1, win):
                acc = acc + x[POOL_BUF + t - k]
            pooled.append(acc / float(min(win, pos0 + t + 1)) - x[POOL_BUF + t])
        mixed = _dot(_bf(stack(pooled)), pw_ref[gi])
        z = stack(tokens(C_ZP + gi * POOL_GROUP_DIM, POOL_GROUP_DIM))
        store_y(D_GDN + D_SSD + gi * POOL_GROUP_DIM, mixed * pscale_ref[0:1, sl] * _silu(z))

    for j in range(CONV_K - 1):
        t = t_len - (CONV_K - 1) + j
        gcv_out[j] = proj_ref[t, :, C_QKV:C_QKV + 3 * D_GDN]
        scv_out[j] = proj_ref[t, :, C_XBC:C_XBC + SSD_CONV_CH]
    for j in range(POOL_BUF):
        src = j + t_len
        pool_out[j] = pool_in[src] if src < POOL_BUF else proj_ref[src - POOL_BUF, :, C_UP:C_UP + D_POOL]


def _mixer_sample(proj, states, params, layer, n_seq, prev_outs):
    nb, t_len = SAMPLE_NB, SAMPLE_T
    assert CONV_K - 1 <= t_len and proj.shape == (t_len, n_seq, D_PACK)
    prev_outs = tuple(prev_outs)
    assert [p.shape for p in prev_outs] == [s.shape for s in states]
    n_state = len(states)

    def seq_spec(s):
        if s.shape[1] == n_seq:
            tail = tuple(s.shape[2:])
            return pl.BlockSpec((None, nb) + tail, lambda i: (layer, i) + (0,) * len(tail))
        assert s.shape[2] == n_seq and s.ndim == 4
        return pl.BlockSpec((None, s.shape[1], nb, s.shape[3]), lambda i: (layer, 0, i, 0))

    return pl.pallas_call(
        functools.partial(_mixer_sample_kernel, n_alias=len(prev_outs), pos0=PAST_LEN),
        grid=(n_seq // nb,),
        in_specs=[pl.BlockSpec((t_len, nb, D_PACK), lambda i: (0, i, 0))]
        + [seq_spec(s) for s in states]
        + [pl.BlockSpec(memory_space=pl.ANY)] * len(prev_outs)
        + _param_specs(params, layer),
        out_specs=[pl.BlockSpec((t_len, nb, D_MODEL), lambda i: (0, i, 0))] + [seq_spec(s) for s in states],
        out_shape=[jax.ShapeDtypeStruct((t_len, n_seq, D_MODEL), F32)]
        + [jax.ShapeDtypeStruct(s.shape, F32) for s in states],
        input_output_aliases={1 + n_state + k: 1 + k for k in range(len(prev_outs))},
        compiler_params=pltpu.CompilerParams(
            dimension_semantics=("parallel",), vmem_limit_bytes=VMEM_LIMIT),
        name="mixer_sample",
    )(proj, *states, *prev_outs, *params)


def _gate_params(gdn_a_log, gdn_dt_bias, ssd_a_log, ssd_dt_bias):
    def lanes(gdn, ssd):
        z = jnp.zeros((DEPTH, LANE), F32)
        z = z.at[:, L_GDN_G:L_GDN_G + GDN_HEADS].set(gdn)
        return z.at[:, L_SSD_DT:L_SSD_DT + SSD_HEADS].set(ssd)

    rows = jnp.stack([lanes(gdn_a_log, ssd_a_log), lanes(gdn_dt_bias, ssd_dt_bias)], axis=1)
    return jnp.concatenate([rows, jnp.zeros((DEPTH, SUBLANE - 2, LANE), F32)], axis=1)


def kernel(x_prompt, x_sample, state_gdn, state_gdn_conv, state_ssd, state_ssd_conv, state_pool, norm_w, w_in, gdn_conv_w, gdn_a_log, gdn_dt_bias, gdn_norm_w, ssd_conv_w, ssd_conv_b, ssd_a_log, ssd_dt_bias, ssd_d, ssd_norm_w, pool_w, pool_scale, w_out, final_norm_w):
    batch, seq, _ = x_prompt.shape
    n_seq, t_len, _ = x_sample.shape
    assert t_len == SAMPLE_T and seq % PROMPT_TT == 0 and n_seq % SAMPLE_NB == 0

    w_in_p = _pack_w_in(jnp.swapaxes(w_in, 1, 2))
    params = (
        _gate_params(gdn_a_log, gdn_dt_bias, ssd_a_log, ssd_dt_bias),
        gdn_conv_w,
        gdn_norm_w[:, None, :],
        ssd_conv_w,
        ssd_conv_b[:, None, :],
        jnp.repeat(ssd_d, SSD_HEAD_DIM, axis=-1)[:, None, :],
        ssd_norm_w[:, None, :],
        pool_w.astype(BF16),
        pool_scale[:, None, :],
    )
    nw = norm_w[:, None, :]
    fw = final_norm_w[None, :]
    states = (state_gdn, jnp.swapaxes(state_gdn_conv, 1, 2), state_ssd, jnp.swapaxes(state_ssd_conv, 1, 2),
              jnp.swapaxes(state_pool, 1, 2))

    xp = x_prompt.reshape(batch * seq, D_MODEL)
    xs = jnp.swapaxes(x_sample, 0, 1).reshape(t_len * n_seq, D_MODEL)
    zero_jobs = {0: (0,), 1: (2,), 2: (1, 3, 4)}
    zeros = [None] * len(states)
    new_p = [[] for _ in range(5)]
    for layer in range(DEPTH):
        final = fw if layer == DEPTH - 1 else None
        y_odd, y_even, gdn_p, ssd_p, gcv_p, scv_p, pool_p = _layer_prompt(xp, nw, w_in_p, params, layer, batch, seq)
        jobs = zero_jobs.get(layer, ())
        xp, *zs = _outproj((y_even, y_odd), w_out, xp, layer, 256 if jobs else 512, final,
                           zero_like=[states[k] for k in jobs], n_seq=n_seq)
        for k, z in zip(jobs, zs):
            zeros[k] = z
        for acc, st in zip(new_p, (gdn_p, gcv_p, ssd_p, scv_p, pool_p)):
            acc.append(st)

    new_s = zeros
    for layer in range(DEPTH):
        final = fw if layer == DEPTH - 1 else None
        proj_s = _inproj(xs, nw, w_in_p, layer, n_seq * t_len).reshape(t_len, n_seq, D_PACK)
        y_s, *new_s = _mixer_sample(proj_s, states, params, layer, n_seq, new_s)
        xs, = _outproj(y_s.reshape(t_len * n_seq, D_MODEL), w_out, xs, layer, n_seq * t_len, final)

    new_s = [a if a.shape[1] == n_seq else jnp.swapaxes(a, 1, 2) for a in new_s]
    y_prompt = xp.reshape(batch, seq, D_MODEL)
    y_sample = jnp.swapaxes(xs.reshape(t_len, n_seq, D_MODEL), 0, 1)
    return (y_prompt, y_sample) + tuple(jnp.stack(a) for a in new_p) + tuple(new_s)
```

```python
import functools

import jax
import jax.numpy as jnp
import numpy as np
from jax import lax
from jax.experimental import pallas as pl
from jax.experimental.pallas import tpu as pltpu

F32 = jnp.float32
BF16 = jnp.bfloat16

D_MODEL = 2048
DEPTH = 4
PAST_LEN = 16384
D_GDN = 768
D_SSD = 768
D_POOL = 512
CONV_K = 4
GDN_HEAD_DIM = 128
GDN_HEADS = 6
GDN_CHUNK = 64
SSD_HEAD_DIM = 64
SSD_HEADS = 12
SSD_STATE = 128
SSD_GROUPS = 2
SSD_HEADS_PER_GROUP = 6
SSD_GROUP_DIM = D_SSD // SSD_GROUPS
SSD_CHUNK = 128
SSD_CONV_CH = 1280
POOL_WINDOWS = (2, 4, 8, 16)
POOL_GROUP_DIM = 128
POOL_BUF = 15
IN_SIZES = (3 * D_GDN, D_GDN, GDN_HEADS, GDN_HEADS, D_SSD, SSD_CONV_CH, SSD_HEADS, D_POOL, D_POOL)
EPS = 1e-6

C_QKV = 0
C_ZG = 2304
C_ZS = 3072
C_XBC = 3840
C_UP = 5120
C_ZP = 5632
C_SM = 6144
D_PACK = 6400
L_BETA = 0
L_GDN_G = 6
L_SSD_DT = 16

LANE = 128
SUBLANE = 8
MXU_WIDTH = 256
VMEM_LIMIT = 48 * 1024 * 1024
VMEM_LIMIT_PROJ = 56 * 1024 * 1024

PROJ_TN = 5 * MXU_WIDTH
PACK_TK = 256
PROMPT_TT = 128
SAMPLE_NB = 8
SAMPLE_T = 4


def _dot(a, b):
    return jnp.dot(a, b, preferred_element_type=F32)


def _dot_nt(a, b):
    return lax.dot_general(a, b, (((1,), (1,)), ((), ())), preferred_element_type=F32)


def _dot_tn(a, b):
    return lax.dot_general(a, b, (((0,), (0,)), ((), ())), preferred_element_type=F32)


def _sigmoid(x):
    return 0.5 + 0.5 * jnp.tanh(0.5 * x)


def _silu(x):
    hx = 0.5 * x
    return hx + hx * jnp.tanh(hx)


def _softplus(x):
    return jnp.maximum(x, 0.0) + jnp.log(1.0 + jnp.exp(-jnp.abs(x)))


def _bf(x):
    return x.astype(BF16)


def _inproj_kernel(x_ref, nw_ref, w_ref, o_ref, h_ref):
    @pl.when(pl.program_id(1) == 0)
    def _():
        x = x_ref[...]
        ms = jnp.mean(x * x, axis=-1, keepdims=True)
        h_ref[...] = _bf(x * lax.rsqrt(ms + EPS) * nw_ref[...])

    o_ref[...] = _dot(h_ref[...], w_ref[...])


def _inproj(x, nw, w_packed, layer, tm):
    n = x.shape[0]
    return pl.pallas_call(
        _inproj_kernel,
        grid=(n // tm, D_PACK // PROJ_TN),
        in_specs=[
            pl.BlockSpec((tm, D_MODEL), lambda i, j: (i, 0)),
            pl.BlockSpec((None, 1, D_MODEL), lambda i, j: (layer, 0, 0)),
            pl.BlockSpec((None, D_MODEL, PROJ_TN), lambda i, j: (layer, 0, j)),
        ],
        out_specs=pl.BlockSpec((tm, PROJ_TN), lambda i, j: (i, j)),
        out_shape=jax.ShapeDtypeStruct((n, D_PACK), F32),
        scratch_shapes=[pltpu.VMEM((tm, D_MODEL), BF16)],
        compiler_params=pltpu.CompilerParams(
            dimension_semantics=("parallel", "arbitrary"), vmem_limit_bytes=VMEM_LIMIT_PROJ),
        name="inproj",
    )(x, nw, w_packed)


def _pack_kernel(w_ref, o_ref):
    offs = np.cumsum((0,) + IN_SIZES)
    tk = o_ref.shape[0]

    for seg, c_out in ((0, C_QKV), (1, C_ZG), (4, C_ZS), (5, C_XBC), (7, C_UP), (8, C_ZP)):
        o_ref[:, c_out:c_out + IN_SIZES[seg]] = _bf(w_ref[offs[seg]:offs[seg + 1], :].T)
    assert offs[3] == offs[2] + GDN_HEADS and L_GDN_G == L_BETA + GDN_HEADS and L_BETA == 0
    assert offs[2] % SUBLANE == 0 and offs[2] + LANE <= offs[-1]
    lane = lax.broadcasted_iota(jnp.int32, (tk, LANE), 1)
    gates = w_ref[offs[2]:offs[2] + LANE, :].T
    dt_row0 = offs[6] - offs[6] % SUBLANE
    dt_lane0 = offs[6] - dt_row0
    assert dt_row0 + LANE <= offs[-1] and dt_lane0 <= L_SSD_DT
    dts = pltpu.roll(w_ref[dt_row0:dt_row0 + LANE, :].T, L_SSD_DT - dt_lane0, 1)
    small = jnp.where(lane < 2 * GDN_HEADS, gates,
                      jnp.where((lane >= L_SSD_DT) & (lane < L_SSD_DT + SSD_HEADS), dts, 0.0))
    o_ref[:, C_SM:C_SM + LANE] = _bf(small)
    o_ref[:, C_SM + LANE:D_PACK] = jnp.zeros((tk, D_PACK - C_SM - LANE), BF16)


def _pack_w_in(w_in_t):
    d_in = w_in_t.shape[1]
    return pl.pallas_call(
        _pack_kernel,
        grid=(DEPTH, D_MODEL // PACK_TK),
        in_specs=[pl.BlockSpec((None, d_in, PACK_TK), lambda l, i: (l, 0, i))],
        out_specs=pl.BlockSpec((None, PACK_TK, D_PACK), lambda l, i: (l, i, 0)),
        out_shape=jax.ShapeDtypeStruct((DEPTH, D_MODEL, D_PACK), BF16),
        compiler_params=pltpu.CompilerParams(
            dimension_semantics=("parallel", "parallel"), vmem_limit_bytes=VMEM_LIMIT),
        name="pack_w_in",
    )(w_in_t)


def _outproj_kernel(*refs, n_y, final):
    y_refs, (w32_ref, x_ref), rest = refs[:n_y], refs[n_y:n_y + 2], refs[n_y + 2:]
    o_ref, w_ref = rest[-2], rest[-1]

    @pl.when(pl.program_id(0) == 0)
    def _():
        w_ref[...] = _bf(w32_ref[...])

    if n_y == 1:
        y = y_refs[0][...]
    else:
        tt = PROMPT_TT
        y = jnp.concatenate([r[k * tt:(k + 1) * tt, :] for k in range(y_refs[0].shape[0] // tt) for r in y_refs],
                            axis=0)
    x = x_ref[...] + _dot(_bf(y), w_ref[...])
    if final:
        ms = jnp.mean(x * x, axis=-1, keepdims=True)
        x = x * lax.rsqrt(ms + EPS) * rest[0][...]
    o_ref[...] = x


def _outproj(ys, w_out, x, layer, tm, final_w=None):
    n = x.shape[0]
    ys = tuple(ys) if isinstance(ys, (tuple, list)) else (ys,)

    in_specs = [pl.BlockSpec((tm // len(ys), D_MODEL), lambda i: (i, 0)) for _ in ys] + [
        pl.BlockSpec((None, D_MODEL, D_MODEL), lambda i: (layer, 0, 0), pipeline_mode=pl.Buffered(1)),
        pl.BlockSpec((tm, D_MODEL), lambda i: (i, 0)),
    ]
    args = [*ys, w_out, x]
    if final_w is not None:
        in_specs.append(pl.BlockSpec((1, D_MODEL), lambda i: (0, 0)))
        args.append(final_w)
    return pl.pallas_call(
        functools.partial(_outproj_kernel, n_y=len(ys), final=final_w is not None),
        grid=(n // tm,),
        in_specs=in_specs,
        out_specs=pl.BlockSpec((tm, D_MODEL), lambda i: (i, 0)),
        out_shape=jax.ShapeDtypeStruct((n, D_MODEL), F32),
        scratch_shapes=[pltpu.VMEM((D_MODEL, D_MODEL), BF16)],
        compiler_params=pltpu.CompilerParams(
            dimension_semantics=("arbitrary",), vmem_limit_bytes=VMEM_LIMIT_PROJ),
        name="outproj",
    )(*args)


def _conv_silu_tile(ext_ref, hist, c0, width, w_ref, wc0, b_ref=None):
    e = ext_ref[:, c0:c0 + width]
    acc = e * w_ref[0:1, wc0:wc0 + width]
    for j in range(1, CONV_K):
        acc = pltpu.roll(acc, 1, 0) + e * w_ref[j:j + 1, wc0:wc0 + width]
    acc = acc[hist:]
    if b_ref is not None:
        acc = acc + b_ref[0:1, wc0:wc0 + width]
    return _silu(acc)


def _l2norm(x):
    return x * lax.rsqrt(jnp.sum(x * x, axis=-1, keepdims=True) + EPS)


def _seg_cumsum(x, seg_len, row_in_seg):
    s = 1
    while s < seg_len:
        x = x + jnp.where(row_in_seg >= s, pltpu.roll(x, s, 0), 0.0)
        s *= 2
    return x


def _gates(small, gp_ref):
    sp = _softplus(small + gp_ref[1:2, :])
    gd = -jnp.exp(gp_ref[0:1, :]) * sp
    return _sigmoid(small), sp, gd


def _masked_decay(col, row, causal):
    return jnp.where(causal, jnp.exp(jnp.where(causal, col - row, 0.0)), 0.0)


def _gdn_finish(o, z, gnorm_ref):
    on = o * lax.rsqrt(jnp.mean(o * o, axis=-1, keepdims=True) + EPS) * gnorm_ref[0:1, :]
    return on * _silu(z)


HIST = SUBLANE
PHIST = 2 * SUBLANE


def _reset_prompt_state(sg_ref, ss_ref, eq_ref, ex_ref, ep_ref):
    sg_ref[...] = jnp.zeros(sg_ref.shape, F32)
    ss_ref[...] = jnp.zeros(ss_ref.shape, F32)
    eq_ref[0:HIST, :] = jnp.zeros((HIST, eq_ref.shape[1]), F32)
    ex_ref[0:HIST, :] = jnp.zeros((HIST, ex_ref.shape[1]), F32)
    ep_ref[0:PHIST, :] = jnp.zeros((PHIST, D_POOL), F32)


def _mixer_tile(proj_ref, y_ref, i, pos0, params, sg_ref, ss_ref, eq_ref, ex_ref, ep_ref):
    gp_ref, gcw_ref, gnorm_ref, scw_ref, scb_ref, sd_ref, snorm_ref, pw_ref, pscale_ref = params
    tt = PROMPT_TT
    n_chunks = tt // GDN_CHUNK
    hist, phist = HIST, PHIST

    eq_ref[hist:hist + tt, :] = proj_ref[:, C_QKV:C_QKV + 3 * D_GDN]
    ex_ref[hist:hist + tt, :] = proj_ref[:, C_XBC:C_XBC + SSD_CONV_CH]
    ep_ref[phist:phist + tt, :] = proj_ref[:, C_UP:C_UP + D_POOL]

    beta, sp, gd = _gates(proj_ref[:, C_SM:C_SM + LANE], gp_ref)
    row = lax.broadcasted_iota(jnp.int32, (tt, LANE), 0)
    gc = _seg_cumsum(gd, GDN_CHUNK, row & (GDN_CHUNK - 1))
    ac = _seg_cumsum(gd, SSD_CHUNK, row)

    def store_y(c0, val, r0=0):
        y_ref[r0:r0 + val.shape[0], c0:c0 + val.shape[1]] = val.astype(y_ref.dtype)

    ci = lax.broadcasted_iota(jnp.int32, (GDN_CHUNK, GDN_CHUNK), 0)
    cj = lax.broadcasted_iota(jnp.int32, (GDN_CHUNK, GDN_CHUNK), 1)
    causal_g, strict_g = ci >= cj, ci > cj
    gc_t = [gc[c * GDN_CHUNK:(c + 1) * GDN_CHUNK].T for c in range(n_chunks)]
    conv_q = functools.partial(_conv_silu_tile, eq_ref, hist)

    keys = [(c, h) for c in range(n_chunks) for h in range(GDN_HEADS)]
    nmat, pmat, rhs, qkb, qdec, kdec_t, glast = {}, {}, {}, {}, {}, {}, {}
    for h in range(GDN_HEADS):
        c0 = h * GDN_HEAD_DIM
        q_all = _l2norm(conv_q(c0, GDN_HEAD_DIM, gcw_ref, c0)) * (GDN_HEAD_DIM ** -0.5)
        k_all = _l2norm(conv_q(D_GDN + c0, GDN_HEAD_DIM, gcw_ref, D_GDN + c0))
        v_all = conv_q(2 * D_GDN + c0, GDN_HEAD_DIM, gcw_ref, 2 * D_GDN + c0)
        for c in range(n_chunks):
            rs = slice(c * GDN_CHUNK, (c + 1) * GDN_CHUNK)
            qh, kh, vh = q_all[rs], k_all[rs], v_all[rs]
            beta_c = beta[rs, L_BETA + h:L_BETA + h + 1]
            gc_c = gc[rs, L_GDN_G + h:L_GDN_G + h + 1]
            gc_r = gc_t[c][L_GDN_G + h:L_GDN_G + h + 1, :]
            gl = gc_c[GDN_CHUNK - 1:GDN_CHUNK, :]
            eg = jnp.exp(gc_c)
            decay = _masked_decay(gc_c, gc_r, causal_g)
            kb, qb = _bf(kh), _bf(qh)
            nmat[c, h] = jnp.where(strict_g, -(_dot_nt(kb, kb) * decay * beta_c), 0.0)
            qkb[c, h] = _bf(jnp.where(causal_g, _dot_nt(qb, kb) * decay, 0.0))
            rhs[c, h] = jnp.concatenate([vh * beta_c, kh * (beta_c * eg)], axis=-1)
            qdec[c, h] = qh * eg
            kdec_t[c, h] = _bf((kh * jnp.exp(gl - gc_c)).T)
            glast[c, h] = jnp.exp(gl)

    for key in keys:
        bb = _bf(nmat[key])
        pmat[key] = _dot(bb, bb)
    for _ in range(4):
        for key in keys:
            pb = _bf(pmat[key])
            r = _dot(jnp.concatenate([_bf(nmat[key]), pb], axis=0), pb)
            nmat[key] = nmat[key] + pmat[key] + r[:GDN_CHUNK]
            pmat[key] = r[GDN_CHUNK:]
    sol = {}
    for key in keys:
        nmat[key] = nmat[key] + pmat[key] + _dot(_bf(nmat[key]), _bf(pmat[key]))
    for key in keys:
        sol[key] = rhs[key] + _dot(_bf(nmat[key]), _bf(rhs[key]))

    for c in range(n_chunks):
        r0 = c * GDN_CHUNK
        for h in range(GDN_HEADS):
            key = (c, h)
            state = sg_ref[h]
            u, w = sol[key][:, :GDN_HEAD_DIM], sol[key][:, GDN_HEAD_DIM:]
            r1 = _dot(_bf(jnp.concatenate([w, qdec[key]], axis=0)), _bf(state))
            vnb = _bf(u - r1[:GDN_CHUNK])
            r2 = _dot(jnp.concatenate([qkb[key], kdec_t[key]], axis=0), vnb)
            sg_ref[h] = state * glast[key] + r2[GDN_CHUNK:]
            o = r1[GDN_CHUNK:] + r2[:GDN_CHUNK]
            c0 = h * GDN_HEAD_DIM
            z = proj_ref[r0:r0 + GDN_CHUNK, C_ZG + c0:C_ZG + c0 + GDN_HEAD_DIM]
            store_y(c0, _gdn_finish(o, z, gnorm_ref), r0)

    si = lax.broadcasted_iota(jnp.int32, (SSD_CHUNK, SSD_CHUNK), 0)
    sj = lax.broadcasted_iota(jnp.int32, (SSD_CHUNK, SSD_CHUNK), 1)
    causal_s = si >= sj
    lo_lanes = sj < SSD_HEAD_DIM
    lo_rows = si < SSD_HEAD_DIM
    ac_t = ac.T
    conv_x = functools.partial(_conv_silu_tile, ex_ref, hist)
    for g in range(SSD_GROUPS):
        bgb = _bf(conv_x(D_SSD + g * SSD_STATE, SSD_STATE, scw_ref, D_SSD + g * SSD_STATE, scb_ref))
        cgb = _bf(conv_x(D_SSD + (SSD_GROUPS + g) * SSD_STATE, SSD_STATE, scw_ref,
                         D_SSD + (SSD_GROUPS + g) * SSD_STATE, scb_ref))
        cb = _dot_nt(cgb, bgb)
        yz, ssq = [], None
        for pair in range(SSD_HEADS_PER_GROUP // 2):
            hd0 = g * SSD_HEADS_PER_GROUP + 2 * pair
            c0 = hd0 * SSD_HEAD_DIM
            la, lb = L_SSD_DT + hd0, L_SSD_DT + hd0 + 1
            xs2 = conv_x(c0, 2 * SSD_HEAD_DIM, scw_ref, c0, scb_ref)
            a2 = jnp.where(lo_lanes, ac[:, la:la + 1], ac[:, lb:lb + 1])
            xdt2 = xs2 * jnp.where(lo_lanes, sp[:, la:la + 1], sp[:, lb:lb + 1])
            alast2 = a2[tt - 1:tt, :]
            state2 = ss_ref[hd0:hd0 + 2].reshape(2 * SSD_HEAD_DIM, SSD_STATE)
            y = jnp.exp(a2) * _dot_nt(cgb, _bf(state2))
            seg_a = _masked_decay(ac[:, la:la + 1], ac_t[la:la + 1, :], causal_s)
            seg_b = _masked_decay(ac[:, lb:lb + 1], ac_t[lb:lb + 1, :], causal_s)
            y = y + _dot(_bf(cb * seg_a), _bf(jnp.where(lo_lanes, xdt2, 0.0)))
            y = y + _dot(_bf(cb * seg_b), _bf(jnp.where(lo_lanes, 0.0, xdt2)))
            upd = _dot_tn(_bf(xdt2 * jnp.exp(alast2 - a2)), bgb)
            elast = jnp.exp(jnp.where(lo_rows, ac[tt - 1:tt, la:la + 1], ac[tt - 1:tt, lb:lb + 1]))
            ss_ref[hd0:hd0 + 2] = (state2 * elast + upd).reshape(2, SSD_HEAD_DIM, SSD_STATE)
            z = proj_ref[:, C_ZS + c0:C_ZS + c0 + 2 * SSD_HEAD_DIM]
            t = (y + sd_ref[0:1, c0:c0 + 2 * SSD_HEAD_DIM] * xs2) * _silu(z)
            yz.append(t)
            s = jnp.sum(t * t, axis=-1, keepdims=True)
            ssq = s if ssq is None else ssq + s
        scale = lax.rsqrt(ssq * (1.0 / SSD_GROUP_DIM) + EPS)
        for pair, t in enumerate(yz):
            c0 = (g * SSD_HEADS_PER_GROUP + 2 * pair) * SSD_HEAD_DIM
            store_y(D_GDN + c0, t * scale * snorm_ref[0:1, c0:c0 + 2 * SSD_HEAD_DIM])

    pos = pos0 + i * tt + lax.broadcasted_iota(jnp.int32, (tt, 1), 0)
    for gi, win in enumerate(POOL_WINDOWS):
        sl = slice(gi * POOL_GROUP_DIM, (gi + 1) * POOL_GROUP_DIM)
        e = ep_ref[:, sl]
        acc, k = e, 1
        while k < win:
            acc = acc + pltpu.roll(acc, k, 0)
            k *= 2
        cnt = jnp.minimum(win, pos + 1).astype(F32)
        pooled = acc[phist:] / cnt - e[phist:]
        mixed = _dot(_bf(pooled), pw_ref[gi])
        z = proj_ref[:, C_ZP + gi * POOL_GROUP_DIM:C_ZP + (gi + 1) * POOL_GROUP_DIM]
        store_y(D_GDN + D_SSD + gi * POOL_GROUP_DIM, mixed * pscale_ref[0:1, sl] * _silu(z))

    eq_ref[0:hist, :] = eq_ref[tt:tt + hist, :]
    ex_ref[0:hist, :] = ex_ref[tt:tt + hist, :]
    ep_ref[0:phist, :] = ep_ref[tt:tt + phist, :]


def _layer_spec(shape):
    nd = len(shape)
    return lambda layer: pl.BlockSpec((None,) + tuple(shape[1:]), lambda *_: (layer,) + (0,) * (nd - 1))


def _param_specs(params, layer):
    return [_layer_spec(p.shape)(layer) for p in params]


def _layer_prompt_kernel(x_ref, nw_ref, w_ref, *refs, tiles_per_seq, n_steps, pos0, n_zero):
    params = refs[:9]
    (y_odd_ref, y_even_ref, sg_out, ss_out, gcv_out, scv_out, pool_out) = refs[9:16]
    z_outs = refs[16:16 + n_zero]
    (p_even, p_odd, sg_ref, ss_ref, eq_ref, ex_ref, ep_ref) = refs[16 + n_zero:23 + n_zero]
    z_bufs, z_sem = refs[23 + n_zero:23 + 2 * n_zero], refs[23 + 2 * n_zero:]
    tt = PROMPT_TT
    g = pl.program_id(0)
    state = (sg_ref, ss_ref, eq_ref, ex_ref, ep_ref)

    def zero_copies():
        copies = []
        for z_out, z_buf in zip(z_outs, z_bufs):
            rb = z_buf.shape[0]
            per_step = z_out.shape[0] // rb // (n_steps - 1)
            assert per_step * rb * (n_steps - 1) == z_out.shape[0]
            for c in range(per_step):
                row0 = pl.multiple_of((g * per_step + c) * rb, SUBLANE)
                copies.append(pltpu.make_async_copy(z_buf, z_out.at[pl.ds(row0, rb)], z_sem[0].at[len(copies)]))
        return copies

    if n_zero:
        @pl.when(g == 0)
        def _():
            for z_buf in z_bufs:
                z_buf[...] = jnp.zeros(z_buf.shape, F32)

        @pl.when(g < n_steps - 1)
        def _():
            for cp in zero_copies():
                cp.start()

    def project(rows, p_ref):
        x = x_ref[rows, :]
        ms = jnp.mean(x * x, axis=-1, keepdims=True)
        p_ref[...] = _dot(_bf(x * lax.rsqrt(ms + EPS) * nw_ref[...]), w_ref[...])

    @pl.when(g == 0)
    def _():
        project(slice(0, tt), p_even)

    @pl.when(g > 0)
    def _():
        project(slice(0, tt), p_even)
        _mixer_tile(p_odd, y_odd_ref, lax.rem(2 * g - 1, tiles_per_seq), pos0, params, *state)
        sg_out[0] = sg_ref[...]
        ss_out[0] = ss_ref[...]
        gcv_out[0] = eq_ref[HIST + tt - (CONV_K - 1):HIST + tt, :]
        scv_out[0] = ex_ref[HIST + tt - (CONV_K - 1):HIST + tt, :]
        pool_out[0] = ep_ref[PHIST + tt - POOL_BUF:PHIST + tt, :]

    @pl.when(g < n_steps - 1)
    def _():
        i_even = lax.rem(2 * g, tiles_per_seq)

        @pl.when(i_even == 0)
        def _():
            _reset_prompt_state(*state)

        project(slice(tt, 2 * tt), p_odd)
        _mixer_tile(p_even, y_even_ref, i_even, pos0, params, *state)

    if n_zero:
        @pl.when(g < n_steps - 1)
        def _():
            for cp in zero_copies():
                cp.wait()


ZERO_BLOCK_ROWS = {GDN_HEAD_DIM: 3072, 3 * D_GDN: 48, SSD_CONV_CH: 48, D_POOL: 240}


def _layer_prompt(x, nw, w_packed, params, layer, batch, seq, zero_shapes=()):
    tt = PROMPT_TT
    tiles_per_seq = seq // tt
    n_tiles = batch * tiles_per_seq
    assert tiles_per_seq % 2 == 0
    n_steps = n_tiles // 2 + 1
    z_blocks = [(ZERO_BLOCK_ROWS[c], c) for _, c in zero_shapes]
    n_copies = sum(r // rb // (n_steps - 1) for (r, _), (rb, _) in zip(zero_shapes, z_blocks))

    def seq_of_odd_tile(g):
        return jnp.maximum(2 * g - 1, 0) // tiles_per_seq

    def state_spec(tail):
        return pl.BlockSpec((1,) + tail, lambda g: (seq_of_odd_tile(g),) + (0,) * len(tail))

    tails = ((GDN_HEADS, GDN_HEAD_DIM, GDN_HEAD_DIM), (SSD_HEADS, SSD_HEAD_DIM, SSD_STATE),
             (CONV_K - 1, 3 * D_GDN), (CONV_K - 1, SSD_CONV_CH), (POOL_BUF, D_POOL))
    return pl.pallas_call(
        functools.partial(_layer_prompt_kernel, tiles_per_seq=tiles_per_seq, n_steps=n_steps, pos0=0,
                          n_zero=len(zero_shapes)),
        grid=(n_steps,),
        in_specs=[
            pl.BlockSpec((2 * tt, D_MODEL), lambda g: (jnp.minimum(g, n_steps - 2), 0)),
            pl.BlockSpec((None, 1, D_MODEL), lambda g: (layer, 0, 0)),
            pl.BlockSpec((None, D_MODEL, D_PACK), lambda g: (layer, 0, 0), pipeline_mode=pl.Buffered(1)),
        ] + _param_specs(params, layer),
        out_specs=[
            pl.BlockSpec((tt, D_MODEL), lambda g: (jnp.maximum(g - 1, 0), 0)),
            pl.BlockSpec((tt, D_MODEL), lambda g: (jnp.minimum(g, n_steps - 2), 0)),
        ] + [state_spec(t) for t in tails] + [pl.BlockSpec(memory_space=pl.ANY)] * len(zero_shapes),
        out_shape=[jax.ShapeDtypeStruct((n_tiles // 2 * tt, D_MODEL), BF16)] * 2
        + [jax.ShapeDtypeStruct((batch,) + t, F32) for t in tails]
        + [jax.ShapeDtypeStruct(s, F32) for s in zero_shapes],
        scratch_shapes=[
            pltpu.VMEM((tt, D_PACK), F32),
            pltpu.VMEM((tt, D_PACK), F32),
            pltpu.VMEM(tails[0], F32),
            pltpu.VMEM(tails[1], F32),
            pltpu.VMEM((HIST + tt, 3 * D_GDN), F32),
            pltpu.VMEM((HIST + tt, SSD_CONV_CH), F32),
            pltpu.VMEM((PHIST + tt, D_POOL), F32),
        ] + [pltpu.VMEM(b, F32) for b in z_blocks]
        + ([pltpu.SemaphoreType.DMA((n_copies,))] if zero_shapes else []),
        compiler_params=pltpu.CompilerParams(
            dimension_semantics=("arbitrary",), vmem_limit_bytes=VMEM_LIMIT_PROJ),
        name="layer_prompt",
    )(x, nw, w_packed, *params)


def _mixer_sample_kernel(*refs, n_alias, pos0):
    proj_ref, sg_in, gcv_in, ss_in, scv_in, pool_in = refs[:6]
    (gp_ref, gcw_ref, gnorm_ref, scw_ref, scb_ref, sd_ref, snorm_ref, pw_ref, pscale_ref,
     y_ref, sg_out, gcv_out, ss_out, scv_out, pool_out) = refs[6 + n_alias:]
    nb, t_len = SAMPLE_NB, SAMPLE_T
    assert nb == SUBLANE
    rows = nb * t_len
    last = slice(rows - nb, rows)

    def tokens(c0, width):
        return [proj_ref[t, :, c0:c0 + width] for t in range(t_len)]

    def stack(parts):
        return jnp.concatenate(parts, axis=0)

    def conv(hist_ref, col0, cw, width, w_ref, b_ref=None):
        u = [hist_ref[j, :, cw:cw + width] for j in range(CONV_K - 1)] + tokens(col0 + cw, width)
        outs = []
        for t in range(t_len):
            acc = u[t] * w_ref[0:1, cw:cw + width]
            for j in range(1, CONV_K):
                acc = acc + u[t + j] * w_ref[j:j + 1, cw:cw + width]
            if b_ref is not None:
                acc = acc + b_ref[0:1, cw:cw + width]
            outs.append(_silu(acc))
        return stack(outs)

    beta, sp, gd = _gates(stack(tokens(C_SM, LANE)), gp_ref)
    parts = [gd[0:nb]]
    for t in range(1, t_len):
        parts.append(parts[-1] + gd[t * nb:(t + 1) * nb])
    cum = stack(parts)
    cum_t = cum.T

    ri = lax.broadcasted_iota(jnp.int32, (rows, rows), 0)
    rj = lax.broadcasted_iota(jnp.int32, (rows, rows), 1)
    same = (ri & (nb - 1)) == (rj & (nb - 1))
    causal, strict = same & (ri >= rj), same & (ri > rj)
    seq1 = lax.broadcasted_iota(jnp.int32, (rows, 1), 0) & (nb - 1)
    seq2 = lax.broadcasted_iota(jnp.int32, (2 * rows, 1), 0) & (nb - 1)

    def block_diag(x):
        return jnp.concatenate([jnp.where(seq1 == b, x, 0.0) for b in range(nb)], axis=1)

    def store_y(c0, val):
        for t in range(t_len):
            y_ref[t, :, c0:c0 + val.shape[1]] = val[t * nb:(t + 1) * nb]

    heads = range(GDN_HEADS)
    bmat, bb, rhs, qkb, qdec, kd_t, egl = {}, {}, {}, {}, {}, {}, {}
    for h in heads:
        c0 = h * GDN_HEAD_DIM
        qh = _l2norm(conv(gcv_in, C_QKV, c0, GDN_HEAD_DIM, gcw_ref)) * (GDN_HEAD_DIM ** -0.5)
        kh = _l2norm(conv(gcv_in, C_QKV, D_GDN + c0, GDN_HEAD_DIM, gcw_ref))
        vh = conv(gcv_in, C_QKV, 2 * D_GDN + c0, GDN_HEAD_DIM, gcw_ref)
        beta_c = beta[:, L_BETA + h:L_BETA + h + 1]
        gc_c = cum[:, L_GDN_G + h:L_GDN_G + h + 1]
        eg = jnp.exp(gc_c)
        decay = _masked_decay(gc_c, cum_t[L_GDN_G + h:L_GDN_G + h + 1, :], causal)
        kb, qb = _bf(kh), _bf(qh)
        bmat[h] = jnp.where(strict, -(_dot_nt(kb, kb) * decay * beta_c), 0.0)
        bb[h] = _bf(bmat[h])
        qkb[h] = _bf(jnp.where(causal, _dot_nt(qb, kb) * decay, 0.0))
        rhs[h] = jnp.concatenate([vh * beta_c, kh * (beta_c * eg)], axis=-1)
        qdec[h] = qh * eg
        gl = gc_c[last]
        kd_t[h] = _bf((kh * jnp.exp(stack([gl] * t_len) - gc_c)).T)
        egl[h] = jnp.exp(gl)
    p1 = {h: _dot(bb[h], bb[h]) for h in heads}
    nmat = {h: bmat[h] + p1[h] + _dot(bb[h], _bf(p1[h])) for h in heads}
    sol = {h: rhs[h] + _dot(_bf(nmat[h]), _bf(rhs[h])) for h in heads}
    acc = {}
    for h in heads:
        lhs = _bf(stack([sol[h][:, GDN_HEAD_DIM:], qdec[h]]))
        a = jnp.zeros((2 * rows, GDN_HEAD_DIM), F32)
        for b in range(nb):
            a = jnp.where(seq2 == b, _dot(lhs, _bf(sg_in[b, h])), a)
        acc[h] = a
    vn = {h: sol[h][:, :GDN_HEAD_DIM] - acc[h][:rows] for h in heads}
    o = {h: acc[h][rows:] + _dot(qkb[h], _bf(vn[h])) for h in heads}
    upd = {h: _dot(kd_t[h], _bf(block_diag(vn[h]))) for h in heads}
    for h in heads:
        c0 = h * GDN_HEAD_DIM
        store_y(c0, _gdn_finish(o[h], stack(tokens(C_ZG + c0, GDN_HEAD_DIM)), gnorm_ref))
        for b in range(nb):
            sg_out[b, h] = (sg_in[b, h] * egl[h][b:b + 1, :]
                            + upd[h][:, b * GDN_HEAD_DIM:(b + 1) * GDN_HEAD_DIM])

    lo_lanes = lax.broadcasted_iota(jnp.int32, (rows, 2 * SSD_HEAD_DIM), 1) < SSD_HEAD_DIM
    lo_rows = lax.broadcasted_iota(jnp.int32, (2 * SSD_HEAD_DIM, SSD_STATE), 0) < SSD_HEAD_DIM
    pairs = range(SSD_HEADS // 2)
    n_gp = SSD_HEADS_PER_GROUP // 2
    cgb, cb, bg_bd = {}, {}, {}
    for g in range(SSD_GROUPS):
        bg = conv(scv_in, C_XBC, D_SSD + g * SSD_STATE, SSD_STATE, scw_ref, scb_ref)
        cg = conv(scv_in, C_XBC, D_SSD + (SSD_GROUPS + g) * SSD_STATE, SSD_STATE, scw_ref, scb_ref)
        cgb[g] = _bf(cg)
        cb[g] = _dot_nt(cgb[g], _bf(bg))
        bg_bd[g] = _bf(block_diag(bg))
    xs2, a2, y, xin_t = {}, {}, {}, {}
    for p in pairs:
        g, c0 = p // n_gp, 2 * p * SSD_HEAD_DIM
        la, lb = L_SSD_DT + 2 * p, L_SSD_DT + 2 * p + 1
        xs2[p] = conv(scv_in, C_XBC, c0, 2 * SSD_HEAD_DIM, scw_ref, scb_ref)
        a2[p] = jnp.where(lo_lanes, cum[:, la:la + 1], cum[:, lb:lb + 1])
        xdt2 = xs2[p] * jnp.where(lo_lanes, sp[:, la:la + 1], sp[:, lb:lb + 1])
        seg_a = _masked_decay(cum[:, la:la + 1], cum_t[la:la + 1, :], causal)
        seg_b = _masked_decay(cum[:, lb:lb + 1], cum_t[lb:lb + 1, :], causal)
        y[p] = (_dot(_bf(cb[g] * seg_a), _bf(jnp.where(lo_lanes, xdt2, 0.0)))
                + _dot(_bf(cb[g] * seg_b), _bf(jnp.where(lo_lanes, 0.0, xdt2))))
        xin_t[p] = _bf((xdt2 * jnp.exp(stack([a2[p][last]] * t_len) - a2[p])).T)
    acc = {}
    for p in pairs:
        a = jnp.zeros((rows, 2 * SSD_HEAD_DIM), F32)
        for b in range(nb):
            state2 = ss_in[b, 2 * p:2 * p + 2].reshape(2 * SSD_HEAD_DIM, SSD_STATE)
            a = jnp.where(seq1 == b, _dot_nt(cgb[p // n_gp], _bf(state2)), a)
        acc[p] = a
    upd = {p: _dot(xin_t[p], bg_bd[p // n_gp]) for p in pairs}
    for p in pairs:
        la, lb = L_SSD_DT + 2 * p, L_SSD_DT + 2 * p + 1
        for b in range(nb):
            r = rows - nb + b
            elast = jnp.exp(jnp.where(lo_rows, cum[r:r + 1, la:la + 1], cum[r:r + 1, lb:lb + 1]))
            state2 = ss_in[b, 2 * p:2 * p + 2].reshape(2 * SSD_HEAD_DIM, SSD_STATE)
            ss_out[b, 2 * p:2 * p + 2] = (state2 * elast + upd[p][:, b * SSD_STATE:(b + 1) * SSD_STATE]).reshape(
                2, SSD_HEAD_DIM, SSD_STATE)
    for g in range(SSD_GROUPS):
        yz, ssq = {}, None
        for p in range(g * n_gp, (g + 1) * n_gp):
            c0 = 2 * p * SSD_HEAD_DIM
            z = stack(tokens(C_ZS + c0, 2 * SSD_HEAD_DIM))
            yt = y[p] + jnp.exp(a2[p]) * acc[p]
            yz[p] = (yt + sd_ref[0:1, c0:c0 + 2 * SSD_HEAD_DIM] * xs2[p]) * _silu(z)
            s = jnp.sum(yz[p] * yz[p], axis=-1, keepdims=True)
            ssq = s if ssq is None else ssq + s
        scale = lax.rsqrt(ssq * (1.0 / SSD_GROUP_DIM) + EPS)
        for p, t in yz.items():
            c0 = 2 * p * SSD_HEAD_DIM
            store_y(D_GDN + c0, t * scale * snorm_ref[0:1, c0:c0 + 2 * SSD_HEAD_DIM])

    for gi, win in enumerate(POOL_WINDOWS):
        sl = slice(gi * POOL_GROUP_DIM, (gi + 1) * POOL_GROUP_DIM)
        x = [pool_in[j, :, sl] for j in range(POOL_BUF)] + tokens(C_UP + gi * POOL_GROUP_DIM, POOL_GROUP_DIM)
        pooled = []
        for t in range(t_len):
            acc = x[POOL_BUF + t]
            for k in range(1, win):
                acc = acc + x[POOL_BUF + t - k]
            pooled.append(acc / float(min(win, pos0 + t + 1)) - x[POOL_BUF + t])
        mixed = _dot(_bf(stack(pooled)), pw_ref[gi])
        z = stack(tokens(C_ZP + gi * POOL_GROUP_DIM, POOL_GROUP_DIM))
        store_y(D_GDN + D_SSD + gi * POOL_GROUP_DIM, mixed * pscale_ref[0:1, sl] * _silu(z))

    for j in range(CONV_K - 1):
        t = t_len - (CONV_K - 1) + j
        gcv_out[j] = proj_ref[t, :, C_QKV:C_QKV + 3 * D_GDN]
        scv_out[j] = proj_ref[t, :, C_XBC:C_XBC + SSD_CONV_CH]
    for j in range(POOL_BUF):
        src = j + t_len
        pool_out[j] = pool_in[src] if src < POOL_BUF else proj_ref[src - POOL_BUF, :, C_UP:C_UP + D_POOL]


def _mixer_sample(proj, states, params, layer, n_seq, prev_outs):
    nb, t_len = SAMPLE_NB, SAMPLE_T
    assert CONV_K - 1 <= t_len and proj.shape == (t_len, n_seq, D_PACK)
    prev_outs = tuple(prev_outs)
    assert [p.shape for p in prev_outs] == [s.shape for s in states]
    n_state = len(states)

    def seq_spec(s):
        if s.shape[1] == n_seq:
            tail = tuple(s.shape[2:])
            return pl.BlockSpec((None, nb) + tail, lambda i: (layer, i) + (0,) * len(tail))
        assert s.shape[2] == n_seq and s.ndim == 4
        return pl.BlockSpec((None, s.shape[1], nb, s.shape[3]), lambda i: (layer, 0, i, 0))

    return pl.pallas_call(
        functools.partial(_mixer_sample_kernel, n_alias=len(prev_outs), pos0=PAST_LEN),
        grid=(n_seq // nb,),
        in_specs=[pl.BlockSpec((t_len, nb, D_PACK), lambda i: (0, i, 0))]
        + [seq_spec(s) for s in states]
        + [pl.BlockSpec(memory_space=pl.ANY)] * len(prev_outs)
        + _param_specs(params, layer),
        out_specs=[pl.BlockSpec((t_len, nb, D_MODEL), lambda i: (0, i, 0))] + [seq_spec(s) for s in states],
        out_shape=[jax.ShapeDtypeStruct((t_len, n_seq, D_MODEL), F32)]
        + [jax.ShapeDtypeStruct(s.shape, F32) for s in states],
        input_output_aliases={1 + n_state + k: 1 + k for k in range(len(prev_outs))},
        compiler_params=pltpu.CompilerParams(
            dimension_semantics=("parallel",), vmem_limit_bytes=VMEM_LIMIT),
        name="mixer_sample",
    )(proj, *states, *prev_outs, *params)


def _gate_params(gdn_a_log, gdn_dt_bias, ssd_a_log, ssd_dt_bias):
    def lanes(gdn, ssd):
        z = jnp.zeros((DEPTH, LANE), F32)
        z = z.at[:, L_GDN_G:L_GDN_G + GDN_HEADS].set(gdn)
        return z.at[:, L_SSD_DT:L_SSD_DT + SSD_HEADS].set(ssd)

    rows = jnp.stack([lanes(gdn_a_log, ssd_a_log), lanes(gdn_dt_bias, ssd_dt_bias)], axis=1)
    return jnp.concatenate([rows, jnp.zeros((DEPTH, SUBLANE - 2, LANE), F32)], axis=1)


def kernel(x_prompt, x_sample, state_gdn, state_gdn_conv, state_ssd, state_ssd_conv, state_pool, norm_w, w_in, gdn_conv_w, gdn_a_log, gdn_dt_bias, gdn_norm_w, ssd_conv_w, ssd_conv_b, ssd_a_log, ssd_dt_bias, ssd_d, ssd_norm_w, pool_w, pool_scale, w_out, final_norm_w):
    batch, seq, _ = x_prompt.shape
    n_seq, t_len, _ = x_sample.shape
    assert t_len == SAMPLE_T and seq % PROMPT_TT == 0 and n_seq % SAMPLE_NB == 0

    w_in_p = _pack_w_in(jnp.swapaxes(w_in, 1, 2))
    params = (
        _gate_params(gdn_a_log, gdn_dt_bias, ssd_a_log, ssd_dt_bias),
        gdn_conv_w,
        gdn_norm_w[:, None, :],
        ssd_conv_w,
        ssd_conv_b[:, None, :],
        jnp.repeat(ssd_d, SSD_HEAD_DIM, axis=-1)[:, None, :],
        ssd_norm_w[:, None, :],
        pool_w.astype(BF16),
        pool_scale[:, None, :],
    )
    nw = norm_w[:, None, :]
    fw = final_norm_w[None, :]
    states = (state_gdn, jnp.swapaxes(state_gdn_conv, 1, 2), state_ssd, jnp.swapaxes(state_ssd_conv, 1, 2),
              jnp.swapaxes(state_pool, 1, 2))

    xp = x_prompt.reshape(batch * seq, D_MODEL)
    xs = jnp.swapaxes(x_sample, 0, 1).reshape(t_len * n_seq, D_MODEL)
    new_p = [[] for _ in range(5)]
    new_s = None
    for layer in range(DEPTH):
        final = fw if layer == DEPTH - 1 else None

        zero_shapes = [(s.size // s.shape[-1], s.shape[-1]) for s in states] if layer == 0 else []
        (y_odd, y_even, gdn_p, ssd_p, gcv_p, scv_p, pool_p, *zeros) = _layer_prompt(
            xp, nw, w_in_p, params, layer, batch, seq, zero_shapes)
        xp = _outproj((y_even, y_odd), w_out, xp, layer, 512, final)
        for acc, st in zip(new_p, (gdn_p, gcv_p, ssd_p, scv_p, pool_p)):
            acc.append(st)
        if layer == 0:
            new_s = [z.reshape(s.shape) for z, s in zip(zeros, states)]

        proj_s = _inproj(xs, nw, w_in_p, layer, n_seq * t_len).reshape(t_len, n_seq, D_PACK)
        y_s, *new_s = _mixer_sample(proj_s, states, params, layer, n_seq, new_s)
        xs = _outproj(y_s.reshape(t_len * n_seq, D_MODEL), w_out, xs, layer, n_seq * t_len, final)

    new_s = [a if a.shape[1] == n_seq else jnp.swapaxes(a, 1, 2) for a in new_s]
    y_prompt = xp.reshape(batch, seq, D_MODEL)
    y_sample = jnp.swapaxes(xs.reshape(t_len, n_seq, D_MODEL), 0, 1)
    return (y_prompt, y_sample) + tuple(jnp.stack(a) for a in new_p) + tuple(new_s)
```

```python
import functools

import jax
import jax.numpy as jnp
import numpy as np
from jax import lax
from jax.experimental import pallas as pl
from jax.experimental.pallas import tpu as pltpu

F32 = jnp.float32
BF16 = jnp.bfloat16

D_MODEL = 2048
DEPTH = 4
PAST_LEN = 16384
D_GDN = 768
D_SSD = 768
D_POOL = 512
CONV_K = 4
GDN_HEAD_DIM = 128
GDN_HEADS = 6
GDN_CHUNK = 64
SSD_HEAD_DIM = 64
SSD_HEADS = 12
SSD_STATE = 128
SSD_GROUPS = 2
SSD_HEADS_PER_GROUP = 6
SSD_GROUP_DIM = D_SSD // SSD_GROUPS
SSD_CHUNK = 128
SSD_CONV_CH = 1280
POOL_WINDOWS = (2, 4, 8, 16)
POOL_GROUP_DIM = 128
POOL_BUF = 15
IN_SIZES = (3 * D_GDN, D_GDN, GDN_HEADS, GDN_HEADS, D_SSD, SSD_CONV_CH, SSD_HEADS, D_POOL, D_POOL)
EPS = 1e-6

C_QKV = 0
C_ZG = 2304
C_ZS = 3072
C_XBC = 3840
C_UP = 5120
C_ZP = 5632
C_SM = 6144
D_PACK = 6400
L_BETA = 0
L_GDN_G = 6
L_SSD_DT = 16

LANE = 128
SUBLANE = 8
MXU_WIDTH = 256
VMEM_LIMIT = 48 * 1024 * 1024
VMEM_LIMIT_PROJ = 56 * 1024 * 1024

PROJ_TN = 5 * MXU_WIDTH
PACK_TK = 256
PROMPT_TT = 128
SAMPLE_NB = 8
SAMPLE_T = 4


def _dot(a, b):
    return jnp.dot(a, b, preferred_element_type=F32)


def _dot_nt(a, b):
    return lax.dot_general(a, b, (((1,), (1,)), ((), ())), preferred_element_type=F32)


def _dot_tn(a, b):
    return lax.dot_general(a, b, (((0,), (0,)), ((), ())), preferred_element_type=F32)


def _sigmoid(x):
    return 0.5 + 0.5 * jnp.tanh(0.5 * x)


def _silu(x):
    hx = 0.5 * x
    return hx + hx * jnp.tanh(hx)


def _softplus(x):
    return jnp.maximum(x, 0.0) + jnp.log(1.0 + jnp.exp(-jnp.abs(x)))


def _bf(x):
    return x.astype(BF16)


def _inproj_kernel(x_ref, nw_ref, w_ref, o_ref, h_ref):
    @pl.when(pl.program_id(1) == 0)
    def _():
        x = x_ref[...]
        ms = jnp.mean(x * x, axis=-1, keepdims=True)
        h_ref[...] = _bf(x * lax.rsqrt(ms + EPS) * nw_ref[...])

    o_ref[...] = _dot(h_ref[...], w_ref[...])


def _inproj(x, nw, w_packed, layer, tm):
    n = x.shape[0]
    return pl.pallas_call(
        _inproj_kernel,
        grid=(n // tm, D_PACK // PROJ_TN),
        in_specs=[
            pl.BlockSpec((tm, D_MODEL), lambda i, j: (i, 0)),
            pl.BlockSpec((None, 1, D_MODEL), lambda i, j: (layer, 0, 0)),
            pl.BlockSpec((None, D_MODEL, PROJ_TN), lambda i, j: (layer, 0, j)),
        ],
        out_specs=pl.BlockSpec((tm, PROJ_TN), lambda i, j: (i, j)),
        out_shape=jax.ShapeDtypeStruct((n, D_PACK), F32),
        scratch_shapes=[pltpu.VMEM((tm, D_MODEL), BF16)],
        compiler_params=pltpu.CompilerParams(
            dimension_semantics=("parallel", "arbitrary"), vmem_limit_bytes=VMEM_LIMIT_PROJ),
        name="inproj",
    )(x, nw, w_packed)


def _pack_kernel(w_ref, o_ref):
    offs = np.cumsum((0,) + IN_SIZES)
    tk = o_ref.shape[0]

    for seg, c_out in ((0, C_QKV), (1, C_ZG), (4, C_ZS), (5, C_XBC), (7, C_UP), (8, C_ZP)):
        o_ref[:, c_out:c_out + IN_SIZES[seg]] = _bf(w_ref[offs[seg]:offs[seg + 1], :].T)
    assert offs[3] == offs[2] + GDN_HEADS and L_GDN_G == L_BETA + GDN_HEADS and L_BETA == 0
    assert offs[2] % SUBLANE == 0 and offs[2] + LANE <= offs[-1]
    lane = lax.broadcasted_iota(jnp.int32, (tk, LANE), 1)
    gates = w_ref[offs[2]:offs[2] + LANE, :].T
    dt_row0 = offs[6] - offs[6] % SUBLANE
    dt_lane0 = offs[6] - dt_row0
    assert dt_row0 + LANE <= offs[-1] and dt_lane0 <= L_SSD_DT
    dts = pltpu.roll(w_ref[dt_row0:dt_row0 + LANE, :].T, L_SSD_DT - dt_lane0, 1)
    small = jnp.where(lane < 2 * GDN_HEADS, gates,
                      jnp.where((lane >= L_SSD_DT) & (lane < L_SSD_DT + SSD_HEADS), dts, 0.0))
    o_ref[:, C_SM:C_SM + LANE] = _bf(small)
    o_ref[:, C_SM + LANE:D_PACK] = jnp.zeros((tk, D_PACK - C_SM - LANE), BF16)


def _pack_w_in(w_in_t):
    d_in = w_in_t.shape[1]
    return pl.pallas_call(
        _pack_kernel,
        grid=(DEPTH, D_MODEL // PACK_TK),
        in_specs=[pl.BlockSpec((None, d_in, PACK_TK), lambda l, i: (l, 0, i))],
        out_specs=pl.BlockSpec((None, PACK_TK, D_PACK), lambda l, i: (l, i, 0)),
        out_shape=jax.ShapeDtypeStruct((DEPTH, D_MODEL, D_PACK), BF16),
        compiler_params=pltpu.CompilerParams(
            dimension_semantics=("parallel", "parallel"), vmem_limit_bytes=VMEM_LIMIT),
        name="pack_w_in",
    )(w_in_t)


def _outproj_kernel(*refs, n_y, final):
    y_refs, (w32_ref, x_ref), rest = refs[:n_y], refs[n_y:n_y + 2], refs[n_y + 2:]
    o_ref, w_ref = rest[-2], rest[-1]

    @pl.when(pl.program_id(0) == 0)
    def _():
        w_ref[...] = _bf(w32_ref[...])

    if n_y == 1:
        y = y_refs[0][...]
    else:
        tt = PROMPT_TT
        y = jnp.concatenate([r[k * tt:(k + 1) * tt, :] for k in range(y_refs[0].shape[0] // tt) for r in y_refs],
                            axis=0)
    x = x_ref[...] + _dot(_bf(y), w_ref[...])
    if final:
        ms = jnp.mean(x * x, axis=-1, keepdims=True)
        x = x * lax.rsqrt(ms + EPS) * rest[0][...]
    o_ref[...] = x


def _outproj(ys, w_out, x, layer, tm, final_w=None):
    n = x.shape[0]
    ys = tuple(ys) if isinstance(ys, (tuple, list)) else (ys,)

    in_specs = [pl.BlockSpec((tm // len(ys), D_MODEL), lambda i: (i, 0)) for _ in ys] + [
        pl.BlockSpec((None, D_MODEL, D_MODEL), lambda i: (layer, 0, 0), pipeline_mode=pl.Buffered(1)),
        pl.BlockSpec((tm, D_MODEL), lambda i: (i, 0)),
    ]
    args = [*ys, w_out, x]
    if final_w is not None:
        in_specs.append(pl.BlockSpec((1, D_MODEL), lambda i: (0, 0)))
        args.append(final_w)
    return pl.pallas_call(
        functools.partial(_outproj_kernel, n_y=len(ys), final=final_w is not None),
        grid=(n // tm,),
        in_specs=in_specs,
        out_specs=pl.BlockSpec((tm, D_MODEL), lambda i: (i, 0)),
        out_shape=jax.ShapeDtypeStruct((n, D_MODEL), F32),
        scratch_shapes=[pltpu.VMEM((D_MODEL, D_MODEL), BF16)],
        compiler_params=pltpu.CompilerParams(
            dimension_semantics=("arbitrary",), vmem_limit_bytes=VMEM_LIMIT_PROJ),
        name="outproj",
    )(*args)


def _conv_silu_tile(ext_ref, hist, c0, width, w_ref, wc0, b_ref=None):
    e = ext_ref[:, c0:c0 + width]
    acc = e * w_ref[0:1, wc0:wc0 + width]
    for j in range(1, CONV_K):
        acc = pltpu.roll(acc, 1, 0) + e * w_ref[j:j + 1, wc0:wc0 + width]
    acc = acc[hist:]
    if b_ref is not None:
        acc = acc + b_ref[0:1, wc0:wc0 + width]
    return _silu(acc)


def _l2norm(x):
    return x * lax.rsqrt(jnp.sum(x * x, axis=-1, keepdims=True) + EPS)


def _seg_cumsum(x, seg_len, row_in_seg):
    s = 1
    while s < seg_len:
        x = x + jnp.where(row_in_seg >= s, pltpu.roll(x, s, 0), 0.0)
        s *= 2
    return x


def _gates(small, gp_ref):
    sp = _softplus(small + gp_ref[1:2, :])
    gd = -jnp.exp(gp_ref[0:1, :]) * sp
    return _sigmoid(small), sp, gd


def _masked_decay(col, row, causal):
    return jnp.where(causal, jnp.exp(jnp.where(causal, col - row, 0.0)), 0.0)


def _gdn_finish(o, z, gnorm_ref):
    on = o * lax.rsqrt(jnp.mean(o * o, axis=-1, keepdims=True) + EPS) * gnorm_ref[0:1, :]
    return on * _silu(z)


HIST = SUBLANE
PHIST = 2 * SUBLANE


def _reset_prompt_state(sg_ref, ss_ref, eq_ref, ex_ref, ep_ref):
    sg_ref[...] = jnp.zeros(sg_ref.shape, F32)
    ss_ref[...] = jnp.zeros(ss_ref.shape, F32)
    eq_ref[0:HIST, :] = jnp.zeros((HIST, eq_ref.shape[1]), F32)
    ex_ref[0:HIST, :] = jnp.zeros((HIST, ex_ref.shape[1]), F32)
    ep_ref[0:PHIST, :] = jnp.zeros((PHIST, D_POOL), F32)


def _mixer_tile(proj_ref, y_ref, i, pos0, params, sg_ref, ss_ref, eq_ref, ex_ref, ep_ref):
    gp_ref, gcw_ref, gnorm_ref, scw_ref, scb_ref, sd_ref, snorm_ref, pw_ref, pscale_ref = params
    tt = PROMPT_TT
    n_chunks = tt // GDN_CHUNK
    hist, phist = HIST, PHIST

    eq_ref[hist:hist + tt, :] = proj_ref[:, C_QKV:C_QKV + 3 * D_GDN]
    ex_ref[hist:hist + tt, :] = proj_ref[:, C_XBC:C_XBC + SSD_CONV_CH]
    ep_ref[phist:phist + tt, :] = proj_ref[:, C_UP:C_UP + D_POOL]

    beta, sp, gd = _gates(proj_ref[:, C_SM:C_SM + LANE], gp_ref)
    row = lax.broadcasted_iota(jnp.int32, (tt, LANE), 0)
    gc = _seg_cumsum(gd, GDN_CHUNK, row & (GDN_CHUNK - 1))
    ac = _seg_cumsum(gd, SSD_CHUNK, row)

    def store_y(c0, val, r0=0):
        y_ref[r0:r0 + val.shape[0], c0:c0 + val.shape[1]] = val.astype(y_ref.dtype)

    ci = lax.broadcasted_iota(jnp.int32, (GDN_CHUNK, GDN_CHUNK), 0)
    cj = lax.broadcasted_iota(jnp.int32, (GDN_CHUNK, GDN_CHUNK), 1)
    causal_g, strict_g = ci >= cj, ci > cj
    gc_t = [gc[c * GDN_CHUNK:(c + 1) * GDN_CHUNK].T for c in range(n_chunks)]
    conv_q = functools.partial(_conv_silu_tile, eq_ref, hist)

    keys = [(c, h) for c in range(n_chunks) for h in range(GDN_HEADS)]
    nmat, pmat, rhs, qkb, qdec, kdec_t, glast = {}, {}, {}, {}, {}, {}, {}
    for h in range(GDN_HEADS):
        c0 = h * GDN_HEAD_DIM
        q_all = _l2norm(conv_q(c0, GDN_HEAD_DIM, gcw_ref, c0)) * (GDN_HEAD_DIM ** -0.5)
        k_all = _l2norm(conv_q(D_GDN + c0, GDN_HEAD_DIM, gcw_ref, D_GDN + c0))
        v_all = conv_q(2 * D_GDN + c0, GDN_HEAD_DIM, gcw_ref, 2 * D_GDN + c0)
        for c in range(n_chunks):
            rs = slice(c * GDN_CHUNK, (c + 1) * GDN_CHUNK)
            qh, kh, vh = q_all[rs], k_all[rs], v_all[rs]
            beta_c = beta[rs, L_BETA + h:L_BETA + h + 1]
            gc_c = gc[rs, L_GDN_G + h:L_GDN_G + h + 1]
            gc_r = gc_t[c][L_GDN_G + h:L_GDN_G + h + 1, :]
            gl = gc_c[GDN_CHUNK - 1:GDN_CHUNK, :]
            eg = jnp.exp(gc_c)
            decay = _masked_decay(gc_c, gc_r, causal_g)
            kb, qb = _bf(kh), _bf(qh)
            nmat[c, h] = jnp.where(strict_g, -(_dot_nt(kb, kb) * decay * beta_c), 0.0)
            qkb[c, h] = _bf(jnp.where(causal_g, _dot_nt(qb, kb) * decay, 0.0))
            rhs[c, h] = jnp.concatenate([vh * beta_c, kh * (beta_c * eg)], axis=-1)
            qdec[c, h] = qh * eg
            kdec_t[c, h] = _bf((kh * jnp.exp(gl - gc_c)).T)
            glast[c, h] = jnp.exp(gl)

    for key in keys:
        bb = _bf(nmat[key])
        pmat[key] = _dot(bb, bb)
    for _ in range(4):
        for key in keys:
            pb = _bf(pmat[key])
            r = _dot(jnp.concatenate([_bf(nmat[key]), pb], axis=0), pb)
            nmat[key] = nmat[key] + pmat[key] + r[:GDN_CHUNK]
            pmat[key] = r[GDN_CHUNK:]
    sol = {}
    for key in keys:
        nmat[key] = nmat[key] + pmat[key] + _dot(_bf(nmat[key]), _bf(pmat[key]))
    for key in keys:
        sol[key] = rhs[key] + _dot(_bf(nmat[key]), _bf(rhs[key]))

    for c in range(n_chunks):
        r0 = c * GDN_CHUNK
        for h in range(GDN_HEADS):
            key = (c, h)
            state = sg_ref[h]
            u, w = sol[key][:, :GDN_HEAD_DIM], sol[key][:, GDN_HEAD_DIM:]
            r1 = _dot(_bf(jnp.concatenate([w, qdec[key]], axis=0)), _bf(state))
            vnb = _bf(u - r1[:GDN_CHUNK])
            r2 = _dot(jnp.concatenate([qkb[key], kdec_t[key]], axis=0), vnb)
            sg_ref[h] = state * glast[key] + r2[GDN_CHUNK:]
            o = r1[GDN_CHUNK:] + r2[:GDN_CHUNK]
            c0 = h * GDN_HEAD_DIM
            z = proj_ref[r0:r0 + GDN_CHUNK, C_ZG + c0:C_ZG + c0 + GDN_HEAD_DIM]
            store_y(c0, _gdn_finish(o, z, gnorm_ref), r0)

    si = lax.broadcasted_iota(jnp.int32, (SSD_CHUNK, SSD_CHUNK), 0)
    sj = lax.broadcasted_iota(jnp.int32, (SSD_CHUNK, SSD_CHUNK), 1)
    causal_s = si >= sj
    lo_lanes = sj < SSD_HEAD_DIM
    lo_rows = si < SSD_HEAD_DIM
    ac_t = ac.T
    conv_x = functools.partial(_conv_silu_tile, ex_ref, hist)
    n_gp = SSD_HEADS_PER_GROUP // 2
    bgb, cgb, cb = {}, {}, {}
    for g in range(SSD_GROUPS):
        bgb[g] = _bf(conv_x(D_SSD + g * SSD_STATE, SSD_STATE, scw_ref, D_SSD + g * SSD_STATE, scb_ref))
        cgb[g] = _bf(conv_x(D_SSD + (SSD_GROUPS + g) * SSD_STATE, SSD_STATE, scw_ref,
                            D_SSD + (SSD_GROUPS + g) * SSD_STATE, scb_ref))
        cb[g] = _dot_nt(cgb[g], bgb[g])
    xs2, a2, y_off, y_dg, upd, st2 = {}, {}, {}, {}, {}, {}
    for p in range(SSD_HEADS // 2):
        g, hd0 = p // n_gp, 2 * p
        c0 = hd0 * SSD_HEAD_DIM
        la, lb = L_SSD_DT + hd0, L_SSD_DT + hd0 + 1
        xs2[p] = conv_x(c0, 2 * SSD_HEAD_DIM, scw_ref, c0, scb_ref)
        a2[p] = jnp.where(lo_lanes, ac[:, la:la + 1], ac[:, lb:lb + 1])
        xdt2 = xs2[p] * jnp.where(lo_lanes, sp[:, la:la + 1], sp[:, lb:lb + 1])
        st2[p] = ss_ref[hd0:hd0 + 2].reshape(2 * SSD_HEAD_DIM, SSD_STATE)
        y_off[p] = _dot_nt(cgb[g], _bf(st2[p]))
        seg_a = _masked_decay(ac[:, la:la + 1], ac_t[la:la + 1, :], causal_s)
        seg_b = _masked_decay(ac[:, lb:lb + 1], ac_t[lb:lb + 1, :], causal_s)
        y_dg[p] = (_dot(_bf(cb[g] * seg_a), _bf(jnp.where(lo_lanes, xdt2, 0.0)))
                   + _dot(_bf(cb[g] * seg_b), _bf(jnp.where(lo_lanes, 0.0, xdt2))))
        upd[p] = _dot_tn(_bf(xdt2 * jnp.exp(a2[p][tt - 1:tt, :] - a2[p])), bgb[g])
    for g in range(SSD_GROUPS):
        yz, ssq = {}, None
        for p in range(g * n_gp, (g + 1) * n_gp):
            hd0 = 2 * p
            c0 = hd0 * SSD_HEAD_DIM
            la, lb = L_SSD_DT + hd0, L_SSD_DT + hd0 + 1
            elast = jnp.exp(jnp.where(lo_rows, ac[tt - 1:tt, la:la + 1], ac[tt - 1:tt, lb:lb + 1]))
            ss_ref[hd0:hd0 + 2] = (st2[p] * elast + upd[p]).reshape(2, SSD_HEAD_DIM, SSD_STATE)
            z = proj_ref[:, C_ZS + c0:C_ZS + c0 + 2 * SSD_HEAD_DIM]
            y = jnp.exp(a2[p]) * y_off[p] + y_dg[p]
            yz[p] = (y + sd_ref[0:1, c0:c0 + 2 * SSD_HEAD_DIM] * xs2[p]) * _silu(z)
            s = jnp.sum(yz[p] * yz[p], axis=-1, keepdims=True)
            ssq = s if ssq is None else ssq + s
        scale = lax.rsqrt(ssq * (1.0 / SSD_GROUP_DIM) + EPS)
        for p, t in yz.items():
            c0 = 2 * p * SSD_HEAD_DIM
            store_y(D_GDN + c0, t * scale * snorm_ref[0:1, c0:c0 + 2 * SSD_HEAD_DIM])

    pos = pos0 + i * tt + lax.broadcasted_iota(jnp.int32, (tt, 1), 0)
    for gi, win in enumerate(POOL_WINDOWS):
        sl = slice(gi * POOL_GROUP_DIM, (gi + 1) * POOL_GROUP_DIM)
        e = ep_ref[:, sl]
        acc, k = e, 1
        while k < win:
            acc = acc + pltpu.roll(acc, k, 0)
            k *= 2
        cnt = jnp.minimum(win, pos + 1).astype(F32)
        pooled = acc[phist:] / cnt - e[phist:]
        mixed = _dot(_bf(pooled), pw_ref[gi])
        z = proj_ref[:, C_ZP + gi * POOL_GROUP_DIM:C_ZP + (gi + 1) * POOL_GROUP_DIM]
        store_y(D_GDN + D_SSD + gi * POOL_GROUP_DIM, mixed * pscale_ref[0:1, sl] * _silu(z))

    eq_ref[0:hist, :] = eq_ref[tt:tt + hist, :]
    ex_ref[0:hist, :] = ex_ref[tt:tt + hist, :]
    ep_ref[0:phist, :] = ep_ref[tt:tt + phist, :]


def _layer_spec(shape):
    nd = len(shape)
    return lambda layer: pl.BlockSpec((None,) + tuple(shape[1:]), lambda *_: (layer,) + (0,) * (nd - 1))


def _param_specs(params, layer):
    return [_layer_spec(p.shape)(layer) for p in params]


def _layer_prompt_kernel(x_ref, nw_ref, w_ref, *refs, tiles_per_seq, n_steps, pos0, n_zero):
    params = refs[:9]
    (y_odd_ref, y_even_ref, sg_out, ss_out, gcv_out, scv_out, pool_out) = refs[9:16]
    z_outs = refs[16:16 + n_zero]
    (p_even, p_odd, sg_ref, ss_ref, eq_ref, ex_ref, ep_ref) = refs[16 + n_zero:23 + n_zero]
    z_bufs, z_sem = refs[23 + n_zero:23 + 2 * n_zero], refs[23 + 2 * n_zero:]
    tt = PROMPT_TT
    g = pl.program_id(0)
    state = (sg_ref, ss_ref, eq_ref, ex_ref, ep_ref)

    def zero_copies():
        copies = []
        for z_out, z_buf in zip(z_outs, z_bufs):
            rb = z_buf.shape[0]
            per_step = z_out.shape[0] // rb // (n_steps - 1)
            assert per_step * rb * (n_steps - 1) == z_out.shape[0]
            for c in range(per_step):
                row0 = pl.multiple_of((g * per_step + c) * rb, SUBLANE)
                copies.append(pltpu.make_async_copy(z_buf, z_out.at[pl.ds(row0, rb)], z_sem[0].at[len(copies)]))
        return copies

    if n_zero:
        @pl.when(g == 0)
        def _():
            for z_buf in z_bufs:
                z_buf[...] = jnp.zeros(z_buf.shape, F32)

        @pl.when(g < n_steps - 1)
        def _():
            for cp in zero_copies():
                cp.start()

    def project(rows, p_ref):
        x = x_ref[rows, :]
        ms = jnp.mean(x * x, axis=-1, keepdims=True)
        p_ref[...] = _dot(_bf(x * lax.rsqrt(ms + EPS) * nw_ref[...]), w_ref[...])

    @pl.when(g == 0)
    def _():
        project(slice(0, tt), p_even)

    @pl.when(g > 0)
    def _():
        project(slice(0, tt), p_even)
        _mixer_tile(p_odd, y_odd_ref, lax.rem(2 * g - 1, tiles_per_seq), pos0, params, *state)
        sg_out[0] = sg_ref[...]
        ss_out[0] = ss_ref[...]
        gcv_out[0] = eq_ref[HIST + tt - (CONV_K - 1):HIST + tt, :]
        scv_out[0] = ex_ref[HIST + tt - (CONV_K - 1):HIST + tt, :]
        pool_out[0] = ep_ref[PHIST + tt - POOL_BUF:PHIST + tt, :]

    @pl.when(g < n_steps - 1)
    def _():
        i_even = lax.rem(2 * g, tiles_per_seq)

        @pl.when(i_even == 0)
        def _():
            _reset_prompt_state(*state)

        project(slice(tt, 2 * tt), p_odd)
        _mixer_tile(p_even, y_even_ref, i_even, pos0, params, *state)

    if n_zero:
        @pl.when(g < n_steps - 1)
        def _():
            for cp in zero_copies():
                cp.wait()


ZERO_BLOCK_ROWS = {GDN_HEAD_DIM: 3072, 3 * D_GDN: 48, SSD_CONV_CH: 48, D_POOL: 240}


def _layer_prompt(x, nw, w_packed, params, layer, batch, seq, zero_shapes=()):
    tt = PROMPT_TT
    tiles_per_seq = seq // tt
    n_tiles = batch * tiles_per_seq
    assert tiles_per_seq % 2 == 0
    n_steps = n_tiles // 2 + 1
    z_blocks = [(ZERO_BLOCK_ROWS[c], c) for _, c in zero_shapes]
    n_copies = sum(r // rb // (n_steps - 1) for (r, _), (rb, _) in zip(zero_shapes, z_blocks))

    def seq_of_odd_tile(g):
        return jnp.maximum(2 * g - 1, 0) // tiles_per_seq

    def state_spec(tail):
        return pl.BlockSpec((1,) + tail, lambda g: (seq_of_odd_tile(g),) + (0,) * len(tail))

    tails = ((GDN_HEADS, GDN_HEAD_DIM, GDN_HEAD_DIM), (SSD_HEADS, SSD_HEAD_DIM, SSD_STATE),
             (CONV_K - 1, 3 * D_GDN), (CONV_K - 1, SSD_CONV_CH), (POOL_BUF, D_POOL))
    return pl.pallas_call(
        functools.partial(_layer_prompt_kernel, tiles_per_seq=tiles_per_seq, n_steps=n_steps, pos0=0,
                          n_zero=len(zero_shapes)),
        grid=(n_steps,),
        in_specs=[
            pl.BlockSpec((2 * tt, D_MODEL), lambda g: (jnp.minimum(g, n_steps - 2), 0)),
            pl.BlockSpec((None, 1, D_MODEL), lambda g: (layer, 0, 0)),
            pl.BlockSpec((None, D_MODEL, D_PACK), lambda g: (layer, 0, 0), pipeline_mode=pl.Buffered(1)),
        ] + _param_specs(params, layer),
        out_specs=[
            pl.BlockSpec((tt, D_MODEL), lambda g: (jnp.maximum(g - 1, 0), 0)),
            pl.BlockSpec((tt, D_MODEL), lambda g: (jnp.minimum(g, n_steps - 2), 0)),
        ] + [state_spec(t) for t in tails] + [pl.BlockSpec(memory_space=pl.ANY)] * len(zero_shapes),
        out_shape=[jax.ShapeDtypeStruct((n_tiles // 2 * tt, D_MODEL), BF16)] * 2
        + [jax.ShapeDtypeStruct((batch,) + t, F32) for t in tails]
        + [jax.ShapeDtypeStruct(s, F32) for s in zero_shapes],
        scratch_shapes=[
            pltpu.VMEM((tt, D_PACK), F32),
            pltpu.VMEM((tt, D_PACK), F32),
            pltpu.VMEM(tails[0], F32),
            pltpu.VMEM(tails[1], F32),
            pltpu.VMEM((HIST + tt, 3 * D_GDN), F32),
            pltpu.VMEM((HIST + tt, SSD_CONV_CH), F32),
            pltpu.VMEM((PHIST + tt, D_POOL), F32),
        ] + [pltpu.VMEM(b, F32) for b in z_blocks]
        + ([pltpu.SemaphoreType.DMA((n_copies,))] if zero_shapes else []),
        compiler_params=pltpu.CompilerParams(
            dimension_semantics=("arbitrary",), vmem_limit_bytes=VMEM_LIMIT_PROJ),
        name="layer_prompt",
    )(x, nw, w_packed, *params)


def _mixer_sample_kernel(*refs, n_alias, pos0):
    proj_ref, sg_in, gcv_in, ss_in, scv_in, pool_in = refs[:6]
    (gp_ref, gcw_ref, gnorm_ref, scw_ref, scb_ref, sd_ref, snorm_ref, pw_ref, pscale_ref,
     y_ref, sg_out, gcv_out, ss_out, scv_out, pool_out) = refs[6 + n_alias:]
    nb, t_len = SAMPLE_NB, SAMPLE_T
    assert nb == SUBLANE
    rows = nb * t_len
    last = slice(rows - nb, rows)

    def tokens(c0, width):
        return [proj_ref[t, :, c0:c0 + width] for t in range(t_len)]

    def stack(parts):
        return jnp.concatenate(parts, axis=0)

    def conv(hist_ref, col0, cw, width, w_ref, b_ref=None):
        u = [hist_ref[j, :, cw:cw + width] for j in range(CONV_K - 1)] + tokens(col0 + cw, width)
        outs = []
        for t in range(t_len):
            acc = u[t] * w_ref[0:1, cw:cw + width]
            for j in range(1, CONV_K):
                acc = acc + u[t + j] * w_ref[j:j + 1, cw:cw + width]
            if b_ref is not None:
                acc = acc + b_ref[0:1, cw:cw + width]
            outs.append(_silu(acc))
        return stack(outs)

    beta, sp, gd = _gates(stack(tokens(C_SM, LANE)), gp_ref)
    parts = [gd[0:nb]]
    for t in range(1, t_len):
        parts.append(parts[-1] + gd[t * nb:(t + 1) * nb])
    cum = stack(parts)
    cum_t = cum.T

    ri = lax.broadcasted_iota(jnp.int32, (rows, rows), 0)
    rj = lax.broadcasted_iota(jnp.int32, (rows, rows), 1)
    same = (ri & (nb - 1)) == (rj & (nb - 1))
    causal, strict = same & (ri >= rj), same & (ri > rj)
    seq1 = lax.broadcasted_iota(jnp.int32, (rows, 1), 0) & (nb - 1)
    seq2 = lax.broadcasted_iota(jnp.int32, (2 * rows, 1), 0) & (nb - 1)

    def block_diag(x):
        return jnp.concatenate([jnp.where(seq1 == b, x, 0.0) for b in range(nb)], axis=1)

    def store_y(c0, val):
        for t in range(t_len):
            y_ref[t, :, c0:c0 + val.shape[1]] = val[t * nb:(t + 1) * nb]

    heads = range(GDN_HEADS)
    bmat, bb, rhs, qkb, qdec, kd_t, egl = {}, {}, {}, {}, {}, {}, {}
    for h in heads:
        c0 = h * GDN_HEAD_DIM
        qh = _l2norm(conv(gcv_in, C_QKV, c0, GDN_HEAD_DIM, gcw_ref)) * (GDN_HEAD_DIM ** -0.5)
        kh = _l2norm(conv(gcv_in, C_QKV, D_GDN + c0, GDN_HEAD_DIM, gcw_ref))
        vh = conv(gcv_in, C_QKV, 2 * D_GDN + c0, GDN_HEAD_DIM, gcw_ref)
        beta_c = beta[:, L_BETA + h:L_BETA + h + 1]
        gc_c = cum[:, L_GDN_G + h:L_GDN_G + h + 1]
        eg = jnp.exp(gc_c)
        decay = _masked_decay(gc_c, cum_t[L_GDN_G + h:L_GDN_G + h + 1, :], causal)
        kb, qb = _bf(kh), _bf(qh)
        bmat[h] = jnp.where(strict, -(_dot_nt(kb, kb) * decay * beta_c), 0.0)
        bb[h] = _bf(bmat[h])
        qkb[h] = _bf(jnp.where(causal, _dot_nt(qb, kb) * decay, 0.0))
        rhs[h] = jnp.concatenate([vh * beta_c, kh * (beta_c * eg)], axis=-1)
        qdec[h] = qh * eg
        gl = gc_c[last]
        kd_t[h] = _bf((kh * jnp.exp(stack([gl] * t_len) - gc_c)).T)
        egl[h] = jnp.exp(gl)
    p1 = {h: _dot(bb[h], bb[h]) for h in heads}
    nmat = {h: bmat[h] + p1[h] + _dot(bb[h], _bf(p1[h])) for h in heads}
    sol = {h: rhs[h] + _dot(_bf(nmat[h]), _bf(rhs[h])) for h in heads}
    acc = {}
    for h in heads:
        lhs = _bf(stack([sol[h][:, GDN_HEAD_DIM:], qdec[h]]))
        a = jnp.zeros((2 * rows, GDN_HEAD_DIM), F32)
        for b in range(nb):
            a = jnp.where(seq2 == b, _dot(lhs, _bf(sg_in[b, h])), a)
        acc[h] = a
    vn = {h: sol[h][:, :GDN_HEAD_DIM] - acc[h][:rows] for h in heads}
    o = {h: acc[h][rows:] + _dot(qkb[h], _bf(vn[h])) for h in heads}
    upd = {h: _dot(kd_t[h], _bf(block_diag(vn[h]))) for h in heads}
    for h in heads:
        c0 = h * GDN_HEAD_DIM
        store_y(c0, _gdn_finish(o[h], stack(tokens(C_ZG + c0, GDN_HEAD_DIM)), gnorm_ref))
        for b in range(nb):
            sg_out[b, h] = (sg_in[b, h] * egl[h][b:b + 1, :]
                            + upd[h][:, b * GDN_HEAD_DIM:(b + 1) * GDN_HEAD_DIM])

    lo_lanes = lax.broadcasted_iota(jnp.int32, (rows, 2 * SSD_HEAD_DIM), 1) < SSD_HEAD_DIM
    lo_rows = lax.broadcasted_iota(jnp.int32, (2 * SSD_HEAD_DIM, SSD_STATE), 0) < SSD_HEAD_DIM
    pairs = range(SSD_HEADS // 2)
    n_gp = SSD_HEADS_PER_GROUP // 2
    cgb, cb, bg_bd = {}, {}, {}
    for g in range(SSD_GROUPS):
        bg = conv(scv_in, C_XBC, D_SSD + g * SSD_STATE, SSD_STATE, scw_ref, scb_ref)
        cg = conv(scv_in, C_XBC, D_SSD + (SSD_GROUPS + g) * SSD_STATE, SSD_STATE, scw_ref, scb_ref)
        cgb[g] = _bf(cg)
        cb[g] = _dot_nt(cgb[g], _bf(bg))
        bg_bd[g] = _bf(block_diag(bg))
    xs2, a2, y, xin_t = {}, {}, {}, {}
    for p in pairs:
        g, c0 = p // n_gp, 2 * p * SSD_HEAD_DIM
        la, lb = L_SSD_DT + 2 * p, L_SSD_DT + 2 * p + 1
        xs2[p] = conv(scv_in, C_XBC, c0, 2 * SSD_HEAD_DIM, scw_ref, scb_ref)
        a2[p] = jnp.where(lo_lanes, cum[:, la:la + 1], cum[:, lb:lb + 1])
        xdt2 = xs2[p] * jnp.where(lo_lanes, sp[:, la:la + 1], sp[:, lb:lb + 1])
        seg_a = _masked_decay(cum[:, la:la + 1], cum_t[la:la + 1, :], causal)
        seg_b = _masked_decay(cum[:, lb:lb + 1], cum_t[lb:lb + 1, :], causal)
        y[p] = (_dot(_bf(cb[g] * seg_a), _bf(jnp.where(lo_lanes, xdt2, 0.0)))
                + _dot(_bf(cb[g] * seg_b), _bf(jnp.where(lo_lanes, 0.0, xdt2))))
        xin_t[p] = _bf((xdt2 * jnp.exp(stack([a2[p][last]] * t_len) - a2[p])).T)
    acc = {}
    for p in pairs:
        a = jnp.zeros((rows, 2 * SSD_HEAD_DIM), F32)
        for b in range(nb):
            state2 = ss_in[b, 2 * p:2 * p + 2].reshape(2 * SSD_HEAD_DIM, SSD_STATE)
            a = jnp.where(seq1 == b, _dot_nt(cgb[p // n_gp], _bf(state2)), a)
        acc[p] = a
    upd = {p: _dot(xin_t[p], bg_bd[p // n_gp]) for p in pairs}
    for p in pairs:
        la, lb = L_SSD_DT + 2 * p, L_SSD_DT + 2 * p + 1
        for b in range(nb):
            r = rows - nb + b
            elast = jnp.exp(jnp.where(lo_rows, cum[r:r + 1, la:la + 1], cum[r:r + 1, lb:lb + 1]))
            state2 = ss_in[b, 2 * p:2 * p + 2].reshape(2 * SSD_HEAD_DIM, SSD_STATE)
            ss_out[b, 2 * p:2 * p + 2] = (state2 * elast + upd[p][:, b * SSD_STATE:(b + 1) * SSD_STATE]).reshape(
                2, SSD_HEAD_DIM, SSD_STATE)
    for g in range(SSD_GROUPS):
        yz, ssq = {}, None
        for p in range(g * n_gp, (g + 1) * n_gp):
            c0 = 2 * p * SSD_HEAD_DIM
            z = stack(tokens(C_ZS + c0, 2 * SSD_HEAD_DIM))
            yt = y[p] + jnp.exp(a2[p]) * acc[p]
            yz[p] = (yt + sd_ref[0:1, c0:c0 + 2 * SSD_HEAD_DIM] * xs2[p]) * _silu(z)
            s = jnp.sum(yz[p] * yz[p], axis=-1, keepdims=True)
            ssq = s if ssq is None else ssq + s
        scale = lax.rsqrt(ssq * (1.0 / SSD_GROUP_DIM) + EPS)
        for p, t in yz.items():
            c0 = 2 * p * SSD_HEAD_DIM
            store_y(D_GDN + c0, t * scale * snorm_ref[0:1, c0:c0 + 2 * SSD_HEAD_DIM])

    for gi, win in enumerate(POOL_WINDOWS):
        sl = slice(gi * POOL_GROUP_DIM, (gi + 1) * POOL_GROUP_DIM)
        x = [pool_in[j, :, sl] for j in range(POOL_BUF)] + tokens(C_UP + gi * POOL_GROUP_DIM, POOL_GROUP_DIM)
        pooled = []
        for t in range(t_len):
            acc = x[POOL_BUF + t]
            for k in range(1, win):
                acc = acc + x[POOL_BUF + t - k]
            pooled.append(acc / float(min(win, pos0 + t + 1)) - x[POOL_BUF + t])
        mixed = _dot(_bf(stack(pooled)), pw_ref[gi])
        z = stack(tokens(C_ZP + gi * POOL_GROUP_DIM, POOL_GROUP_DIM))
        store_y(D_GDN + D_SSD + gi * POOL_GROUP_DIM, mixed * pscale_ref[0:1, sl] * _silu(z))

    for j in range(CONV_K - 1):
        t = t_len - (CONV_K - 1) + j
        gcv_out[j] = proj_ref[t, :, C_QKV:C_QKV + 3 * D_GDN]
        scv_out[j] = proj_ref[t, :, C_XBC:C_XBC + SSD_CONV_CH]
    for j in range(POOL_BUF):
        src = j + t_len
        pool_out[j] = pool_in[src] if src < POOL_BUF else proj_ref[src - POOL_BUF, :, C_UP:C_UP + D_POOL]


def _mixer_sample(proj, states, params, layer, n_seq, prev_outs):
    nb, t_len = SAMPLE_NB, SAMPLE_T
    assert CONV_K - 1 <= t_len and proj.shape == (t_len, n_seq, D_PACK)
    prev_outs = tuple(prev_outs)
    assert [p.shape for p in prev_outs] == [s.shape for s in states]
    n_state = len(states)

    def seq_spec(s):
        if s.shape[1] == n_seq:
            tail = tuple(s.shape[2:])
            return pl.BlockSpec((None, nb) + tail, lambda i: (layer, i) + (0,) * len(tail))
        assert s.shape[2] == n_seq and s.ndim == 4
        return pl.BlockSpec((None, s.shape[1], nb, s.shape[3]), lambda i: (layer, 0, i, 0))

    return pl.pallas_call(
        functools.partial(_mixer_sample_kernel, n_alias=len(prev_outs), pos0=PAST_LEN),
        grid=(n_seq // nb,),
        in_specs=[pl.BlockSpec((t_len, nb, D_PACK), lambda i: (0, i, 0))]
        + [seq_spec(s) for s in states]
        + [pl.BlockSpec(memory_space=pl.ANY)] * len(prev_outs)
        + _param_specs(params, layer),
        out_specs=[pl.BlockSpec((t_len, nb, D_MODEL), lambda i: (0, i, 0))] + [seq_spec(s) for s in states],
        out_shape=[jax.ShapeDtypeStruct((t_len, n_seq, D_MODEL), F32)]
        + [jax.ShapeDtypeStruct(s.shape, F32) for s in states],
        input_output_aliases={1 + n_state + k: 1 + k for k in range(len(prev_outs))},
        compiler_params=pltpu.CompilerParams(
            dimension_semantics=("parallel",), vmem_limit_bytes=VMEM_LIMIT),
        name="mixer_sample",
    )(proj, *states, *prev_outs, *params)


def _gate_params(gdn_a_log, gdn_dt_bias, ssd_a_log, ssd_dt_bias):
    def lanes(gdn, ssd):
        z = jnp.zeros((DEPTH, LANE), F32)
        z = z.at[:, L_GDN_G:L_GDN_G + GDN_HEADS].set(gdn)
        return z.at[:, L_SSD_DT:L_SSD_DT + SSD_HEADS].set(ssd)

    rows = jnp.stack([lanes(gdn_a_log, ssd_a_log), lanes(gdn_dt_bias, ssd_dt_bias)], axis=1)
    return jnp.concatenate([rows, jnp.zeros((DEPTH, SUBLANE - 2, LANE), F32)], axis=1)


def kernel(x_prompt, x_sample, state_gdn, state_gdn_conv, state_ssd, state_ssd_conv, state_pool, norm_w, w_in, gdn_conv_w, gdn_a_log, gdn_dt_bias, gdn_norm_w, ssd_conv_w, ssd_conv_b, ssd_a_log, ssd_dt_bias, ssd_d, ssd_norm_w, pool_w, pool_scale, w_out, final_norm_w):
    batch, seq, _ = x_prompt.shape
    n_seq, t_len, _ = x_sample.shape
    assert t_len == SAMPLE_T and seq % PROMPT_TT == 0 and n_seq % SAMPLE_NB == 0

    w_in_p = _pack_w_in(jnp.swapaxes(w_in, 1, 2))
    params = (
        _gate_params(gdn_a_log, gdn_dt_bias, ssd_a_log, ssd_dt_bias),
        gdn_conv_w,
        gdn_norm_w[:, None, :],
        ssd_conv_w,
        ssd_conv_b[:, None, :],
        jnp.repeat(ssd_d, SSD_HEAD_DIM, axis=-1)[:, None, :],
        ssd_norm_w[:, None, :],
        pool_w.astype(BF16),
        pool_scale[:, None, :],
    )
    nw = norm_w[:, None, :]
    fw = final_norm_w[None, :]
    states = (state_gdn, jnp.swapaxes(state_gdn_conv, 1, 2), state_ssd, jnp.swapaxes(state_ssd_conv, 1, 2),
              jnp.swapaxes(state_pool, 1, 2))

    xp = x_prompt.reshape(batch * seq, D_MODEL)
    xs = jnp.swapaxes(x_sample, 0, 1).reshape(t_len * n_seq, D_MODEL)
    new_p = [[] for _ in range(5)]
    new_s = None
    for layer in range(DEPTH):
        final = fw if layer == DEPTH - 1 else None

        zero_shapes = [(s.size // s.shape[-1], s.shape[-1]) for s in states] if layer == 0 else []
        (y_odd, y_even, gdn_p, ssd_p, gcv_p, scv_p, pool_p, *zeros) = _layer_prompt(
            xp, nw, w_in_p, params, layer, batch, seq, zero_shapes)
        xp = _outproj((y_even, y_odd), w_out, xp, layer, 512, final)
        for acc, st in zip(new_p, (gdn_p, gcv_p, ssd_p, scv_p, pool_p)):
            acc.append(st)
        if layer == 0:
            new_s = [z.reshape(s.shape) for z, s in zip(zeros, states)]

        proj_s = _inproj(xs, nw, w_in_p, layer, n_seq * t_len).reshape(t_len, n_seq, D_PACK)
        y_s, *new_s = _mixer_sample(proj_s, states, params, layer, n_seq, new_s)
        xs = _outproj(y_s.reshape(t_len * n_seq, D_MODEL), w_out, xs, layer, n_seq * t_len, final)

    new_s = [a if a.shape[1] == n_seq else jnp.swapaxes(a, 1, 2) for a in new_s]
    y_prompt = xp.reshape(batch, seq, D_MODEL)
    y_sample = jnp.swapaxes(xs.reshape(t_len, n_seq, D_MODEL), 0, 1)
    return (y_prompt, y_sample) + tuple(jnp.stack(a) for a in new_p) + tuple(new_s)
```
